```python
import jax, jax.numpy as jnp
from jax import lax
import numpy as np

D_MODEL = 2048
BATCH = 8
SEQ = 2048
DEPTH = 1

D_MIX = D_MODEL
FOURIER_WIDTH = D_MIX // 2
N_FOURIER_GROUPS = 4
FOURIER_GROUP = FOURIER_WIDTH // N_FOURIER_GROUPS
POOL_WIDTH = D_MIX - FOURIER_WIDTH
POOL_WINDOWS = (2, 4, 8, 16)
N_POOL_GROUPS = len(POOL_WINDOWS)
POOL_GROUP = POOL_WIDTH // N_POOL_GROUPS
N_BRANCHES = 2
N_EXPERTS = 16
EXPERT_FF = 1408
CAPACITY_FACTOR = 2
EPS = 1e-6

kernel_name = "hybrid_fourier_pool_ec_moe_encoder"


def rmsnorm(x, g):
    xf = x.astype(jnp.float32)
    y = xf * lax.rsqrt(jnp.mean(xf * xf, axis=-1, keepdims=True) + EPS)
    return (y * g.astype(jnp.float32)).astype(x.dtype)


def fourier_mixer(p_f, w_mix):
    B, S, _ = p_f.shape
    z = p_f.reshape(B, S, N_FOURIER_GROUPS, FOURIER_GROUP).astype(jnp.float32)
    z = jnp.fft.fftn(z, axes=(1, 3), norm="ortho").real.astype(p_f.dtype)
    y = jnp.einsum("bsgc,gcd->bsgd", z, w_mix)
    return y.reshape(B, S, FOURIER_WIDTH)


def pool_mixer(p_p, w_mix, scale):
    B, S, _ = p_p.shape
    z = p_p.reshape(B, S, N_POOL_GROUPS, POOL_GROUP).astype(jnp.float32)
    csum = jnp.concatenate([jnp.zeros((B, 1, N_POOL_GROUPS, POOL_GROUP), jnp.float32),
                            jnp.cumsum(z, axis=1)], axis=1)
    half = jnp.array([w // 2 for w in POOL_WINDOWS], jnp.int32)
    pos = jnp.arange(S, dtype=jnp.int32)[:, None]
    lo = jnp.clip(pos - half[None, :], 0, S)
    hi = jnp.clip(pos + half[None, :], 0, S)
    g_idx = jnp.arange(N_POOL_GROUPS, dtype=jnp.int32)[None, :]
    win_sum = csum[:, hi, g_idx] - csum[:, lo, g_idx]
    count = (hi - lo).astype(jnp.float32)[None, :, :, None]
    pooled = (win_sum / count - z).astype(p_p.dtype)
    y = jnp.einsum("bsgc,gcd->bsgd", pooled, w_mix)
    return y.reshape(B, S, POOL_WIDTH) * scale


def mixer_block(u, w_in, w_fourier_mix, w_pool_mix, pool_scale, w_branch_f, w_branch_p,
                w_gate, b_gate, w_out):
    p = jnp.einsum("bsd,dm->bsm", u, w_in)
    y_f = jnp.einsum("bsm,md->bsd", fourier_mixer(p[..., :FOURIER_WIDTH], w_fourier_mix), w_branch_f)
    y_p = jnp.einsum("bsm,md->bsd", pool_mixer(p[..., FOURIER_WIDTH:], w_pool_mix, pool_scale), w_branch_p)
    gates = jax.nn.sigmoid((jnp.einsum("bsd,dk->bsk", u, w_gate) + b_gate).astype(jnp.float32)).astype(u.dtype)
    g_f, g_p = gates[..., :D_MODEL], gates[..., D_MODEL:]
    merged = g_f * y_f + g_p * y_p
    return jnp.einsum("bsd,de->bse", merged, w_out)


def expert_choice_moe(u, w_router, w_gate_e, w_up_e, w_down_e):
    B, S, D = u.shape
    cap = CAPACITY_FACTOR * S // N_EXPERTS
    logits = jnp.einsum("bsd,de->bse", u, w_router).astype(jnp.float32)
    affinity = jax.nn.softmax(logits, axis=-1)
    gate, idx = lax.top_k(jnp.swapaxes(affinity, 1, 2), cap)
    xs = jax.vmap(lambda ub, ib: ub[ib])(u, idx)
    h = jax.nn.silu(jnp.einsum("becd,edf->becf", xs, w_gate_e)) * jnp.einsum("becd,edf->becf", xs, w_up_e)
    y = jnp.einsum("becf,efd->becd", h, w_down_e) * gate[..., None].astype(u.dtype)
    combine = lambda ib, yb: jnp.zeros((S, D), yb.dtype).at[ib.reshape(-1)].add(yb.reshape(-1, D))
    return jax.vmap(combine)(idx, y)


def setup_inputs(seed: int = 0) -> dict:
    key = jax.random.key(seed)
    ks = jax.random.split(key, 20)
    f32 = jnp.float32
    nrm = lambda k, shape, fan_in: jax.random.normal(k, shape, f32) * (fan_in ** -0.5)
    L = DEPTH
    return {
        "x": jax.random.normal(ks[0], (BATCH, SEQ, D_MODEL), f32),
        "norm_mix_g": 1.0 + 0.02 * jax.random.normal(ks[1], (L, D_MODEL), f32),
        "w_in": nrm(ks[2], (L, D_MODEL, D_MIX), D_MODEL),
        "w_fourier_mix": nrm(ks[3], (L, N_FOURIER_GROUPS, FOURIER_GROUP, FOURIER_GROUP), FOURIER_GROUP),
        "w_pool_mix": nrm(ks[4], (L, N_POOL_GROUPS, POOL_GROUP, POOL_GROUP), POOL_GROUP),
        "pool_scale": 1.0 + 0.02 * jax.random.normal(ks[5], (L, POOL_WIDTH), f32),
        "w_branch_f": nrm(ks[6], (L, FOURIER_WIDTH, D_MODEL), FOURIER_WIDTH),
        "w_branch_p": nrm(ks[7], (L, POOL_WIDTH, D_MODEL), POOL_WIDTH),
        "w_gate": nrm(ks[8], (L, D_MODEL, N_BRANCHES * D_MODEL), D_MODEL),
        "b_gate": 0.02 * jax.random.normal(ks[9], (L, N_BRANCHES * D_MODEL), f32),
        "w_out": nrm(ks[10], (L, D_MODEL, D_MODEL), D_MODEL),
        "norm_moe_g": 1.0 + 0.02 * jax.random.normal(ks[11], (L, D_MODEL), f32),
        "w_router": nrm(ks[12], (L, D_MODEL, N_EXPERTS), D_MODEL),
        "w_expert_gate": nrm(ks[13], (L, N_EXPERTS, D_MODEL, EXPERT_FF), D_MODEL),
        "w_expert_up": nrm(ks[14], (L, N_EXPERTS, D_MODEL, EXPERT_FF), D_MODEL),
        "w_expert_down": nrm(ks[15], (L, N_EXPERTS, EXPERT_FF, D_MODEL), EXPERT_FF),
        "norm_final_g": 1.0 + 0.02 * jax.random.normal(ks[16], (D_MODEL,), f32),
    }


def reference(x, norm_mix_g, w_in, w_fourier_mix, w_pool_mix, pool_scale, w_branch_f, w_branch_p,
              w_gate, b_gate, w_out, norm_moe_g, w_router, w_expert_gate, w_expert_up,
              w_expert_down, norm_final_g):
    h = x
    for l in range(DEPTH):
        u = rmsnorm(h, norm_mix_g[l])
        h = h + mixer_block(u, w_in[l], w_fourier_mix[l], w_pool_mix[l], pool_scale[l],
                            w_branch_f[l], w_branch_p[l], w_gate[l], b_gate[l], w_out[l])
        v = rmsnorm(h, norm_moe_g[l])
        h = h + expert_choice_moe(v, w_router[l], w_expert_gate[l], w_expert_up[l], w_expert_down[l])
    return rmsnorm(h, norm_final_g)
```

```python
import functools
import math

import jax
import jax.numpy as jnp
from jax import lax
from jax.experimental import pallas as pl
from jax.experimental.pallas import tpu as pltpu

F32 = jnp.float32
BF16 = jnp.bfloat16

D_MODEL = 2048
FOURIER_WIDTH = 1024
N_GROUPS = 4
GROUP = 256
POOL_HALF = (1, 2, 4, 8)
N_EXPERTS = 16
EXPERT_FF = 1408
CAPACITY_FACTOR = 2
EPS = 1e-6

V7X_VMEM_LIMIT_BYTES = 58 * 1024 * 1024


def _params(sem, vmem=V7X_VMEM_LIMIT_BYTES):
    return pltpu.CompilerParams(dimension_semantics=sem, vmem_limit_bytes=vmem)


def _const_spec(shape):
    nd = len(shape)
    return pl.BlockSpec(shape, lambda *_: (0,) * nd, pipeline_mode=pl.Buffered(1))


PROJ_TM = 512
PROJ_NC = 512


def _proj_body(x_ref, g_ref, win_ref, wg_ref, bg_ref, p_ref, gate_ref):
    x = x_ref[...]
    ms = jnp.mean(x * x, axis=-1, keepdims=True)
    u = (x * lax.rsqrt(ms + EPS) * g_ref[...]).astype(BF16)
    for j in range(win_ref.shape[1] // PROJ_NC):
        sl = slice(j * PROJ_NC, (j + 1) * PROJ_NC)
        p_ref[:, sl] = jnp.dot(u, win_ref[:, sl], preferred_element_type=F32).astype(BF16)
    for j in range(wg_ref.shape[1] // PROJ_NC):
        sl = slice(j * PROJ_NC, (j + 1) * PROJ_NC)
        a = jnp.dot(u, wg_ref[:, sl], preferred_element_type=F32) + bg_ref[:, sl]
        gate_ref[:, sl] = jax.nn.sigmoid(a).astype(BF16)


def _proj(x2, g, w_in, w_gate, b_gate):
    t, d = x2.shape
    dm, dg = w_in.shape[1], w_gate.shape[1]
    return pl.pallas_call(
        _proj_body,
        grid=(t // PROJ_TM,),
        in_specs=[
            pl.BlockSpec((PROJ_TM, d), lambda i: (i, 0)),
            _const_spec((1, d)),
            _const_spec((d, dm)),
            _const_spec((d, dg)),
            _const_spec((1, dg)),
        ],
        out_specs=[
            pl.BlockSpec((PROJ_TM, dm), lambda i: (i, 0)),
            pl.BlockSpec((PROJ_TM, dg), lambda i: (i, 0)),
        ],
        out_shape=[jax.ShapeDtypeStruct((t, dm), BF16), jax.ShapeDtypeStruct((t, dg), BF16)],
        compiler_params=_params(("parallel",)),
        name="proj",
    )(x2, g, w_in, w_gate, b_gate)


SEQ_MC = 512
N_SEQ_TASKS = 3 * N_GROUPS


def _seqmix_body(m_ref, p_ref, o_ref):
    p = p_ref[...]
    for i in range(m_ref.shape[0] // SEQ_MC):
        sl = slice(i * SEQ_MC, (i + 1) * SEQ_MC)
        o_ref[sl, :] = jnp.dot(m_ref[sl, :], p, preferred_element_type=F32).astype(BF16)


def _seqmix(mats, p3):
    b, s, _ = p3.shape
    mat_idx = lambda t, i: (jnp.where(t < 2 * N_GROUPS, t // N_GROUPS, t - 2 * N_GROUPS + 2), 0, 0)
    col_idx = lambda t, i: (i, 0, jnp.where(t < N_GROUPS, t, t - N_GROUPS))
    return pl.pallas_call(
        _seqmix_body,
        grid=(N_SEQ_TASKS, b),
        in_specs=[
            pl.BlockSpec((None, s, s), mat_idx),
            pl.BlockSpec((None, s, GROUP), col_idx),
        ],
        out_specs=pl.BlockSpec((None, s, GROUP), lambda t, i: (i, 0, t)),
        out_shape=jax.ShapeDtypeStruct((b, s, N_SEQ_TASKS * GROUP), BF16),
        compiler_params=_params(("arbitrary", "arbitrary")),
        name="seqmix",
    )(mats, p3)


TAIL_TM = 256


def _tail_body(q_ref, gate_ref, x_ref, cc_ref, sc_ref, wf_ref, wp_ref, ps_ref, wbf_ref, wbp_ref,
               wout_ref, gm_ref, wr_ref, h_ref, v_ref, lg_ref):
    yf, yp = [], []
    for g in range(N_GROUPS):
        pc = q_ref[:, g * GROUP:(g + 1) * GROUP]
        ps = q_ref[:, (N_GROUPS + g) * GROUP:(N_GROUPS + g + 1) * GROUP]
        pooled = q_ref[:, (2 * N_GROUPS + g) * GROUP:(2 * N_GROUPS + g + 1) * GROUP]
        z = (jnp.dot(pc, cc_ref[...], preferred_element_type=F32)
             - jnp.dot(ps, sc_ref[...], preferred_element_type=F32))
        yf.append(jnp.dot(z.astype(BF16), wf_ref[g], preferred_element_type=F32).astype(BF16))
        ypg = jnp.dot(pooled, wp_ref[g], preferred_element_type=F32) * ps_ref[:, g * GROUP:(g + 1) * GROUP]
        yp.append(ypg.astype(BF16))
    yf = jnp.concatenate(yf, axis=-1)
    yp = jnp.concatenate(yp, axis=-1)
    d = wout_ref.shape[0]
    bf = jnp.dot(yf, wbf_ref[...], preferred_element_type=F32)
    bp = jnp.dot(yp, wbp_ref[...], preferred_element_type=F32)
    merged = gate_ref[:, :d].astype(F32) * bf + gate_ref[:, d:].astype(F32) * bp
    h = x_ref[...] + jnp.dot(merged.astype(BF16), wout_ref[...], preferred_element_type=F32)
    h_ref[...] = h
    ms = jnp.mean(h * h, axis=-1, keepdims=True)
    v = h * lax.rsqrt(ms + EPS) * gm_ref[...]
    v_hi = v.astype(BF16)
    v_lo = (v - v_hi.astype(F32)).astype(BF16)
    v_ref[...] = v_hi
    acc = (jnp.dot(v_hi, wr_ref[...], preferred_element_type=F32)
           + jnp.dot(v_lo, wr_ref[...], preferred_element_type=F32))
    lg_ref[...] = acc[:, :N_EXPERTS] + acc[:, N_EXPERTS:]


def _tail(q2, gates, x2, cc, sc, wf, wp, pscale, wbf, wbp, wout, gm, wr2):
    t, d = x2.shape
    row = lambda width: pl.BlockSpec((TAIL_TM, width), lambda i: (i, 0))
    return pl.pallas_call(
        _tail_body,
        grid=(t // TAIL_TM,),
        in_specs=[
            row(q2.shape[1]), row(gates.shape[1]), row(d),
            _const_spec(cc.shape), _const_spec(sc.shape), _const_spec(wf.shape), _const_spec(wp.shape),
            _const_spec(pscale.shape), _const_spec(wbf.shape), _const_spec(wbp.shape),
            _const_spec(wout.shape), _const_spec(gm.shape), _const_spec(wr2.shape),
        ],
        out_specs=[row(d), row(d), row(N_EXPERTS)],
        out_shape=[jax.ShapeDtypeStruct((t, d), F32), jax.ShapeDtypeStruct((t, d), BF16),
                   jax.ShapeDtypeStruct((t, N_EXPERTS), F32)],
        compiler_params=_params(("parallel",)),
        name="tail",
    )(q2, gates, x2, cc, sc, wf, wp, pscale, wbf, wbp, wout, gm, wr2)


def _route_body(lg_ref, tri_ref, aff_ref, sel_ref, *, cap):
    nb, ne, s = lg_ref.shape
    lg = lg_ref[...]
    mx = jnp.max(lg, axis=1, keepdims=True)
    ex = jnp.exp(lg - mx)
    aff = ex / jnp.sum(ex, axis=1, keepdims=True)
    aff_ref[...] = aff
    aff2 = aff.reshape(nb * ne, s)

    def step(i, bits):
        cand = bits | jnp.left_shift(jnp.int32(1), 30 - i)
        n_ge = jnp.sum(jnp.where(aff2 >= pltpu.bitcast(cand, F32), 1.0, 0.0), axis=1, keepdims=True)
        return jnp.where(n_ge >= cap, cand, bits)

    thr = pltpu.bitcast(lax.fori_loop(0, 31, step, jnp.zeros((nb * ne, 1), jnp.int32)), F32)
    above = aff2 > thr
    tie = aff2 == thr
    n_above = jnp.sum(jnp.where(above, 1.0, 0.0), axis=1, keepdims=True)
    tri = tri_ref[...]
    tie_rank = jnp.dot(jnp.where(tie, 1.0, 0.0).astype(BF16), tri, preferred_element_type=F32)
    chosen = above | (tie & (tie_rank <= (cap - n_above)))
    slot = jnp.dot(jnp.where(chosen, 1.0, 0.0).astype(BF16), tri, preferred_element_type=F32) - 1.0
    sel = jnp.where(chosen, slot, -1.0).astype(jnp.int32)
    sel_ref[...] = sel.reshape(nb, ne, s)


def _route(lg_t, tri, cap):
    nb, ne, s = lg_t.shape
    return pl.pallas_call(
        functools.partial(_route_body, cap=cap),
        out_shape=[jax.ShapeDtypeStruct((nb, ne, s), F32), jax.ShapeDtypeStruct((nb, ne, s), jnp.int32)],
        compiler_params=pltpu.CompilerParams(vmem_limit_bytes=V7X_VMEM_LIMIT_BYTES),
        name="route",
    )(lg_t, tri)


def _gather_body(sel_ref, aff_ref, v_ref, xs_ref, gs_ref, *, cap):
    s = sel_ref.shape[-1]
    sel = sel_ref[...]
    hit = lax.broadcasted_iota(jnp.int32, (cap, s), 0) == sel
    onehot = jnp.where(hit, 1.0, 0.0).astype(BF16)
    xs_ref[...] = jnp.dot(onehot, v_ref[...], preferred_element_type=F32).astype(BF16)
    gs_ref[...] = jnp.sum(jnp.where(hit, aff_ref[...], 0.0), axis=1, keepdims=True)


def _gather(sel4, aff4, v3, cap):
    nb, ne, _, s = sel4.shape
    d = v3.shape[-1]
    return pl.pallas_call(
        functools.partial(_gather_body, cap=cap),
        grid=(nb, ne),
        in_specs=[
            pl.BlockSpec((None, None, 1, s), lambda b, e: (b, e, 0, 0)),
            pl.BlockSpec((None, None, 1, s), lambda b, e: (b, e, 0, 0)),
            pl.BlockSpec((None, s, d), lambda b, e: (b, 0, 0)),
        ],
        out_specs=[
            pl.BlockSpec((None, cap, d), lambda b, e: (e, b, 0)),
            pl.BlockSpec((None, cap, 1), lambda b, e: (e, b, 0)),
        ],
        out_shape=[jax.ShapeDtypeStruct((ne, nb * cap, d), BF16),
                   jax.ShapeDtypeStruct((ne, nb * cap, 1), F32)],
        compiler_params=_params(("arbitrary", "arbitrary")),
        name="gather",
    )(sel4, aff4, v3)


FFN_TM = 1024
FFN_TF = 256
FFN_TAIL = EXPERT_FF % FFN_TF
FFN_NFULL = EXPERT_FF // FFN_TF


def _ffn_body(xs_ref, gs_ref, wg_ref, wu_ref, wd_ref, wgt_ref, wut_ref, wdt_ref, y_ref, acc_ref):
    f = pl.program_id(2)

    def partial_out(wg, wu, wd):
        xs = xs_ref[...]
        a = jnp.dot(xs, wg[...].astype(BF16), preferred_element_type=F32)
        u = jnp.dot(xs, wu[...].astype(BF16), preferred_element_type=F32)
        hid = (jax.nn.silu(a) * u).astype(BF16)
        return jnp.dot(hid, wd[...].astype(BF16), preferred_element_type=F32)

    @pl.when(f == 0)
    def _():
        acc_ref[...] = partial_out(wg_ref, wu_ref, wd_ref)

    @pl.when((f > 0) & (f < FFN_NFULL))
    def _():
        acc_ref[...] += partial_out(wg_ref, wu_ref, wd_ref)

    @pl.when(f == FFN_NFULL)
    def _():
        y = acc_ref[...] + partial_out(wgt_ref, wut_ref, wdt_ref)
        y_ref[...] = (y * gs_ref[...]).astype(BF16)


def _ffn(xs, gs, w_gate_e, w_up_e, w_down_e):
    ne, m, d = xs.shape
    ff = w_gate_e.shape[-1]
    assert ff == FFN_NFULL * FFN_TF + FFN_TAIL and FFN_TAIL % 128 == 0 and FFN_TAIL > 0
    tail_blk = ff // FFN_TAIL - 1
    full = lambda e, i, f: jnp.minimum(f, FFN_NFULL - 1)
    return pl.pallas_call(
        _ffn_body,
        grid=(ne, m // FFN_TM, FFN_NFULL + 1),
        in_specs=[
            pl.BlockSpec((None, FFN_TM, d), lambda e, i, f: (e, i, 0)),
            pl.BlockSpec((None, FFN_TM, 1), lambda e, i, f: (e, i, 0)),
            pl.BlockSpec((None, d, FFN_TF), lambda e, i, f: (e, 0, full(e, i, f))),
            pl.BlockSpec((None, d, FFN_TF), lambda e, i, f: (e, 0, full(e, i, f))),
            pl.BlockSpec((None, FFN_TF, d), lambda e, i, f: (e, full(e, i, f), 0)),
            pl.BlockSpec((None, d, FFN_TAIL), lambda e, i, f: (e, 0, tail_blk)),
            pl.BlockSpec((None, d, FFN_TAIL), lambda e, i, f: (e, 0, tail_blk)),
            pl.BlockSpec((None, FFN_TAIL, d), lambda e, i, f: (e, tail_blk, 0)),
        ],
        out_specs=pl.BlockSpec((None, FFN_TM, d), lambda e, i, f: (e, i, 0)),
        out_shape=jax.ShapeDtypeStruct((ne, m, d), BF16),
        scratch_shapes=[pltpu.VMEM((FFN_TM, d), F32)],
        compiler_params=_params(("arbitrary", "arbitrary", "arbitrary")),
        name="ffn",
    )(xs, gs, w_gate_e, w_up_e, w_down_e, w_gate_e, w_up_e, w_down_e)


COMB_TS = 512


def _combine_body(sel_ref, y_ref, h_ref, g_ref, o_ref, acc_ref, *, cap):
    e = pl.program_id(2)

    @pl.when(e == 0)
    def _():
        acc_ref[...] = h_ref[...]

    sel = sel_ref[...]
    hit = lax.broadcasted_iota(jnp.int32, (sel.shape[0], cap), 1) == sel
    onehot = jnp.where(hit, 1.0, 0.0).astype(BF16)
    acc_ref[...] += jnp.dot(onehot, y_ref[...], preferred_element_type=F32)

    @pl.when(e == pl.num_programs(2) - 1)
    def _():
        h = acc_ref[...]
        ms = jnp.mean(h * h, axis=-1, keepdims=True)
        o_ref[...] = h * lax.rsqrt(ms + EPS) * g_ref[...]


def _combine(sel_col, ys, h3, g_final, cap):
    nb, ne, s, _ = sel_col.shape
    d = h3.shape[-1]
    return pl.pallas_call(
        functools.partial(_combine_body, cap=cap),
        grid=(nb, s // COMB_TS, ne),
        in_specs=[
            pl.BlockSpec((None, None, COMB_TS, 1), lambda b, i, e: (b, e, i, 0)),
            pl.BlockSpec((None, cap, d), lambda b, i, e: (e, b, 0)),
            pl.BlockSpec((None, COMB_TS, d), lambda b, i, e: (b, i, 0)),
            pl.BlockSpec((1, d), lambda b, i, e: (0, 0)),
        ],
        out_specs=pl.BlockSpec((None, COMB_TS, d), lambda b, i, e: (b, i, 0)),
        out_shape=jax.ShapeDtypeStruct((nb, s, d), F32),
        scratch_shapes=[pltpu.VMEM((COMB_TS, d), F32)],
        compiler_params=_params(("arbitrary", "arbitrary", "arbitrary")),
        name="combine",
    )(sel_col, ys, h3, g_final)


def _dft_tables(n):
    j = lax.iota(jnp.int32, n)
    m = (j[:, None] * j[None, :]) % n
    ang = m.astype(F32) * (2.0 * math.pi / n)
    scale = 1.0 / math.sqrt(n)
    return jnp.cos(ang) * scale, jnp.sin(ang) * scale


def _pool_tables(s):
    i = lax.iota(jnp.int32, s)[:, None]
    k = lax.iota(jnp.int32, s)[None, :]
    out = []
    for half in POOL_HALF:
        lo = jnp.clip(i - half, 0, s)
        hi = jnp.clip(i + half, 0, s)
        inside = (k >= lo) & (k < hi)
        cnt = (hi - lo).astype(F32)
        out.append(jnp.where(inside, 1.0 / cnt, 0.0) - jnp.where(i == k, 1.0, 0.0))
    return jnp.stack(out)


def kernel(x, norm_mix_g, w_in, w_fourier_mix, w_pool_mix, pool_scale, w_branch_f, w_branch_p, w_gate,
           b_gate, w_out, norm_moe_g, w_router, w_expert_gate, w_expert_up, w_expert_down, norm_final_g):
    nb, s, d = x.shape
    depth = w_in.shape[0]
    cap = CAPACITY_FACTOR * s // N_EXPERTS
    t = nb * s

    cos_s, sin_s = _dft_tables(s)
    mats = jnp.concatenate([cos_s[None], sin_s[None], _pool_tables(s)], axis=0).astype(BF16)
    cos_c, sin_c = _dft_tables(GROUP)
    cos_c, sin_c = cos_c.astype(BF16), sin_c.astype(BF16)
    tri = (lax.iota(jnp.int32, s)[:, None] <= lax.iota(jnp.int32, s)[None, :]).astype(BF16)

    assert depth == 1, "single-layer block only"
    l = 0
    x2 = x.reshape(t, d)
    p, gates = _proj(x2, norm_mix_g[l][None], w_in[l].astype(BF16), w_gate[l].astype(BF16), b_gate[l][None])
    q = _seqmix(mats, p.reshape(nb, s, -1))
    wr = w_router[l]
    wr_hi = wr.astype(BF16)
    wr_lo = (wr - wr_hi.astype(F32)).astype(BF16)
    h2, v, logits = _tail(
        q.reshape(t, -1), gates, x2, cos_c, sin_c,
        w_fourier_mix[l].astype(BF16), w_pool_mix[l].astype(BF16), pool_scale[l][None],
        w_branch_f[l].astype(BF16), w_branch_p[l].astype(BF16), w_out[l].astype(BF16),
        norm_moe_g[l][None], jnp.concatenate([wr_hi, wr_lo], axis=1))
    lg_t = jnp.swapaxes(logits.reshape(nb, s, N_EXPERTS), 1, 2)
    aff, sel = _route(lg_t, tri, cap)
    xs, gs = _gather(sel[:, :, None, :], aff[:, :, None, :], v.reshape(nb, s, d), cap)
    ys = _ffn(xs, gs, w_expert_gate[l], w_expert_up[l], w_expert_down[l])
    return _combine(sel[..., None], ys, h2.reshape(nb, s, d), norm_final_g[None], cap)
```

```python
import functools
import math

import jax
import jax.numpy as jnp
from jax import lax
from jax.experimental import pallas as pl
from jax.experimental.pallas import tpu as pltpu

F32 = jnp.float32
BF16 = jnp.bfloat16
I32 = jnp.int32

D_MODEL = 2048
FOURIER_WIDTH = 1024
N_GROUPS = 4
GROUP = 256
POOL_HALF = (1, 2, 4, 8)
N_EXPERTS = 16
EXPERT_FF = 1408
CAPACITY_FACTOR = 2
EPS = 1e-6

V7X_VMEM_LIMIT_BYTES = 58 * 1024 * 1024
BF16_SUBLANES = 16


def _params(sem, vmem=V7X_VMEM_LIMIT_BYTES):
    return pltpu.CompilerParams(dimension_semantics=sem, vmem_limit_bytes=vmem)


def _const_spec(shape):
    nd = len(shape)
    return pl.BlockSpec(shape, lambda *_: (0,) * nd, pipeline_mode=pl.Buffered(1))


PROJ_TM = 512
PROJ_NC = 512


def _proj_body(x_ref, g_ref, win_ref, wg_ref, bg_ref, p_ref, gate_ref):
    x = x_ref[...]
    ms = jnp.mean(x * x, axis=-1, keepdims=True)
    u = (x * lax.rsqrt(ms + EPS) * g_ref[...]).astype(BF16)
    for j in range(win_ref.shape[1] // PROJ_NC):
        sl = slice(j * PROJ_NC, (j + 1) * PROJ_NC)
        p_ref[:, sl] = jnp.dot(u, win_ref[:, sl], preferred_element_type=F32).astype(BF16)
    for j in range(wg_ref.shape[1] // PROJ_NC):
        sl = slice(j * PROJ_NC, (j + 1) * PROJ_NC)
        a = jnp.dot(u, wg_ref[:, sl], preferred_element_type=F32) + bg_ref[:, sl]
        gate_ref[:, sl] = jax.nn.sigmoid(a).astype(BF16)


def _proj(x2, g, w_in, w_gate, b_gate):
    t, d = x2.shape
    dm, dg = w_in.shape[1], w_gate.shape[1]
    return pl.pallas_call(
        _proj_body,
        grid=(t // PROJ_TM,),
        in_specs=[
            pl.BlockSpec((PROJ_TM, d), lambda i: (i, 0)),
            _const_spec((1, d)),
            _const_spec((d, dm)),
            _const_spec((d, dg)),
            _const_spec((1, dg)),
        ],
        out_specs=[
            pl.BlockSpec((PROJ_TM, dm), lambda i: (i, 0)),
            pl.BlockSpec((PROJ_TM, dg), lambda i: (i, 0)),
        ],
        out_shape=[jax.ShapeDtypeStruct((t, dm), BF16), jax.ShapeDtypeStruct((t, dg), BF16)],
        compiler_params=_params(("parallel",)),
        name="proj",
    )(x2, g, w_in, w_gate, b_gate)


DFT_MC = 512


def _fourier_body(trig_ref, pe_ref, po_ref, pc_ref, ps_ref):
    half = trig_ref.shape[1]
    pe = pe_ref[...]
    po = po_ref[...]
    for i in range(half // DFT_MC):
        lo = slice(i * DFT_MC, (i + 1) * DFT_MC)
        hi = slice(half + i * DFT_MC, half + (i + 1) * DFT_MC)
        for tab, out in ((0, pc_ref), (2, ps_ref)):
            ev = jnp.dot(trig_ref[tab, lo, :], pe, preferred_element_type=F32)
            od = jnp.dot(trig_ref[tab + 1, lo, :], po, preferred_element_type=F32)
            out[lo, :] = (ev + od).astype(BF16)
            out[hi, :] = (ev - od).astype(BF16)


def _fourier(trig, p_pairs, s):
    b, half, two_dm = p_pairs.shape
    odd0 = (two_dm // 2) // GROUP
    out = jax.ShapeDtypeStruct((b, s, FOURIER_WIDTH), BF16)
    return pl.pallas_call(
        _fourier_body,
        grid=(b, FOURIER_WIDTH // GROUP),
        in_specs=[
            _const_spec(trig.shape),
            pl.BlockSpec((None, half, GROUP), lambda i, g: (i, 0, g)),
            pl.BlockSpec((None, half, GROUP), lambda i, g: (i, 0, odd0 + g)),
        ],
        out_specs=[pl.BlockSpec((None, s, GROUP), lambda i, g: (i, 0, g))] * 2,
        out_shape=[out, out],
        compiler_params=_params(("arbitrary", "arbitrary")),
        name="fourier",
    )(trig, p_pairs, p_pairs)


POOL_TM = 256
POOL_HALO = 128


def _pool_body(a_ref, p_ref, o_ref):
    s = a_ref.shape[0]
    for i in range(s // POOL_TM):
        rows = slice(i * POOL_TM, (i + 1) * POOL_TM)
        band = slice(max(i * POOL_TM - POOL_HALO, 0), min((i + 1) * POOL_TM + POOL_HALO, s))
        o_ref[rows, :] = jnp.dot(a_ref[rows, band], p_ref[band, :], preferred_element_type=F32).astype(BF16)


def _pool(pool_mats, p3):
    b, s, dm = p3.shape
    first = FOURIER_WIDTH // GROUP
    return pl.pallas_call(
        _pool_body,
        grid=(N_GROUPS, b),
        in_specs=[
            pl.BlockSpec((None, s, s), lambda g, i: (g, 0, 0)),
            pl.BlockSpec((None, s, GROUP), lambda g, i: (i, 0, first + g)),
        ],
        out_specs=pl.BlockSpec((None, s, GROUP), lambda g, i: (i, 0, g)),
        out_shape=jax.ShapeDtypeStruct((b, s, dm - FOURIER_WIDTH), BF16),
        compiler_params=_params(("arbitrary", "arbitrary")),
        name="pool",
    )(pool_mats, p3)


TAIL_TM = 256


def _tail_body(pc_ref, ps_ref, pl_ref, gate_ref, x_ref, cc_ref, sc_ref, wf_ref, wp_ref, psc_ref, wbf_ref, wbp_ref,
               wout_ref, gm_ref, wr_ref, h_ref, v_ref, lg_ref):
    yf, yp = [], []
    for g in range(N_GROUPS):
        cols = slice(g * GROUP, (g + 1) * GROUP)
        z = (jnp.dot(pc_ref[:, cols], cc_ref[...], preferred_element_type=F32)
             - jnp.dot(ps_ref[:, cols], sc_ref[...], preferred_element_type=F32))
        yf.append(jnp.dot(z.astype(BF16), wf_ref[g], preferred_element_type=F32).astype(BF16))
        ypg = jnp.dot(pl_ref[:, cols], wp_ref[g], preferred_element_type=F32) * psc_ref[:, cols]
        yp.append(ypg.astype(BF16))
    yf = jnp.concatenate(yf, axis=-1)
    yp = jnp.concatenate(yp, axis=-1)
    d = wout_ref.shape[0]
    bf = jnp.dot(yf, wbf_ref[...], preferred_element_type=F32)
    bp = jnp.dot(yp, wbp_ref[...], preferred_element_type=F32)
    merged = gate_ref[:, :d].astype(F32) * bf + gate_ref[:, d:].astype(F32) * bp
    h = x_ref[...] + jnp.dot(merged.astype(BF16), wout_ref[...], preferred_element_type=F32)
    h_ref[...] = h
    ms = jnp.mean(h * h, axis=-1, keepdims=True)
    v = h * lax.rsqrt(ms + EPS) * gm_ref[...]
    v_hi = v.astype(BF16)
    v_lo = (v - v_hi.astype(F32)).astype(BF16)
    v_ref[...] = v_hi
    acc = (jnp.dot(v_hi, wr_ref[...], preferred_element_type=F32)
           + jnp.dot(v_lo, wr_ref[...], preferred_element_type=F32))
    lg_ref[...] = acc[:, :N_EXPERTS] + acc[:, N_EXPERTS:]


def _tail(pc2, ps2, pl2, gates, x2, cc, sc, wf, wp, pscale, wbf, wbp, wout, gm, wr2):
    t, d = x2.shape
    row = lambda width: pl.BlockSpec((TAIL_TM, width), lambda i: (i, 0))
    return pl.pallas_call(
        _tail_body,
        grid=(t // TAIL_TM,),
        in_specs=[
            row(pc2.shape[1]), row(ps2.shape[1]), row(pl2.shape[1]), row(gates.shape[1]), row(d),
            _const_spec(cc.shape), _const_spec(sc.shape), _const_spec(wf.shape), _const_spec(wp.shape),
            _const_spec(pscale.shape), _const_spec(wbf.shape), _const_spec(wbp.shape),
            _const_spec(wout.shape), _const_spec(gm.shape), _const_spec(wr2.shape),
        ],
        out_specs=[row(d), row(d), row(N_EXPERTS)],
        out_shape=[jax.ShapeDtypeStruct((t, d), F32), jax.ShapeDtypeStruct((t, d), BF16),
                   jax.ShapeDtypeStruct((t, N_EXPERTS), F32)],
        compiler_params=_params(("parallel",)),
        name="tail",
    )(pc2, ps2, pl2, gates, x2, cc, sc, wf, wp, pscale, wbf, wbp, wout, gm, wr2)


def _route_body(lg_ref, tri_ref, aff_ref, sel_ref, cum_ref, *, cap):
    nb, ne, s = lg_ref.shape
    lg = lg_ref[...]
    mx = jnp.max(lg, axis=1, keepdims=True)
    ex = jnp.exp(lg - mx)
    aff = ex / jnp.sum(ex, axis=1, keepdims=True)
    aff_ref[...] = aff
    aff2 = aff.reshape(nb * ne, s)

    def step(i, bits):
        cand = bits | jnp.left_shift(jnp.int32(1), 30 - i)
        n_ge = jnp.sum(jnp.where(aff2 >= pltpu.bitcast(cand, F32), 1.0, 0.0), axis=1, keepdims=True)
        return jnp.where(n_ge >= cap, cand, bits)

    thr = pltpu.bitcast(lax.fori_loop(0, 31, step, jnp.zeros((nb * ne, 1), I32)), F32)
    above = aff2 > thr
    tie = aff2 == thr
    n_above = jnp.sum(jnp.where(above, 1.0, 0.0), axis=1, keepdims=True)
    tri = tri_ref[...]
    tie_rank = jnp.dot(jnp.where(tie, 1.0, 0.0).astype(BF16), tri, preferred_element_type=F32)
    chosen = above | (tie & (tie_rank <= (cap - n_above)))
    cum = jnp.dot(jnp.where(chosen, 1.0, 0.0).astype(BF16), tri, preferred_element_type=F32)
    cum_ref[...] = cum.astype(I32).reshape(nb, ne, s)
    sel_ref[...] = jnp.where(chosen, cum - 1.0, -1.0).astype(I32).reshape(nb, ne, s)


def _route(lg_t, tri, cap):
    nb, ne, s = lg_t.shape
    ints = jax.ShapeDtypeStruct((nb, ne, s), I32)
    return pl.pallas_call(
        functools.partial(_route_body, cap=cap),
        out_shape=[jax.ShapeDtypeStruct((nb, ne, s), F32), ints, ints],
        compiler_params=pltpu.CompilerParams(vmem_limit_bytes=V7X_VMEM_LIMIT_BYTES),
        name="route",
    )(lg_t, tri)


TOK_BLK = 256
WIN = 64


def _window_plan(cum, cap):
    c_end = cum[:, :, TOK_BLK - 1::TOK_BLK]
    c_start = jnp.concatenate([jnp.zeros_like(c_end[:, :, :1]), c_end[:, :, :-1]], axis=-1)
    base = (c_start // BF16_SUBLANES) * BF16_SUBLANES
    passes = jnp.where(c_end > c_start, (c_end - base + WIN - 1) // WIN, 0)
    return jnp.swapaxes(base, 1, 2), jnp.max(passes, axis=1)


GATHER_DT = 1024


def _gather_body(base_s, npass_s, sel_ref, aff_ref, basec_ref, v_ref, xs_ref, gs_ref, *, cap):
    b = pl.program_id(0)
    ne, s = sel_ref.shape
    nk = s // TOK_BLK
    xs_ref[...] = jnp.zeros(xs_ref.shape, xs_ref.dtype)
    wiota = lax.broadcasted_iota(I32, (WIN, TOK_BLK), 0)
    for k in range(nk):
        toks = slice(k * TOK_BLK, (k + 1) * TOK_BLK)
        selk = sel_ref[:, toks]
        basek = basec_ref[k]

        def one_pass(p, carry, k=k, toks=toks, selk=selk, basek=basek):
            first = basek + p * WIN
            wbase = jnp.minimum(first, cap - WIN)
            rel = jnp.where(selk >= first, selk - wbase, -1)
            onehot = jnp.concatenate([jnp.where(rel[e:e + 1, :] == wiota, 1.0, 0.0) for e in range(ne)],
                                     axis=0).astype(BF16)
            rows = jnp.dot(onehot, v_ref[toks, :], preferred_element_type=F32)
            for e in range(ne):
                off = jnp.minimum(base_s[(b * nk + k) * ne + e] + p * WIN, cap - WIN)
                off = pl.multiple_of(off, BF16_SUBLANES)
                xs_ref[e, pl.ds(off, WIN), :] += rows[e * WIN:(e + 1) * WIN].astype(BF16)
            return carry

        lax.fori_loop(0, npass_s[b * nk + k], one_pass, 0)

    @pl.when(pl.program_id(1) == 0)
    def _():
        slots = lax.broadcasted_iota(I32, (cap, s), 0)
        for e in range(ne):
            own = slots == sel_ref[e:e + 1, :]
            gs_ref[e] = jnp.sum(jnp.where(own, aff_ref[e:e + 1, :], 0.0), axis=1, keepdims=True)


def _gather(base, npass, sel, aff, v3, cap):
    nb, ne, s = sel.shape
    d = v3.shape[-1]
    nk = s // TOK_BLK
    grid_spec = pltpu.PrefetchScalarGridSpec(
        num_scalar_prefetch=2,
        grid=(nb, d // GATHER_DT),
        in_specs=[
            pl.BlockSpec((None, ne, s), lambda b, j, *_: (b, 0, 0)),
            pl.BlockSpec((None, ne, s), lambda b, j, *_: (b, 0, 0)),
            pl.BlockSpec((None, nk, ne, 1), lambda b, j, *_: (b, 0, 0, 0)),
            pl.BlockSpec((None, s, GATHER_DT), lambda b, j, *_: (b, 0, j)),
        ],
        out_specs=[
            pl.BlockSpec((ne, cap, GATHER_DT), lambda b, j, *_: (0, b, j)),
            pl.BlockSpec((ne, cap, 1), lambda b, j, *_: (0, b, 0)),
        ],
    )
    return pl.pallas_call(
        functools.partial(_gather_body, cap=cap),
        grid_spec=grid_spec,
        out_shape=[jax.ShapeDtypeStruct((ne, nb * cap, d), BF16),
                   jax.ShapeDtypeStruct((ne, nb * cap, 1), F32)],
        compiler_params=_params(("arbitrary", "arbitrary")),
        name="gather",
    )(base.reshape(-1), npass.reshape(-1), sel, aff, base[..., None], v3)


def _combine_body(base_s, npass_s, selc_ref, baser_ref, y_ref, h_ref, g_ref, o_ref, acc_ref, *, cap):
    b = pl.program_id(0)
    k = pl.program_id(1)
    nk = pl.num_programs(1)
    ne = y_ref.shape[0]
    acc_ref[...] = h_ref[...]
    selk = selc_ref[...]
    basek = baser_ref[...]
    lane = lax.broadcasted_iota(I32, (ne, ne * WIN), 1)
    spread = jnp.where(lane // WIN == lax.broadcasted_iota(I32, (ne, ne * WIN), 0), 1.0, 0.0).astype(BF16)
    wlane = (lax.broadcasted_iota(I32, (1, ne * WIN), 1) % WIN).astype(F32)

    def one_pass(p, carry):
        first = basek + p * WIN
        wbase = jnp.minimum(first, cap - WIN)
        rel = jnp.where(selk >= first, selk - wbase, -1)
        relx = jnp.dot(rel.astype(F32).astype(BF16), spread, preferred_element_type=F32)
        onehot = jnp.where(relx == wlane, 1.0, 0.0).astype(BF16)
        wins = []
        for e in range(ne):
            off = jnp.minimum(base_s[(b * nk + k) * ne + e] + p * WIN, cap - WIN)
            wins.append(y_ref[e, pl.ds(pl.multiple_of(off, BF16_SUBLANES), WIN), :])
        acc_ref[...] += jnp.dot(onehot, jnp.concatenate(wins, axis=0), preferred_element_type=F32)
        return carry

    lax.fori_loop(0, npass_s[b * nk + k], one_pass, 0)
    h = acc_ref[...]
    ms = jnp.mean(h * h, axis=-1, keepdims=True)
    o_ref[...] = h * lax.rsqrt(ms + EPS) * g_ref[...]


def _combine(base, npass, sel_col, ys, h3, g_final, cap):
    nb, s, ne = sel_col.shape
    d = h3.shape[-1]
    nk = s // TOK_BLK
    grid_spec = pltpu.PrefetchScalarGridSpec(
        num_scalar_prefetch=2,
        grid=(nb, nk),
        in_specs=[
            pl.BlockSpec((None, TOK_BLK, ne), lambda b, k, *_: (b, k, 0)),
            pl.BlockSpec((None, None, 1, ne), lambda b, k, *_: (b, k, 0, 0)),
            pl.BlockSpec((ne, cap, d), lambda b, k, *_: (0, b, 0)),
            pl.BlockSpec((None, TOK_BLK, d), lambda b, k, *_: (b, k, 0)),
            pl.BlockSpec((1, d), lambda b, k, *_: (0, 0)),
        ],
        out_specs=pl.BlockSpec((None, TOK_BLK, d), lambda b, k, *_: (b, k, 0)),
        scratch_shapes=[pltpu.VMEM((TOK_BLK, d), F32)],
    )
    return pl.pallas_call(
        functools.partial(_combine_body, cap=cap),
        grid_spec=grid_spec,
        out_shape=jax.ShapeDtypeStruct((nb, s, d), F32),
        compiler_params=_params(("arbitrary", "arbitrary")),
        name="combine",
    )(base.reshape(-1), npass.reshape(-1), sel_col, base[:, :, None, :], ys, h3, g_final)


FFN_TM = 1024
FFN_TF = 256
FFN_TAIL = EXPERT_FF % FFN_TF
FFN_NFULL = EXPERT_FF // FFN_TF


def _ffn_body(xs_ref, gs_ref, wg_ref, wu_ref, wd_ref, wgt_ref, wut_ref, wdt_ref, y_ref, acc_ref):
    f = pl.program_id(2)

    def partial_out(wg, wu, wd):
        xs = xs_ref[...]
        a = jnp.dot(xs, wg[...].astype(BF16), preferred_element_type=F32)
        u = jnp.dot(xs, wu[...].astype(BF16), preferred_element_type=F32)
        hid = (jax.nn.silu(a) * u).astype(BF16)
        return jnp.dot(hid, wd[...].astype(BF16), preferred_element_type=F32)

    @pl.when(f == 0)
    def _():
        acc_ref[...] = partial_out(wg_ref, wu_ref, wd_ref)

    @pl.when((f > 0) & (f < FFN_NFULL))
    def _():
        acc_ref[...] += partial_out(wg_ref, wu_ref, wd_ref)

    @pl.when(f == FFN_NFULL)
    def _():
        y = acc_ref[...] + partial_out(wgt_ref, wut_ref, wdt_ref)
        y_ref[...] = (y * gs_ref[...]).astype(BF16)


def _ffn(xs, gs, w_gate_e, w_up_e, w_down_e):
    ne, m, d = xs.shape
    ff = w_gate_e.shape[-1]
    assert ff == FFN_NFULL * FFN_TF + FFN_TAIL and FFN_TAIL % 128 == 0 and FFN_TAIL > 0
    tail_blk = ff // FFN_TAIL - 1
    full = lambda e, i, f: jnp.minimum(f, FFN_NFULL - 1)
    return pl.pallas_call(
        _ffn_body,
        grid=(ne, m // FFN_TM, FFN_NFULL + 1),
        in_specs=[
            pl.BlockSpec((None, FFN_TM, d), lambda e, i, f: (e, i, 0)),
            pl.BlockSpec((None, FFN_TM, 1), lambda e, i, f: (e, i, 0)),
            pl.BlockSpec((None, d, FFN_TF), lambda e, i, f: (e, 0, full(e, i, f))),
            pl.BlockSpec((None, d, FFN_TF), lambda e, i, f: (e, 0, full(e, i, f))),
            pl.BlockSpec((None, FFN_TF, d), lambda e, i, f: (e, full(e, i, f), 0)),
            pl.BlockSpec((None, d, FFN_TAIL), lambda e, i, f: (e, 0, tail_blk)),
            pl.BlockSpec((None, d, FFN_TAIL), lambda e, i, f: (e, 0, tail_blk)),
            pl.BlockSpec((None, FFN_TAIL, d), lambda e, i, f: (e, tail_blk, 0)),
        ],
        out_specs=pl.BlockSpec((None, FFN_TM, d), lambda e, i, f: (e, i, 0)),
        out_shape=jax.ShapeDtypeStruct((ne, m, d), BF16),
        scratch_shapes=[pltpu.VMEM((FFN_TM, d), F32)],
        compiler_params=_params(("arbitrary", "arbitrary", "arbitrary")),
        name="ffn",
    )(xs, gs, w_gate_e, w_up_e, w_down_e, w_gate_e, w_up_e, w_down_e)


TRIG_SPLIT = 32


def _trig_rows(n, rows, cols):
    ang = ((rows[:, None] * cols[None, :]) % n).astype(F32) * (2.0 * math.pi / n)
    return jnp.cos(ang), jnp.sin(ang)


def _dft_tables(n, rows, cols):
    nr = rows.shape[0]
    r1 = lax.iota(I32, nr // TRIG_SPLIT) * TRIG_SPLIT
    r0 = lax.iota(I32, TRIG_SPLIT)
    c1, s1 = _trig_rows(n, r1, cols)
    c0, s0 = _trig_rows(n, r0, cols)
    scale = 1.0 / math.sqrt(n)
    c1, s1 = (c1 * scale)[:, None, :], (s1 * scale)[:, None, :]
    c0, s0 = c0[None], s0[None]
    cos = (c1 * c0 - s1 * s0).reshape(nr, -1)
    sin = (s1 * c0 + c1 * s0).reshape(nr, -1)
    return cos, sin


def _pool_tables(s):
    i = lax.iota(I32, s)[:, None]
    k = lax.iota(I32, s)[None, :]
    out = []
    for half in POOL_HALF:
        lo = jnp.clip(i - half, 0, s)
        hi = jnp.clip(i + half, 0, s)
        inside = (k >= lo) & (k < hi)
        cnt = (hi - lo).astype(F32)
        out.append((jnp.where(inside, 1.0 / cnt, 0.0) - jnp.where(i == k, 1.0, 0.0)).astype(BF16))
    return jnp.stack(out)


def kernel(x, norm_mix_g, w_in, w_fourier_mix, w_pool_mix, pool_scale, w_branch_f, w_branch_p, w_gate,
           b_gate, w_out, norm_moe_g, w_router, w_expert_gate, w_expert_up, w_expert_down, norm_final_g):
    nb, s, d = x.shape
    assert w_in.shape[0] == 1, "single-layer block only"
    assert s % (2 * TOK_BLK) == 0 and max(POOL_HALF) <= POOL_HALO
    cap = CAPACITY_FACTOR * s // N_EXPERTS
    t = nb * s

    freqs = lax.iota(I32, s // 2)
    cos_e, sin_e = _dft_tables(s, freqs, 2 * freqs)
    cos_o, sin_o = _dft_tables(s, freqs, 2 * freqs + 1)
    trig = jnp.stack([cos_e.astype(BF16), cos_o.astype(BF16), sin_e.astype(BF16), sin_o.astype(BF16)])
    chan = lax.iota(I32, GROUP)
    cos_c, sin_c = _trig_rows(GROUP, chan, chan)
    cos_c = (cos_c / math.sqrt(GROUP)).astype(BF16)
    sin_c = (sin_c / math.sqrt(GROUP)).astype(BF16)
    tri = (lax.iota(I32, s)[:, None] <= lax.iota(I32, s)[None, :]).astype(BF16)

    x2 = x.reshape(t, d)
    p, gates = _proj(x2, norm_mix_g[0][None], w_in[0].astype(BF16), w_gate[0].astype(BF16), b_gate[0][None])
    pc, ps = _fourier(trig, p.reshape(nb, s // 2, -1), s)
    pooled = _pool(_pool_tables(s), p.reshape(nb, s, -1))
    wr = w_router[0]
    wr_hi = wr.astype(BF16)
    wr_lo = (wr - wr_hi.astype(F32)).astype(BF16)
    h2, v, logits = _tail(
        pc.reshape(t, -1), ps.reshape(t, -1), pooled.reshape(t, -1), gates, x2, cos_c, sin_c,
        w_fourier_mix[0].astype(BF16), w_pool_mix[0].astype(BF16), pool_scale[0][None],
        w_branch_f[0].astype(BF16), w_branch_p[0].astype(BF16), w_out[0].astype(BF16),
        norm_moe_g[0][None], jnp.concatenate([wr_hi, wr_lo], axis=1))
    lg_t = jnp.swapaxes(logits.reshape(nb, s, N_EXPERTS), 1, 2)
    aff, sel, cum = _route(lg_t, tri, cap)
    base, npass = _window_plan(cum, cap)
    xs, gs = _gather(base, npass, sel, aff, v.reshape(nb, s, d), cap)
    ys = _ffn(xs, gs, w_expert_gate[0], w_expert_up[0], w_expert_down[0])
    return _combine(base, npass, jnp.swapaxes(sel, 1, 2), ys, h2.reshape(nb, s, d), norm_final_g[None], cap)
```

```python
import functools
import math

import jax
import jax.numpy as jnp
from jax import lax
from jax.experimental import pallas as pl
from jax.experimental.pallas import tpu as pltpu

F32 = jnp.float32
BF16 = jnp.bfloat16
I32 = jnp.int32

D_MODEL = 2048
FOURIER_WIDTH = 1024
N_GROUPS = 4
GROUP = 256
POOL_HALF = (1, 2, 4, 8)
N_EXPERTS = 16
EXPERT_FF = 1408
CAPACITY_FACTOR = 2
EPS = 1e-6

V7X_VMEM_LIMIT_BYTES = 58 * 1024 * 1024
BF16_SUBLANES = 16
LANES = 128


def _params(sem, vmem=V7X_VMEM_LIMIT_BYTES):
    return pltpu.CompilerParams(dimension_semantics=sem, vmem_limit_bytes=vmem)


def _const_spec(shape):
    nd = len(shape)
    return pl.BlockSpec(shape, lambda *_: (0,) * nd, pipeline_mode=pl.Buffered(1))


PROJ_TM = 512
PROJ_NC = 512


def _proj_body(x_ref, g_ref, win_ref, wg_ref, bg_ref, pf_ref, pp_ref, gate_ref, split_ref):
    x = x_ref[...]
    ms = jnp.mean(x * x, axis=-1, keepdims=True)
    u = (x * lax.rsqrt(ms + EPS) * g_ref[...]).astype(BF16)
    half = PROJ_TM // 2
    for j in range(FOURIER_WIDTH // PROJ_NC):
        res = jnp.dot(u, win_ref[:, j * PROJ_NC:(j + 1) * PROJ_NC], preferred_element_type=F32)
        for c in range(PROJ_NC // LANES):
            col = j * PROJ_NC + c * LANES
            split_ref[c] = res[:, c * LANES:(c + 1) * LANES]
            pf_ref[:, col:col + LANES] = split_ref[c, pl.ds(0, half, stride=2), :].astype(BF16)
            pf_ref[:, FOURIER_WIDTH + col:FOURIER_WIDTH + col + LANES] = (
                split_ref[c, pl.ds(1, half, stride=2), :].astype(BF16))
    for j in range((win_ref.shape[1] - FOURIER_WIDTH) // PROJ_NC):
        sl = slice(FOURIER_WIDTH + j * PROJ_NC, FOURIER_WIDTH + (j + 1) * PROJ_NC)
        pp_ref[:, j * PROJ_NC:(j + 1) * PROJ_NC] = jnp.dot(u, win_ref[:, sl], preferred_element_type=F32).astype(BF16)
    for j in range(wg_ref.shape[1] // PROJ_NC):
        sl = slice(j * PROJ_NC, (j + 1) * PROJ_NC)
        a = jnp.dot(u, wg_ref[:, sl], preferred_element_type=F32) + bg_ref[:, sl]
        gate_ref[:, sl] = jax.nn.sigmoid(a).astype(BF16)


def _proj(x2, g, w_in, w_gate, b_gate):
    t, d = x2.shape
    dm, dg = w_in.shape[1], w_gate.shape[1]
    return pl.pallas_call(
        _proj_body,
        grid=(t // PROJ_TM,),
        in_specs=[
            pl.BlockSpec((PROJ_TM, d), lambda i: (i, 0)),
            _const_spec((1, d)),
            _const_spec((d, dm)),
            _const_spec((d, dg)),
            _const_spec((1, dg)),
        ],
        out_specs=[
            pl.BlockSpec((PROJ_TM // 2, 2 * FOURIER_WIDTH), lambda i: (i, 0)),
            pl.BlockSpec((PROJ_TM, dm - FOURIER_WIDTH), lambda i: (i, 0)),
            pl.BlockSpec((PROJ_TM, dg), lambda i: (i, 0)),
        ],
        out_shape=[jax.ShapeDtypeStruct((t // 2, 2 * FOURIER_WIDTH), BF16),
                   jax.ShapeDtypeStruct((t, dm - FOURIER_WIDTH), BF16),
                   jax.ShapeDtypeStruct((t, dg), BF16)],
        scratch_shapes=[pltpu.VMEM((PROJ_NC // LANES, PROJ_TM, LANES), F32)],
        compiler_params=_params(("parallel",)),
        name="proj",
    )(x2, g, w_in, w_gate, b_gate)


DFT_MC = 512


def _fourier_body(trig_ref, pe_ref, po_ref, pc_ref, ps_ref):
    half = trig_ref.shape[1]
    pe = pe_ref[...]
    po = po_ref[...]
    for i in range(half // DFT_MC):
        lo = slice(i * DFT_MC, (i + 1) * DFT_MC)
        hi = slice(half + i * DFT_MC, half + (i + 1) * DFT_MC)
        for tab, out in ((0, pc_ref), (2, ps_ref)):
            ev = jnp.dot(trig_ref[tab, lo, :], pe, preferred_element_type=F32)
            od = jnp.dot(trig_ref[tab + 1, lo, :], po, preferred_element_type=F32)
            out[lo, :] = (ev + od).astype(BF16)
            out[hi, :] = (ev - od).astype(BF16)


def _fourier(trig, p_pairs, s):
    b, half, two_fw = p_pairs.shape
    odd0 = (two_fw // 2) // GROUP
    out = jax.ShapeDtypeStruct((b, s, FOURIER_WIDTH), BF16)
    return pl.pallas_call(
        _fourier_body,
        grid=(b, FOURIER_WIDTH // GROUP),
        in_specs=[
            _const_spec(trig.shape),
            pl.BlockSpec((None, half, GROUP), lambda i, g: (i, 0, g)),
            pl.BlockSpec((None, half, GROUP), lambda i, g: (i, 0, odd0 + g)),
        ],
        out_specs=[pl.BlockSpec((None, s, GROUP), lambda i, g: (i, 0, g))] * 2,
        out_shape=[out, out],
        compiler_params=_params(("arbitrary", "arbitrary")),
        name="fourier",
    )(trig, p_pairs, p_pairs)


POOL_TM = 256
POOL_HALO = 128


POOL_BAND = POOL_TM + 2 * POOL_HALO


def _pool_body(a_ref, p_ref, o_ref):
    s = a_ref.shape[0]
    for i in range(s // POOL_TM):
        rows = slice(i * POOL_TM, (i + 1) * POOL_TM)
        lo, hi = i * POOL_TM - POOL_HALO, (i + 1) * POOL_TM + POOL_HALO
        band = slice(max(-lo, 0), POOL_BAND - max(hi - s, 0))
        src = slice(max(lo, 0), min(hi, s))
        o_ref[rows, :] = jnp.dot(a_ref[rows, band], p_ref[src, :], preferred_element_type=F32).astype(BF16)


def _pool(pool_bands, pp3):
    b, s, pw = pp3.shape
    return pl.pallas_call(
        _pool_body,
        grid=(N_GROUPS, b),
        in_specs=[
            pl.BlockSpec((None, s, POOL_BAND), lambda g, i: (g, 0, 0)),
            pl.BlockSpec((None, s, GROUP), lambda g, i: (i, 0, g)),
        ],
        out_specs=pl.BlockSpec((None, s, GROUP), lambda g, i: (i, 0, g)),
        out_shape=jax.ShapeDtypeStruct((b, s, pw), BF16),
        compiler_params=_params(("arbitrary", "arbitrary")),
        name="pool",
    )(pool_bands, pp3)


TAIL_TM = 256


def _tail_body(pc_ref, ps_ref, pl_ref, gate_ref, x_ref, cc_ref, sc_ref, wf_ref, wp_ref, psc_ref, wbf_ref, wbp_ref,
               wout_ref, gm_ref, wr_ref, h_ref, v_ref, lg_ref):
    yf, yp = [], []
    for g in range(N_GROUPS):
        cols = slice(g * GROUP, (g + 1) * GROUP)
        z = (jnp.dot(pc_ref[:, cols], cc_ref[...], preferred_element_type=F32)
             - jnp.dot(ps_ref[:, cols], sc_ref[...], preferred_element_type=F32))
        yf.append(jnp.dot(z.astype(BF16), wf_ref[g], preferred_element_type=F32).astype(BF16))
        ypg = jnp.dot(pl_ref[:, cols], wp_ref[g], preferred_element_type=F32) * psc_ref[:, cols]
        yp.append(ypg.astype(BF16))
    yf = jnp.concatenate(yf, axis=-1)
    yp = jnp.concatenate(yp, axis=-1)
    d = wout_ref.shape[0]
    bf = jnp.dot(yf, wbf_ref[...], preferred_element_type=F32)
    bp = jnp.dot(yp, wbp_ref[...], preferred_element_type=F32)
    merged = gate_ref[:, :d].astype(F32) * bf + gate_ref[:, d:].astype(F32) * bp
    h = x_ref[...] + jnp.dot(merged.astype(BF16), wout_ref[...], preferred_element_type=F32)
    h_ref[...] = h
    ms = jnp.mean(h * h, axis=-1, keepdims=True)
    v = h * lax.rsqrt(ms + EPS) * gm_ref[...]
    v_hi = v.astype(BF16)
    v_lo = (v - v_hi.astype(F32)).astype(BF16)
    v_ref[...] = v_hi
    acc = (jnp.dot(v_hi, wr_ref[...], preferred_element_type=F32)
           + jnp.dot(v_lo, wr_ref[...], preferred_element_type=F32))
    lg_ref[...] = acc[:, :N_EXPERTS] + acc[:, N_EXPERTS:]


def _tail(pc2, ps2, pl2, gates, x2, cc, sc, wf, wp, pscale, wbf, wbp, wout, gm, wr2):
    t, d = x2.shape
    row = lambda width: pl.BlockSpec((TAIL_TM, width), lambda i: (i, 0))
    return pl.pallas_call(
        _tail_body,
        grid=(t // TAIL_TM,),
        in_specs=[
            row(pc2.shape[1]), row(ps2.shape[1]), row(pl2.shape[1]), row(gates.shape[1]), row(d),
            _const_spec(cc.shape), _const_spec(sc.shape), _const_spec(wf.shape), _const_spec(wp.shape),
            _const_spec(pscale.shape), _const_spec(wbf.shape), _const_spec(wbp.shape),
            _const_spec(wout.shape), _const_spec(gm.shape), _const_spec(wr2.shape),
        ],
        out_specs=[row(d), row(d), row(N_EXPERTS)],
        out_shape=[jax.ShapeDtypeStruct((t, d), F32), jax.ShapeDtypeStruct((t, d), BF16),
                   jax.ShapeDtypeStruct((t, N_EXPERTS), F32)],
        compiler_params=_params(("parallel",)),
        name="tail",
    )(pc2, ps2, pl2, gates, x2, cc, sc, wf, wp, pscale, wbf, wbp, wout, gm, wr2)


def _route_body(lg_ref, tri_ref, aff_ref, sel_ref, cum_ref, *, cap):
    nb, ne, s = lg_ref.shape
    lg = lg_ref[...]
    mx = jnp.max(lg, axis=1, keepdims=True)
    ex = jnp.exp(lg - mx)
    aff = ex / jnp.sum(ex, axis=1, keepdims=True)
    aff_ref[...] = aff
    aff2 = aff.reshape(nb * ne, s)

    def step(i, bits):
        cand = bits | jnp.left_shift(jnp.int32(1), 30 - i)
        n_ge = jnp.sum(jnp.where(aff2 >= pltpu.bitcast(cand, F32), 1.0, 0.0), axis=1, keepdims=True)
        return jnp.where(n_ge >= cap, cand, bits)

    thr = pltpu.bitcast(lax.fori_loop(0, 31, step, jnp.zeros((nb * ne, 1), I32)), F32)
    above = aff2 > thr
    tie = aff2 == thr
    n_above = jnp.sum(jnp.where(above, 1.0, 0.0), axis=1, keepdims=True)
    tri = tri_ref[...]
    tie_rank = jnp.dot(jnp.where(tie, 1.0, 0.0).astype(BF16), tri, preferred_element_type=F32)
    chosen = above | (tie & (tie_rank <= (cap - n_above)))
    cum = jnp.dot(jnp.where(chosen, 1.0, 0.0).astype(BF16), tri, preferred_element_type=F32)
    cum_ref[...] = cum.astype(I32).reshape(nb, ne, s)
    sel_ref[...] = jnp.where(chosen, cum - 1.0, -1.0).astype(I32).reshape(nb, ne, s)


def _route(lg_t, tri, cap):
    nb, ne, s = lg_t.shape
    ints = jax.ShapeDtypeStruct((nb, ne, s), I32)
    return pl.pallas_call(
        functools.partial(_route_body, cap=cap),
        out_shape=[jax.ShapeDtypeStruct((nb, ne, s), F32), ints, ints],
        compiler_params=pltpu.CompilerParams(vmem_limit_bytes=V7X_VMEM_LIMIT_BYTES),
        name="route",
    )(lg_t, tri)


TOK_BLK = 256
WIN = 64


def _window_plan(cum, cap):
    c_end = cum[:, :, TOK_BLK - 1::TOK_BLK]
    c_start = jnp.concatenate([jnp.zeros_like(c_end[:, :, :1]), c_end[:, :, :-1]], axis=-1)
    base = (c_start // BF16_SUBLANES) * BF16_SUBLANES
    passes = jnp.where(c_end > c_start, (c_end - base + WIN - 1) // WIN, 0)
    return jnp.swapaxes(base, 1, 2), jnp.max(passes, axis=1)


GATHER_DT = 1024


def _gather_body(base_s, npass_s, sel_ref, aff_ref, basec_ref, v_ref, xs_ref, gs_ref, *, cap):
    b = pl.program_id(0)
    ne, s = sel_ref.shape
    nk = s // TOK_BLK
    xs_ref[...] = jnp.zeros(xs_ref.shape, xs_ref.dtype)
    wiota = lax.broadcasted_iota(I32, (WIN, TOK_BLK), 0)
    for k in range(nk):
        toks = slice(k * TOK_BLK, (k + 1) * TOK_BLK)
        selk = sel_ref[:, toks]
        basek = basec_ref[k]

        def one_pass(p, carry, k=k, toks=toks, selk=selk, basek=basek):
            first = basek + p * WIN
            wbase = jnp.minimum(first, cap - WIN)
            rel = jnp.where(selk >= first, selk - wbase, -1)
            onehot = jnp.concatenate([jnp.where(rel[e:e + 1, :] == wiota, 1.0, 0.0) for e in range(ne)],
                                     axis=0).astype(BF16)
            rows = jnp.dot(onehot, v_ref[toks, :], preferred_element_type=F32)
            for e in range(ne):
                off = jnp.minimum(base_s[(b * nk + k) * ne + e] + p * WIN, cap - WIN)
                off = pl.multiple_of(off, BF16_SUBLANES)
                xs_ref[e, pl.ds(off, WIN), :] += rows[e * WIN:(e + 1) * WIN].astype(BF16)
            return carry

        lax.fori_loop(0, npass_s[b * nk + k], one_pass, 0)

    @pl.when(pl.program_id(1) == 0)
    def _():
        slots = lax.broadcasted_iota(I32, (cap, s), 0)
        for e in range(ne):
            own = slots == sel_ref[e:e + 1, :]
            gs_ref[e] = jnp.sum(jnp.where(own, aff_ref[e:e + 1, :], 0.0), axis=1, keepdims=True)


def _gather(base, npass, sel, aff, v3, cap):
    nb, ne, s = sel.shape
    d = v3.shape[-1]
    nk = s // TOK_BLK
    grid_spec = pltpu.PrefetchScalarGridSpec(
        num_scalar_prefetch=2,
        grid=(nb, d // GATHER_DT),
        in_specs=[
            pl.BlockSpec((None, ne, s), lambda b, j, *_: (b, 0, 0)),
            pl.BlockSpec((None, ne, s), lambda b, j, *_: (b, 0, 0)),
            pl.BlockSpec((None, nk, ne, 1), lambda b, j, *_: (b, 0, 0, 0)),
            pl.BlockSpec((None, s, GATHER_DT), lambda b, j, *_: (b, 0, j)),
        ],
        out_specs=[
            pl.BlockSpec((ne, cap, GATHER_DT), lambda b, j, *_: (0, b, j)),
            pl.BlockSpec((ne, cap, 1), lambda b, j, *_: (0, b, 0)),
        ],
    )
    return pl.pallas_call(
        functools.partial(_gather_body, cap=cap),
        grid_spec=grid_spec,
        out_shape=[jax.ShapeDtypeStruct((ne, nb * cap, d), BF16),
                   jax.ShapeDtypeStruct((ne, nb * cap, 1), F32)],
        compiler_params=_params(("arbitrary", "arbitrary")),
        name="gather",
    )(base.reshape(-1), npass.reshape(-1), sel, aff, base[..., None], v3)


def _combine_body(base_s, npass_s, selc_ref, baser_ref, y_ref, h_ref, g_ref, o_ref, acc_ref, *, cap):
    b = pl.program_id(0)
    k = pl.program_id(1)
    nk = pl.num_programs(1)
    ne = y_ref.shape[0]
    acc_ref[...] = h_ref[...]
    selk = selc_ref[...]
    basek = baser_ref[...]
    lane = lax.broadcasted_iota(I32, (ne, ne * WIN), 1)
    spread = jnp.where(lane // WIN == lax.broadcasted_iota(I32, (ne, ne * WIN), 0), 1.0, 0.0).astype(BF16)
    wlane = (lax.broadcasted_iota(I32, (1, ne * WIN), 1) % WIN).astype(F32)

    def one_pass(p, carry):
        first = basek + p * WIN
        wbase = jnp.minimum(first, cap - WIN)
        rel = jnp.where(selk >= first, selk - wbase, -1)
        relx = jnp.dot(rel.astype(F32).astype(BF16), spread, preferred_element_type=F32)
        onehot = jnp.where(relx == wlane, 1.0, 0.0).astype(BF16)
        wins = []
        for e in range(ne):
            off = jnp.minimum(base_s[(b * nk + k) * ne + e] + p * WIN, cap - WIN)
            wins.append(y_ref[e, pl.ds(pl.multiple_of(off, BF16_SUBLANES), WIN), :])
        acc_ref[...] += jnp.dot(onehot, jnp.concatenate(wins, axis=0), preferred_element_type=F32)
        return carry

    lax.fori_loop(0, npass_s[b * nk + k], one_pass, 0)
    h = acc_ref[...]
    ms = jnp.mean(h * h, axis=-1, keepdims=True)
    o_ref[...] = h * lax.rsqrt(ms + EPS) * g_ref[...]


def _combine(base, npass, sel_col, ys, h3, g_final, cap):
    nb, s, ne = sel_col.shape
    d = h3.shape[-1]
    nk = s // TOK_BLK
    grid_spec = pltpu.PrefetchScalarGridSpec(
        num_scalar_prefetch=2,
        grid=(nb, nk),
        in_specs=[
            pl.BlockSpec((None, TOK_BLK, ne), lambda b, k, *_: (b, k, 0)),
            pl.BlockSpec((None, None, 1, ne), lambda b, k, *_: (b, k, 0, 0)),
            pl.BlockSpec((ne, cap, d), lambda b, k, *_: (0, b, 0)),
            pl.BlockSpec((None, TOK_BLK, d), lambda b, k, *_: (b, k, 0)),
            pl.BlockSpec((1, d), lambda b, k, *_: (0, 0)),
        ],
        out_specs=pl.BlockSpec((None, TOK_BLK, d), lambda b, k, *_: (b, k, 0)),
        scratch_shapes=[pltpu.VMEM((TOK_BLK, d), F32)],
    )
    return pl.pallas_call(
        functools.partial(_combine_body, cap=cap),
        grid_spec=grid_spec,
        out_shape=jax.ShapeDtypeStruct((nb, s, d), F32),
        compiler_params=_params(("arbitrary", "arbitrary")),
        name="combine",
    )(base.reshape(-1), npass.reshape(-1), sel_col, base[:, :, None, :], ys, h3, g_final)


FFN_TM = 1024
FFN_TF = 256
FFN_TAIL = EXPERT_FF % FFN_TF
FFN_NFULL = EXPERT_FF // FFN_TF


def _ffn_body(xs_ref, gs_ref, wg_ref, wu_ref, wd_ref, wgt_ref, wut_ref, wdt_ref, y_ref, acc_ref):
    f = pl.program_id(2)

    def partial_out(wg, wu, wd):
        xs = xs_ref[...]
        tf = wg.shape[1]
        if tf < FFN_TF:
            both = jnp.concatenate([wg[...].astype(BF16), wu[...].astype(BF16)], axis=1)
            au = jnp.dot(xs, both, preferred_element_type=F32)
            a, u = au[:, :tf], au[:, tf:]
        else:
            a = jnp.dot(xs, wg[...].astype(BF16), preferred_element_type=F32)
            u = jnp.dot(xs, wu[...].astype(BF16), preferred_element_type=F32)
        hid = (jax.nn.silu(a) * u).astype(BF16)
        return jnp.dot(hid, wd[...].astype(BF16), preferred_element_type=F32)

    @pl.when(f == 0)
    def _():
        acc_ref[...] = partial_out(wg_ref, wu_ref, wd_ref)

    @pl.when((f > 0) & (f < FFN_NFULL))
    def _():
        acc_ref[...] += partial_out(wg_ref, wu_ref, wd_ref)

    @pl.when(f == FFN_NFULL)
    def _():
        y = acc_ref[...] + partial_out(wgt_ref, wut_ref, wdt_ref)
        y_ref[...] = (y * gs_ref[...]).astype(BF16)


def _ffn(xs, gs, w_gate_e, w_up_e, w_down_e):
    ne, m, d = xs.shape
    ff = w_gate_e.shape[-1]
    assert ff == FFN_NFULL * FFN_TF + FFN_TAIL and FFN_TAIL % 128 == 0 and FFN_TAIL > 0
    tail_blk = ff // FFN_TAIL - 1
    full = lambda e, i, f: jnp.minimum(f, FFN_NFULL - 1)
    return pl.pallas_call(
        _ffn_body,
        grid=(ne, m // FFN_TM, FFN_NFULL + 1),
        in_specs=[
            pl.BlockSpec((None, FFN_TM, d), lambda e, i, f: (e, i, 0)),
            pl.BlockSpec((None, FFN_TM, 1), lambda e, i, f: (e, i, 0)),
            pl.BlockSpec((None, d, FFN_TF), lambda e, i, f: (e, 0, full(e, i, f))),
            pl.BlockSpec((None, d, FFN_TF), lambda e, i, f: (e, 0, full(e, i, f))),
            pl.BlockSpec((None, FFN_TF, d), lambda e, i, f: (e, full(e, i, f), 0)),
            pl.BlockSpec((None, d, FFN_TAIL), lambda e, i, f: (e, 0, tail_blk)),
            pl.BlockSpec((None, d, FFN_TAIL), lambda e, i, f: (e, 0, tail_blk)),
            pl.BlockSpec((None, FFN_TAIL, d), lambda e, i, f: (e, tail_blk, 0)),
        ],
        out_specs=pl.BlockSpec((None, FFN_TM, d), lambda e, i, f: (e, i, 0)),
        out_shape=jax.ShapeDtypeStruct((ne, m, d), BF16),
        scratch_shapes=[pltpu.VMEM((FFN_TM, d), F32)],
        compiler_params=_params(("arbitrary", "arbitrary", "arbitrary")),
        name="ffn",
    )(xs, gs, w_gate_e, w_up_e, w_down_e, w_gate_e, w_up_e, w_down_e)


TRIG_SPLIT = 32


def _trig_rows(n, rows, cols):
    ang = ((rows[:, None] * cols[None, :]) % n).astype(F32) * (2.0 * math.pi / n)
    return jnp.cos(ang), jnp.sin(ang)


def _dft_tables(n, rows, cols):
    nr = rows.shape[0]
    r1 = lax.iota(I32, nr // TRIG_SPLIT) * TRIG_SPLIT
    r0 = lax.iota(I32, TRIG_SPLIT)
    c1, s1 = _trig_rows(n, r1, cols)
    c0, s0 = _trig_rows(n, r0, cols)
    scale = 1.0 / math.sqrt(n)
    c1, s1 = (c1 * scale)[:, None, :], (s1 * scale)[:, None, :]
    c0, s0 = c0[None], s0[None]
    cos = (c1 * c0 - s1 * s0).reshape(nr, -1)
    sin = (s1 * c0 + c1 * s0).reshape(nr, -1)
    return cos, sin


def _pool_tables(s):
    i = lax.iota(I32, s)[:, None]
    k = (i // POOL_TM) * POOL_TM - POOL_HALO + lax.iota(I32, POOL_BAND)[None, :]
    out = []
    for half in POOL_HALF:
        lo = jnp.clip(i - half, 0, s)
        hi = jnp.clip(i + half, 0, s)
        inside = (k >= lo) & (k < hi)
        cnt = (hi - lo).astype(F32)
        out.append((jnp.where(inside, 1.0 / cnt, 0.0) - jnp.where(i == k, 1.0, 0.0)).astype(BF16))
    return jnp.stack(out)


def kernel(x, norm_mix_g, w_in, w_fourier_mix, w_pool_mix, pool_scale, w_branch_f, w_branch_p, w_gate,
           b_gate, w_out, norm_moe_g, w_router, w_expert_gate, w_expert_up, w_expert_down, norm_final_g):
    nb, s, d = x.shape
    assert w_in.shape[0] == 1, "single-layer block only"
    assert s % (2 * TOK_BLK) == 0 and max(POOL_HALF) <= POOL_HALO
    cap = CAPACITY_FACTOR * s // N_EXPERTS
    t = nb * s

    freqs = lax.iota(I32, s // 2)
    cos_e, sin_e = _dft_tables(s, freqs, 2 * freqs)
    cos_o, sin_o = _dft_tables(s, freqs, 2 * freqs + 1)
    trig = jnp.stack([cos_e.astype(BF16), cos_o.astype(BF16), sin_e.astype(BF16), sin_o.astype(BF16)])
    chan = lax.iota(I32, GROUP)
    cos_c, sin_c = _trig_rows(GROUP, chan, chan)
    cos_c = (cos_c / math.sqrt(GROUP)).astype(BF16)
    sin_c = (sin_c / math.sqrt(GROUP)).astype(BF16)
    tri = (lax.iota(I32, s)[:, None] <= lax.iota(I32, s)[None, :]).astype(BF16)

    x2 = x.reshape(t, d)
    pf, pp, gates = _proj(x2, norm_mix_g[0][None], w_in[0].astype(BF16), w_gate[0].astype(BF16), b_gate[0][None])
    pc, ps = _fourier(trig, pf.reshape(nb, s // 2, -1), s)
    pooled = _pool(_pool_tables(s), pp.reshape(nb, s, -1))
    wr = w_router[0]
    wr_hi = wr.astype(BF16)
    wr_lo = (wr - wr_hi.astype(F32)).astype(BF16)
    h2, v, logits = _tail(
        pc.reshape(t, -1), ps.reshape(t, -1), pooled.reshape(t, -1), gates, x2, cos_c, sin_c,
        w_fourier_mix[0].astype(BF16), w_pool_mix[0].astype(BF16), pool_scale[0][None],
        w_branch_f[0].astype(BF16), w_branch_p[0].astype(BF16), w_out[0].astype(BF16),
        norm_moe_g[0][None], jnp.concatenate([wr_hi, wr_lo], axis=1))
    lg_t = jnp.swapaxes(logits.reshape(nb, s, N_EXPERTS), 1, 2)
    aff, sel, cum = _route(lg_t, tri, cap)
    base, npass = _window_plan(cum, cap)
    xs, gs = _gather(base, npass, sel, aff, v.reshape(nb, s, d), cap)
    ys = _ffn(xs, gs, w_expert_gate[0], w_expert_up[0], w_expert_down[0])
    return _combine(base, npass, jnp.swapaxes(sel, 1, 2), ys, h2.reshape(nb, s, d), norm_final_g[None], cap)
```

```python
import functools
import math

import jax
import jax.numpy as jnp
from jax import lax
from jax.experimental import pallas as pl
from jax.experimental.pallas import tpu as pltpu

F32 = jnp.float32
BF16 = jnp.bfloat16
I32 = jnp.int32

D_MODEL = 2048
FOURIER_WIDTH = 1024
N_GROUPS = 4
GROUP = 256
POOL_HALF = (1, 2, 4, 8)
N_EXPERTS = 16
EXPERT_FF = 1408
CAPACITY_FACTOR = 2
EPS = 1e-6

V7X_VMEM_LIMIT_BYTES = 58 * 1024 * 1024
BF16_SUBLANES = 16
LANES = 128


def _params(sem, vmem=V7X_VMEM_LIMIT_BYTES):
    return pltpu.CompilerParams(dimension_semantics=sem, vmem_limit_bytes=vmem)


def _const_spec(shape):
    nd = len(shape)
    return pl.BlockSpec(shape, lambda *_: (0,) * nd, pipeline_mode=pl.Buffered(1))


PROJ_TM = 512
PROJ_NC = 512


def _proj_body(x_ref, g_ref, win_ref, wg_ref, bg_ref, pf_ref, pp_ref, gate_ref, split_ref):
    x = x_ref[...]
    ms = jnp.mean(x * x, axis=-1, keepdims=True)
    u = (x * lax.rsqrt(ms + EPS) * g_ref[...]).astype(BF16)
    half = PROJ_TM // 2
    for j in range(FOURIER_WIDTH // PROJ_NC):
        res = jnp.dot(u, win_ref[:, j * PROJ_NC:(j + 1) * PROJ_NC], preferred_element_type=F32)
        for c in range(PROJ_NC // LANES):
            col = j * PROJ_NC + c * LANES
            split_ref[c] = res[:, c * LANES:(c + 1) * LANES]
            pf_ref[:, col:col + LANES] = split_ref[c, pl.ds(0, half, stride=2), :].astype(BF16)
            pf_ref[:, FOURIER_WIDTH + col:FOURIER_WIDTH + col + LANES] = (
                split_ref[c, pl.ds(1, half, stride=2), :].astype(BF16))
    for j in range((win_ref.shape[1] - FOURIER_WIDTH) // PROJ_NC):
        sl = slice(FOURIER_WIDTH + j * PROJ_NC, FOURIER_WIDTH + (j + 1) * PROJ_NC)
        pp_ref[:, j * PROJ_NC:(j + 1) * PROJ_NC] = jnp.dot(u, win_ref[:, sl], preferred_element_type=F32).astype(BF16)
    for j in range(wg_ref.shape[1] // PROJ_NC):
        sl = slice(j * PROJ_NC, (j + 1) * PROJ_NC)
        a = jnp.dot(u, wg_ref[:, sl], preferred_element_type=F32) + bg_ref[:, sl]
        gate_ref[:, sl] = jax.nn.sigmoid(a).astype(BF16)


def _proj(x2, g, w_in, w_gate, b_gate):
    t, d = x2.shape
    dm, dg = w_in.shape[1], w_gate.shape[1]
    return pl.pallas_call(
        _proj_body,
        grid=(t // PROJ_TM,),
        in_specs=[
            pl.BlockSpec((PROJ_TM, d), lambda i: (i, 0)),
            _const_spec((1, d)),
            _const_spec((d, dm)),
            _const_spec((d, dg)),
            _const_spec((1, dg)),
        ],
        out_specs=[
            pl.BlockSpec((PROJ_TM // 2, 2 * FOURIER_WIDTH), lambda i: (i, 0)),
            pl.BlockSpec((PROJ_TM, dm - FOURIER_WIDTH), lambda i: (i, 0)),
            pl.BlockSpec((PROJ_TM, dg), lambda i: (i, 0)),
        ],
        out_shape=[jax.ShapeDtypeStruct((t // 2, 2 * FOURIER_WIDTH), BF16),
                   jax.ShapeDtypeStruct((t, dm - FOURIER_WIDTH), BF16),
                   jax.ShapeDtypeStruct((t, dg), BF16)],
        scratch_shapes=[pltpu.VMEM((PROJ_NC // LANES, PROJ_TM, LANES), F32)],
        compiler_params=_params(("parallel",)),
        name="proj",
    )(x2, g, w_in, w_gate, b_gate)


DFT_MC = 512


def _fourier_body(trig_ref, pe_ref, po_ref, pc_ref, ps_ref):
    half = trig_ref.shape[1]
    pe = pe_ref[...]
    po = po_ref[...]
    for i in range(half // DFT_MC):
        lo = slice(i * DFT_MC, (i + 1) * DFT_MC)
        hi = slice(half + i * DFT_MC, half + (i + 1) * DFT_MC)
        for tab, out in ((0, pc_ref), (2, ps_ref)):
            ev = jnp.dot(trig_ref[tab, lo, :], pe, preferred_element_type=F32)
            od = jnp.dot(trig_ref[tab + 1, lo, :], po, preferred_element_type=F32)
            out[lo, :] = (ev + od).astype(BF16)
            out[hi, :] = (ev - od).astype(BF16)


def _fourier(trig, p_pairs, s):
    b, half, two_fw = p_pairs.shape
    odd0 = (two_fw // 2) // GROUP
    out = jax.ShapeDtypeStruct((b, s, FOURIER_WIDTH), BF16)
    return pl.pallas_call(
        _fourier_body,
        grid=(b, FOURIER_WIDTH // GROUP),
        in_specs=[
            _const_spec(trig.shape),
            pl.BlockSpec((None, half, GROUP), lambda i, g: (i, 0, g)),
            pl.BlockSpec((None, half, GROUP), lambda i, g: (i, 0, odd0 + g)),
        ],
        out_specs=[pl.BlockSpec((None, s, GROUP), lambda i, g: (i, 0, g))] * 2,
        out_shape=[out, out],
        compiler_params=_params(("arbitrary", "arbitrary")),
        name="fourier",
    )(trig, p_pairs, p_pairs)


POOL_TM = 256
POOL_HALO = 128


POOL_BAND = POOL_TM + 2 * POOL_HALO


def _pool_body(a_ref, p_ref, o_ref):
    s = a_ref.shape[0]
    for i in range(s // POOL_TM):
        rows = slice(i * POOL_TM, (i + 1) * POOL_TM)
        lo, hi = i * POOL_TM - POOL_HALO, (i + 1) * POOL_TM + POOL_HALO
        band = slice(max(-lo, 0), POOL_BAND - max(hi - s, 0))
        src = slice(max(lo, 0), min(hi, s))
        o_ref[rows, :] = jnp.dot(a_ref[rows, band], p_ref[src, :], preferred_element_type=F32).astype(BF16)


def _pool(pool_bands, pp3):
    b, s, pw = pp3.shape
    return pl.pallas_call(
        _pool_body,
        grid=(N_GROUPS, b),
        in_specs=[
            pl.BlockSpec((None, s, POOL_BAND), lambda g, i: (g, 0, 0)),
            pl.BlockSpec((None, s, GROUP), lambda g, i: (i, 0, g)),
        ],
        out_specs=pl.BlockSpec((None, s, GROUP), lambda g, i: (i, 0, g)),
        out_shape=jax.ShapeDtypeStruct((b, s, pw), BF16),
        compiler_params=_params(("arbitrary", "arbitrary")),
        name="pool",
    )(pool_bands, pp3)


TAIL_TM = 256


def _tail_body(pc_ref, ps_ref, pl_ref, gate_ref, x_ref, cc_ref, sc_ref, wf_ref, wp_ref, psc_ref, wbf_ref, wbp_ref,
               wout_ref, gm_ref, wr_ref, h_ref, v_ref, lg_ref):
    yf, yp = [], []
    for g in range(N_GROUPS):
        cols = slice(g * GROUP, (g + 1) * GROUP)
        z = (jnp.dot(pc_ref[:, cols], cc_ref[...], preferred_element_type=F32)
             - jnp.dot(ps_ref[:, cols], sc_ref[...], preferred_element_type=F32))
        yf.append(jnp.dot(z.astype(BF16), wf_ref[g], preferred_element_type=F32).astype(BF16))
        ypg = jnp.dot(pl_ref[:, cols], wp_ref[g], preferred_element_type=F32) * psc_ref[:, cols]
        yp.append(ypg.astype(BF16))
    yf = jnp.concatenate(yf, axis=-1)
    yp = jnp.concatenate(yp, axis=-1)
    d = wout_ref.shape[0]
    bf = jnp.dot(yf, wbf_ref[...], preferred_element_type=F32)
    bp = jnp.dot(yp, wbp_ref[...], preferred_element_type=F32)
    merged = gate_ref[:, :d].astype(F32) * bf + gate_ref[:, d:].astype(F32) * bp
    h = x_ref[...] + jnp.dot(merged.astype(BF16), wout_ref[...], preferred_element_type=F32)
    h_ref[...] = h
    ms = jnp.mean(h * h, axis=-1, keepdims=True)
    v = h * lax.rsqrt(ms + EPS) * gm_ref[...]
    v_hi = v.astype(BF16)
    v_lo = (v - v_hi.astype(F32)).astype(BF16)
    v_ref[...] = v_hi
    acc = (jnp.dot(v_hi, wr_ref[...], preferred_element_type=F32)
           + jnp.dot(v_lo, wr_ref[...], preferred_element_type=F32))
    lg_ref[...] = acc[:, :N_EXPERTS] + acc[:, N_EXPERTS:]


def _tail(pc2, ps2, pl2, gates, x2, cc, sc, wf, wp, pscale, wbf, wbp, wout, gm, wr2):
    t, d = x2.shape
    row = lambda width: pl.BlockSpec((TAIL_TM, width), lambda i: (i, 0))
    return pl.pallas_call(
        _tail_body,
        grid=(t // TAIL_TM,),
        in_specs=[
            row(pc2.shape[1]), row(ps2.shape[1]), row(pl2.shape[1]), row(gates.shape[1]), row(d),
            _const_spec(cc.shape), _const_spec(sc.shape), _const_spec(wf.shape), _const_spec(wp.shape),
            _const_spec(pscale.shape), _const_spec(wbf.shape), _const_spec(wbp.shape),
            _const_spec(wout.shape), _const_spec(gm.shape), _const_spec(wr2.shape),
        ],
        out_specs=[row(d), row(d), row(N_EXPERTS)],
        out_shape=[jax.ShapeDtypeStruct((t, d), F32), jax.ShapeDtypeStruct((t, d), BF16),
                   jax.ShapeDtypeStruct((t, N_EXPERTS), F32)],
        compiler_params=_params(("parallel",)),
        name="tail",
    )(pc2, ps2, pl2, gates, x2, cc, sc, wf, wp, pscale, wbf, wbp, wout, gm, wr2)


def _route_body(lg_ref, tri_ref, aff_ref, sel_ref, cum_ref, *, cap):
    nb, ne, s = lg_ref.shape
    lg = lg_ref[...]
    mx = jnp.max(lg, axis=1, keepdims=True)
    ex = jnp.exp(lg - mx)
    aff = ex / jnp.sum(ex, axis=1, keepdims=True)
    aff_ref[...] = aff
    aff2 = aff.reshape(nb * ne, s)

    def step(i, bits):
        cand = bits | jnp.left_shift(jnp.int32(1), 30 - i)
        n_ge = jnp.sum(jnp.where(aff2 >= pltpu.bitcast(cand, F32), 1.0, 0.0), axis=1, keepdims=True)
        return jnp.where(n_ge >= cap, cand, bits)

    thr = pltpu.bitcast(lax.fori_loop(0, 31, step, jnp.zeros((nb * ne, 1), I32)), F32)
    above = aff2 > thr
    tie = aff2 == thr
    n_above = jnp.sum(jnp.where(above, 1.0, 0.0), axis=1, keepdims=True)
    tri = tri_ref[...]
    tie_rank = jnp.dot(jnp.where(tie, 1.0, 0.0).astype(BF16), tri, preferred_element_type=F32)
    chosen = above | (tie & (tie_rank <= (cap - n_above)))
    cum = jnp.dot(jnp.where(chosen, 1.0, 0.0).astype(BF16), tri, preferred_element_type=F32)
    cum_ref[...] = cum.astype(I32).reshape(nb, ne, s)
    sel_ref[...] = jnp.where(chosen, cum - 1.0, -1.0).astype(I32).reshape(nb, ne, s)


def _route(lg_t, tri, cap):
    nb, ne, s = lg_t.shape
    ints = jax.ShapeDtypeStruct((nb, ne, s), I32)
    return pl.pallas_call(
        functools.partial(_route_body, cap=cap),
        out_shape=[jax.ShapeDtypeStruct((nb, ne, s), F32), ints, ints],
        compiler_params=pltpu.CompilerParams(vmem_limit_bytes=V7X_VMEM_LIMIT_BYTES),
        name="route",
    )(lg_t, tri)


TOK_BLK = 256
WIN = 64


def _window_plan(cum, cap):
    c_end = cum[:, :, TOK_BLK - 1::TOK_BLK]
    c_start = jnp.concatenate([jnp.zeros_like(c_end[:, :, :1]), c_end[:, :, :-1]], axis=-1)
    base = (c_start // BF16_SUBLANES) * BF16_SUBLANES
    passes = jnp.where(c_end > c_start, (c_end - base + WIN - 1) // WIN, 0)
    return jnp.swapaxes(base, 1, 2), jnp.max(passes, axis=1)


GATHER_DT = 1024


def _gather_body(base_s, npass_s, sel_ref, aff_ref, basec_ref, v_ref, xs_ref, gs_ref, *, cap):
    b = pl.program_id(0)
    ne, s = sel_ref.shape
    nk = s // TOK_BLK
    xs_ref[...] = jnp.zeros(xs_ref.shape, xs_ref.dtype)
    wiota = lax.broadcasted_iota(I32, (WIN, TOK_BLK), 0)
    for k in range(nk):
        toks = slice(k * TOK_BLK, (k + 1) * TOK_BLK)
        selk = sel_ref[:, toks]
        basek = basec_ref[k]

        def one_pass(p, carry, k=k, toks=toks, selk=selk, basek=basek):
            first = basek + p * WIN
            wbase = jnp.minimum(first, cap - WIN)
            rel = jnp.where(selk >= first, selk - wbase, -1)
            onehot = jnp.concatenate([jnp.where(rel[e:e + 1, :] == wiota, 1.0, 0.0) for e in range(ne)],
                                     axis=0).astype(BF16)
            rows = jnp.dot(onehot, v_ref[toks, :], preferred_element_type=F32)
            for e in range(ne):
                off = jnp.minimum(base_s[(b * nk + k) * ne + e] + p * WIN, cap - WIN)
                off = pl.multiple_of(off, BF16_SUBLANES)
                xs_ref[e, pl.ds(off, WIN), :] += rows[e * WIN:(e + 1) * WIN].astype(BF16)
            return carry

        lax.fori_loop(0, npass_s[b * nk + k], one_pass, 0)

    @pl.when(pl.program_id(1) == 0)
    def _():
        slots = lax.broadcasted_iota(I32, (cap, s), 0)
        for e in range(ne):
            own = slots == sel_ref[e:e + 1, :]
            gs_ref[e] = jnp.sum(jnp.where(own, aff_ref[e:e + 1, :], 0.0), axis=1, keepdims=True)


def _gather(base, npass, sel, aff, v3, cap):
    nb, ne, s = sel.shape
    d = v3.shape[-1]
    nk = s // TOK_BLK
    grid_spec = pltpu.PrefetchScalarGridSpec(
        num_scalar_prefetch=2,
        grid=(nb, d // GATHER_DT),
        in_specs=[
            pl.BlockSpec((None, ne, s), lambda b, j, *_: (b, 0, 0)),
            pl.BlockSpec((None, ne, s), lambda b, j, *_: (b, 0, 0)),
            pl.BlockSpec((None, nk, ne, 1), lambda b, j, *_: (b, 0, 0, 0)),
            pl.BlockSpec((None, s, GATHER_DT), lambda b, j, *_: (b, 0, j)),
        ],
        out_specs=[
            pl.BlockSpec((ne, cap, GATHER_DT), lambda b, j, *_: (0, b, j)),
            pl.BlockSpec((ne, cap, 1), lambda b, j, *_: (0, b, 0)),
        ],
    )
    return pl.pallas_call(
        functools.partial(_gather_body, cap=cap),
        grid_spec=grid_spec,
        out_shape=[jax.ShapeDtypeStruct((ne, nb * cap, d), BF16),
                   jax.ShapeDtypeStruct((ne, nb * cap, 1), F32)],
        compiler_params=_params(("arbitrary", "arbitrary")),
        name="gather",
    )(base.reshape(-1), npass.reshape(-1), sel, aff, base[..., None], v3)


def _combine_body(base_s, npass_s, selc_ref, baser_ref, y_ref, h_ref, g_ref, o_ref, acc_ref, *, cap):
    b = pl.program_id(0)
    k = pl.program_id(1)
    nk = pl.num_programs(1)
    ne = y_ref.shape[0]
    acc_ref[...] = h_ref[...]
    selk = selc_ref[...]
    basek = baser_ref[...]
    lane = lax.broadcasted_iota(I32, (ne, ne * WIN), 1)
    spread = jnp.where(lane // WIN == lax.broadcasted_iota(I32, (ne, ne * WIN), 0), 1.0, 0.0).astype(BF16)
    wlane = (lax.broadcasted_iota(I32, (1, ne * WIN), 1) % WIN).astype(F32)

    def one_pass(p, carry):
        first = basek + p * WIN
        wbase = jnp.minimum(first, cap - WIN)
        rel = jnp.where(selk >= first, selk - wbase, -1)
        relx = jnp.dot(rel.astype(F32).astype(BF16), spread, preferred_element_type=F32)
        onehot = jnp.where(relx == wlane, 1.0, 0.0).astype(BF16)
        wins = []
        for e in range(ne):
            off = jnp.minimum(base_s[(b * nk + k) * ne + e] + p * WIN, cap - WIN)
            wins.append(y_ref[e, pl.ds(pl.multiple_of(off, BF16_SUBLANES), WIN), :])
        acc_ref[...] += jnp.dot(onehot, jnp.concatenate(wins, axis=0), preferred_element_type=F32)
        return carry

    lax.fori_loop(0, npass_s[b * nk + k], one_pass, 0)
    h = acc_ref[...]
    ms = jnp.mean(h * h, axis=-1, keepdims=True)
    o_ref[...] = h * lax.rsqrt(ms + EPS) * g_ref[...]


def _combine(base, npass, sel_col, ys, h3, g_final, cap):
    nb, s, ne = sel_col.shape
    d = h3.shape[-1]
    nk = s // TOK_BLK
    grid_spec = pltpu.PrefetchScalarGridSpec(
        num_scalar_prefetch=2,
        grid=(nb, nk),
        in_specs=[
            pl.BlockSpec((None, TOK_BLK, ne), lambda b, k, *_: (b, k, 0)),
            pl.BlockSpec((None, None, 1, ne), lambda b, k, *_: (b, k, 0, 0)),
            pl.BlockSpec((ne, cap, d), lambda b, k, *_: (0, b, 0)),
            pl.BlockSpec((None, TOK_BLK, d), lambda b, k, *_: (b, k, 0)),
            pl.BlockSpec((1, d), lambda b, k, *_: (0, 0)),
        ],
        out_specs=pl.BlockSpec((None, TOK_BLK, d), lambda b, k, *_: (b, k, 0)),
        scratch_shapes=[pltpu.VMEM((TOK_BLK, d), F32)],
    )
    return pl.pallas_call(
        functools.partial(_combine_body, cap=cap),
        grid_spec=grid_spec,
        out_shape=jax.ShapeDtypeStruct((nb, s, d), F32),
        compiler_params=_params(("arbitrary", "arbitrary")),
        name="combine",
    )(base.reshape(-1), npass.reshape(-1), sel_col, base[:, :, None, :], ys, h3, g_final)


FFN_TM = 512
FFN_STEPS = 8
FFN_NC = 512


def _ffn_body(xs_ref, gs_ref, wg_ref, wu_ref, wd_ref, y_ref, wgu_scr, wd_scr, hid_scr):
    e1 = pl.program_id(0)
    j = pl.program_id(1)
    ne = pl.num_programs(0) - 1
    ff = wg_ref.shape[1]
    nt = ff // LANES

    @pl.when(e1 < ne)
    def _stage():
        slot = e1 % 2
        kr = wg_ref.shape[0]
        r0 = pl.multiple_of(j * kr, kr)
        for t in range(nt):
            src = slice(t * LANES, (t + 1) * LANES)
            wgu_scr[slot, pl.ds(r0, kr), 2 * t * LANES:(2 * t + 1) * LANES] = wg_ref[:, src].astype(BF16)
            wgu_scr[slot, pl.ds(r0, kr), (2 * t + 1) * LANES:(2 * t + 2) * LANES] = wu_ref[:, src].astype(BF16)
        fr = wd_ref.shape[0]
        wd_scr[slot, pl.ds(pl.multiple_of(j * fr, BF16_SUBLANES), fr), :] = wd_ref[...].astype(BF16)

    @pl.when((e1 > 0) & (j % 2 == 0))
    def _gate_up():
        slot = (e1 - 1) % 2
        xs = xs_ref[...]
        for t in range(nt):
            res = jnp.dot(xs, wgu_scr[slot, :, 2 * t * LANES:(2 * t + 2) * LANES], preferred_element_type=F32)
            hid_scr[:, t * LANES:(t + 1) * LANES] = (jax.nn.silu(res[:, :LANES]) * res[:, LANES:]).astype(BF16)

    @pl.when((e1 > 0) & (j % 2 == 1))
    def _down():
        slot = (e1 - 1) % 2
        hid = hid_scr[...]
        for c in range(y_ref.shape[1] // FFN_NC):
            cols = slice(c * FFN_NC, (c + 1) * FFN_NC)
            y = jnp.dot(hid, wd_scr[slot, :, cols], preferred_element_type=F32)
            y_ref[:, cols] = (y * gs_ref[...]).astype(BF16)


def _ffn(xs, gs, w_gate_e, w_up_e, w_down_e):
    ne, m, d = xs.shape
    ff = w_gate_e.shape[-1]
    assert m == FFN_TM * FFN_STEPS // 2 and d % FFN_STEPS == 0 and ff % (FFN_STEPS * BF16_SUBLANES) == 0
    assert ff % LANES == 0
    prev = lambda e1, j: (jnp.maximum(e1 - 1, 0), jnp.where(e1 == 0, 0, j // 2), 0)
    nxt = lambda e1, j: (jnp.minimum(e1, ne - 1), j, 0)
    return pl.pallas_call(
        _ffn_body,
        grid=(ne + 1, FFN_STEPS),
        in_specs=[
            pl.BlockSpec((None, FFN_TM, d), prev),
            pl.BlockSpec((None, FFN_TM, 1), prev),
            pl.BlockSpec((None, d // FFN_STEPS, ff), nxt),
            pl.BlockSpec((None, d // FFN_STEPS, ff), nxt),
            pl.BlockSpec((None, ff // FFN_STEPS, d), nxt),
        ],
        out_specs=pl.BlockSpec((None, FFN_TM, d), prev),
        out_shape=jax.ShapeDtypeStruct((ne, m, d), BF16),
        scratch_shapes=[pltpu.VMEM((2, d, 2 * ff), BF16), pltpu.VMEM((2, ff, d), BF16),
                        pltpu.VMEM((FFN_TM, ff), BF16)],
        compiler_params=_params(("arbitrary", "arbitrary")),
        name="ffn",
    )(xs, gs, w_gate_e, w_up_e, w_down_e)


TRIG_SPLIT = 32


def _trig_rows(n, rows, cols):
    ang = ((rows[:, None] * cols[None, :]) % n).astype(F32) * (2.0 * math.pi / n)
    return jnp.cos(ang), jnp.sin(ang)


def _dft_tables(n, rows, cols):
    nr = rows.shape[0]
    r1 = lax.iota(I32, nr // TRIG_SPLIT) * TRIG_SPLIT
    r0 = lax.iota(I32, TRIG_SPLIT)
    c1, s1 = _trig_rows(n, r1, cols)
    c0, s0 = _trig_rows(n, r0, cols)
    scale = 1.0 / math.sqrt(n)
    c1, s1 = (c1 * scale)[:, None, :], (s1 * scale)[:, None, :]
    c0, s0 = c0[None], s0[None]
    cos = (c1 * c0 - s1 * s0).reshape(nr, -1)
    sin = (s1 * c0 + c1 * s0).reshape(nr, -1)
    return cos, sin


def _pool_tables(s):
    i = lax.iota(I32, s)[:, None]
    k = (i // POOL_TM) * POOL_TM - POOL_HALO + lax.iota(I32, POOL_BAND)[None, :]
    out = []
    for half in POOL_HALF:
        lo = jnp.clip(i - half, 0, s)
        hi = jnp.clip(i + half, 0, s)
        inside = (k >= lo) & (k < hi)
        cnt = (hi - lo).astype(F32)
        out.append((jnp.where(inside, 1.0 / cnt, 0.0) - jnp.where(i == k, 1.0, 0.0)).astype(BF16))
    return jnp.stack(out)


def kernel(x, norm_mix_g, w_in, w_fourier_mix, w_pool_mix, pool_scale, w_branch_f, w_branch_p, w_gate,
           b_gate, w_out, norm_moe_g, w_router, w_expert_gate, w_expert_up, w_expert_down, norm_final_g):
    nb, s, d = x.shape
    assert w_in.shape[0] == 1, "single-layer block only"
    assert s % (2 * TOK_BLK) == 0 and max(POOL_HALF) <= POOL_HALO
    cap = CAPACITY_FACTOR * s // N_EXPERTS
    t = nb * s

    freqs = lax.iota(I32, s // 2)
    cos_e, sin_e = _dft_tables(s, freqs, 2 * freqs)
    cos_o, sin_o = _dft_tables(s, freqs, 2 * freqs + 1)
    trig = jnp.stack([cos_e.astype(BF16), cos_o.astype(BF16), sin_e.astype(BF16), sin_o.astype(BF16)])
    chan = lax.iota(I32, GROUP)
    cos_c, sin_c = _trig_rows(GROUP, chan, chan)
    cos_c = (cos_c / math.sqrt(GROUP)).astype(BF16)
    sin_c = (sin_c / math.sqrt(GROUP)).astype(BF16)
    tri = (lax.iota(I32, s)[:, None] <= lax.iota(I32, s)[None, :]).astype(BF16)

    x2 = x.reshape(t, d)
    pf, pp, gates = _proj(x2, norm_mix_g[0][None], w_in[0].astype(BF16), w_gate[0].astype(BF16), b_gate[0][None])
    pc, ps = _fourier(trig, pf.reshape(nb, s // 2, -1), s)
    pooled = _pool(_pool_tables(s), pp.reshape(nb, s, -1))
    wr = w_router[0]
    wr_hi = wr.astype(BF16)
    wr_lo = (wr - wr_hi.astype(F32)).astype(BF16)
    h2, v, logits = _tail(
        pc.reshape(t, -1), ps.reshape(t, -1), pooled.reshape(t, -1), gates, x2, cos_c, sin_c,
        w_fourier_mix[0].astype(BF16), w_pool_mix[0].astype(BF16), pool_scale[0][None],
        w_branch_f[0].astype(BF16), w_branch_p[0].astype(BF16), w_out[0].astype(BF16),
        norm_moe_g[0][None], jnp.concatenate([wr_hi, wr_lo], axis=1))
    lg_t = jnp.swapaxes(logits.reshape(nb, s, N_EXPERTS), 1, 2)
    aff, sel, cum = _route(lg_t, tri, cap)
    base, npass = _window_plan(cum, cap)
    xs, gs = _gather(base, npass, sel, aff, v.reshape(nb, s, d), cap)
    ys = _ffn(xs, gs, w_expert_gate[0], w_expert_up[0], w_expert_down[0])
    return _combine(base, npass, jnp.swapaxes(sel, 1, 2), ys, h2.reshape(nb, s, d), norm_final_g[None], cap)
```

```python
import functools
import math

import jax
import jax.numpy as jnp
from jax import lax
from jax.experimental import pallas as pl
from jax.experimental.pallas import tpu as pltpu

F32 = jnp.float32
BF16 = jnp.bfloat16
I32 = jnp.int32

D_MODEL = 2048
FOURIER_WIDTH = 1024
N_GROUPS = 4
GROUP = 256
POOL_HALF = (1, 2, 4, 8)
N_EXPERTS = 16
EXPERT_FF = 1408
CAPACITY_FACTOR = 2
EPS = 1e-6

V7X_VMEM_LIMIT_BYTES = 58 * 1024 * 1024
BF16_SUBLANES = 16
LANES = 128


def _params(sem, vmem=V7X_VMEM_LIMIT_BYTES):
    return pltpu.CompilerParams(dimension_semantics=sem, vmem_limit_bytes=vmem)


def _const_spec(shape):
    nd = len(shape)
    return pl.BlockSpec(shape, lambda *_: (0,) * nd, pipeline_mode=pl.Buffered(1))


PROJ_TM = 512
PROJ_NC = 512


def _proj_body(x_ref, g_ref, win_ref, wg_ref, bg_ref, pf_ref, pp_ref, gate_ref, split_ref):
    x = x_ref[...]
    ms = jnp.mean(x * x, axis=-1, keepdims=True)
    u = (x * lax.rsqrt(ms + EPS) * g_ref[...]).astype(BF16)
    half = PROJ_TM // 2
    for j in range(FOURIER_WIDTH // PROJ_NC):
        res = jnp.dot(u, win_ref[:, j * PROJ_NC:(j + 1) * PROJ_NC], preferred_element_type=F32)
        for c in range(PROJ_NC // LANES):
            col = j * PROJ_NC + c * LANES
            split_ref[c] = res[:, c * LANES:(c + 1) * LANES]
            pf_ref[:, col:col + LANES] = split_ref[c, pl.ds(0, half, stride=2), :].astype(BF16)
            pf_ref[:, FOURIER_WIDTH + col:FOURIER_WIDTH + col + LANES] = (
                split_ref[c, pl.ds(1, half, stride=2), :].astype(BF16))
    for j in range((win_ref.shape[1] - FOURIER_WIDTH) // PROJ_NC):
        sl = slice(FOURIER_WIDTH + j * PROJ_NC, FOURIER_WIDTH + (j + 1) * PROJ_NC)
        pp_ref[:, j * PROJ_NC:(j + 1) * PROJ_NC] = jnp.dot(u, win_ref[:, sl], preferred_element_type=F32).astype(BF16)
    for j in range(wg_ref.shape[1] // PROJ_NC):
        sl = slice(j * PROJ_NC, (j + 1) * PROJ_NC)
        a = jnp.dot(u, wg_ref[:, sl], preferred_element_type=F32) + bg_ref[:, sl]
        gate_ref[:, sl] = jax.nn.sigmoid(a).astype(BF16)


def _proj(x2, g, w_in, w_gate, b_gate):
    t, d = x2.shape
    dm, dg = w_in.shape[1], w_gate.shape[1]
    return pl.pallas_call(
        _proj_body,
        grid=(t // PROJ_TM,),
        in_specs=[
            pl.BlockSpec((PROJ_TM, d), lambda i: (i, 0)),
            _const_spec((1, d)),
            _const_spec((d, dm)),
            _const_spec((d, dg)),
            _const_spec((1, dg)),
        ],
        out_specs=[
            pl.BlockSpec((PROJ_TM // 2, 2 * FOURIER_WIDTH), lambda i: (i, 0)),
            pl.BlockSpec((PROJ_TM, dm - FOURIER_WIDTH), lambda i: (i, 0)),
            pl.BlockSpec((PROJ_TM, dg), lambda i: (i, 0)),
        ],
        out_shape=[jax.ShapeDtypeStruct((t // 2, 2 * FOURIER_WIDTH), BF16),
                   jax.ShapeDtypeStruct((t, dm - FOURIER_WIDTH), BF16),
                   jax.ShapeDtypeStruct((t, dg), BF16)],
        scratch_shapes=[pltpu.VMEM((PROJ_NC // LANES, PROJ_TM, LANES), F32)],
        compiler_params=_params(("parallel",)),
        name="proj",
    )(x2, g, w_in, w_gate, b_gate)


DFT_MC = 512
DFT_NC = 512


def _fourier_body(trig_ref, pe_ref, po_ref, pc_ref, ps_ref):
    half = trig_ref.shape[1]
    pe = pe_ref[...]
    po = po_ref[...]
    for i in range(half // DFT_MC):
        lo = slice(i * DFT_MC, (i + 1) * DFT_MC)
        hi = slice(half + i * DFT_MC, half + (i + 1) * DFT_MC)
        for tab, out in ((0, pc_ref), (2, ps_ref)):
            ev = jnp.dot(trig_ref[tab, lo, :], pe, preferred_element_type=F32)
            od = jnp.dot(trig_ref[tab + 1, lo, :], po, preferred_element_type=F32)
            out[lo, :] = (ev + od).astype(BF16)
            out[hi, :] = (ev - od).astype(BF16)


def _fourier(trig, p_pairs, s):
    b, half, two_fw = p_pairs.shape
    odd0 = (two_fw // 2) // DFT_NC
    out = jax.ShapeDtypeStruct((b, s, FOURIER_WIDTH), BF16)
    return pl.pallas_call(
        _fourier_body,
        grid=(b, FOURIER_WIDTH // DFT_NC),
        in_specs=[
            _const_spec(trig.shape),
            pl.BlockSpec((None, half, DFT_NC), lambda i, g: (i, 0, g)),
            pl.BlockSpec((None, half, DFT_NC), lambda i, g: (i, 0, odd0 + g)),
        ],
        out_specs=[pl.BlockSpec((None, s, DFT_NC), lambda i, g: (i, 0, g))] * 2,
        out_shape=[out, out],
        compiler_params=_params(("arbitrary", "arbitrary")),
        name="fourier",
    )(trig, p_pairs, p_pairs)


POOL_TM = 256
POOL_HALO = 128


POOL_BAND = POOL_TM + 2 * POOL_HALO


def _pool_body(a_ref, p_ref, o_ref):
    s = a_ref.shape[1]
    for g in range(N_GROUPS):
        cols = slice(g * GROUP, (g + 1) * GROUP)
        for i in range(s // POOL_TM):
            rows = slice(i * POOL_TM, (i + 1) * POOL_TM)
            lo, hi = i * POOL_TM - POOL_HALO, (i + 1) * POOL_TM + POOL_HALO
            band = slice(max(-lo, 0), POOL_BAND - max(hi - s, 0))
            src = slice(max(lo, 0), min(hi, s))
            o_ref[rows, cols] = jnp.dot(a_ref[g, rows, band], p_ref[src, cols],
                                        preferred_element_type=F32).astype(BF16)


def _pool(pool_bands, pp3):
    b, s, pw = pp3.shape
    return pl.pallas_call(
        _pool_body,
        grid=(b,),
        in_specs=[
            _const_spec(pool_bands.shape),
            pl.BlockSpec((None, s, pw), lambda i: (i, 0, 0)),
        ],
        out_specs=pl.BlockSpec((None, s, pw), lambda i: (i, 0, 0)),
        out_shape=jax.ShapeDtypeStruct((b, s, pw), BF16),
        compiler_params=_params(("arbitrary",)),
        name="pool",
    )(pool_bands, pp3)


TAIL_TM = 256


def _tail_body(pc_ref, ps_ref, pl_ref, gate_ref, x_ref, cc_ref, sc_ref, wf_ref, wp_ref, psc_ref, wbf_ref, wbp_ref,
               wout_ref, gm_ref, wr_ref, h_ref, v_ref, lg_ref):
    yf, yp = [], []
    for g in range(N_GROUPS):
        cols = slice(g * GROUP, (g + 1) * GROUP)
        z = (jnp.dot(pc_ref[:, cols], cc_ref[...], preferred_element_type=F32)
             - jnp.dot(ps_ref[:, cols], sc_ref[...], preferred_element_type=F32))
        yf.append(jnp.dot(z.astype(BF16), wf_ref[g], preferred_element_type=F32).astype(BF16))
        ypg = jnp.dot(pl_ref[:, cols], wp_ref[g], preferred_element_type=F32) * psc_ref[:, cols]
        yp.append(ypg.astype(BF16))
    yf = jnp.concatenate(yf, axis=-1)
    yp = jnp.concatenate(yp, axis=-1)
    d = wout_ref.shape[0]
    bf = jnp.dot(yf, wbf_ref[...], preferred_element_type=F32)
    bp = jnp.dot(yp, wbp_ref[...], preferred_element_type=F32)
    merged = gate_ref[:, :d].astype(F32) * bf + gate_ref[:, d:].astype(F32) * bp
    h = x_ref[...] + jnp.dot(merged.astype(BF16), wout_ref[...], preferred_element_type=F32)
    h_ref[...] = h
    ms = jnp.mean(h * h, axis=-1, keepdims=True)
    v = h * lax.rsqrt(ms + EPS) * gm_ref[...]
    v_hi = v.astype(BF16)
    v_lo = (v - v_hi.astype(F32)).astype(BF16)
    v_ref[...] = v_hi
    acc = (jnp.dot(v_hi, wr_ref[...], preferred_element_type=F32)
           + jnp.dot(v_lo, wr_ref[...], preferred_element_type=F32))
    lg_ref[...] = acc[:, :N_EXPERTS] + acc[:, N_EXPERTS:]


def _tail(pc2, ps2, pl2, gates, x2, cc, sc, wf, wp, pscale, wbf, wbp, wout, gm, wr2):
    t, d = x2.shape
    row = lambda width: pl.BlockSpec((TAIL_TM, width), lambda i: (i, 0))
    return pl.pallas_call(
        _tail_body,
        grid=(t // TAIL_TM,),
        in_specs=[
            row(pc2.shape[1]), row(ps2.shape[1]), row(pl2.shape[1]), row(gates.shape[1]), row(d),
            _const_spec(cc.shape), _const_spec(sc.shape), _const_spec(wf.shape), _const_spec(wp.shape),
            _const_spec(pscale.shape), _const_spec(wbf.shape), _const_spec(wbp.shape),
            _const_spec(wout.shape), _const_spec(gm.shape), _const_spec(wr2.shape),
        ],
        out_specs=[row(d), row(d), row(N_EXPERTS)],
        out_shape=[jax.ShapeDtypeStruct((t, d), F32), jax.ShapeDtypeStruct((t, d), BF16),
                   jax.ShapeDtypeStruct((t, N_EXPERTS), F32)],
        compiler_params=_params(("parallel",)),
        name="tail",
    )(pc2, ps2, pl2, gates, x2, cc, sc, wf, wp, pscale, wbf, wbp, wout, gm, wr2)


def _route_body(lg_ref, tri_ref, aff_ref, sel_ref, cum_ref, *, cap):
    nb, ne, s = lg_ref.shape
    lg = lg_ref[...]
    mx = jnp.max(lg, axis=1, keepdims=True)
    ex = jnp.exp(lg - mx)
    aff = ex / jnp.sum(ex, axis=1, keepdims=True)
    aff_ref[...] = aff
    aff2 = aff.reshape(nb * ne, s)

    def step(i, bits):
        cand = bits | jnp.left_shift(jnp.int32(1), 30 - i)
        n_ge = jnp.sum(jnp.where(aff2 >= pltpu.bitcast(cand, F32), 1.0, 0.0), axis=1, keepdims=True)
        return jnp.where(n_ge >= cap, cand, bits)

    thr = pltpu.bitcast(lax.fori_loop(0, 31, step, jnp.zeros((nb * ne, 1), I32)), F32)
    above = aff2 > thr
    tie = aff2 == thr
    n_above = jnp.sum(jnp.where(above, 1.0, 0.0), axis=1, keepdims=True)
    tri = tri_ref[...]
    tie_rank = jnp.dot(jnp.where(tie, 1.0, 0.0).astype(BF16), tri, preferred_element_type=F32)
    chosen = above | (tie & (tie_rank <= (cap - n_above)))
    cum = jnp.dot(jnp.where(chosen, 1.0, 0.0).astype(BF16), tri, preferred_element_type=F32)
    cum_ref[...] = cum.astype(I32).reshape(nb, ne, s)
    sel_ref[...] = jnp.where(chosen, cum - 1.0, -1.0).astype(I32).reshape(nb, ne, s)


def _route(lg_t, tri, cap):
    nb, ne, s = lg_t.shape
    ints = jax.ShapeDtypeStruct((nb, ne, s), I32)
    return pl.pallas_call(
        functools.partial(_route_body, cap=cap),
        out_shape=[jax.ShapeDtypeStruct((nb, ne, s), F32), ints, ints],
        compiler_params=pltpu.CompilerParams(vmem_limit_bytes=V7X_VMEM_LIMIT_BYTES),
        name="route",
    )(lg_t, tri)


TOK_BLK = 256
WIN = 64


def _window_plan(cum, cap):
    c_end = cum[:, :, TOK_BLK - 1::TOK_BLK]
    c_start = jnp.concatenate([jnp.zeros_like(c_end[:, :, :1]), c_end[:, :, :-1]], axis=-1)
    base = (c_start // BF16_SUBLANES) * BF16_SUBLANES
    passes = jnp.where(c_end > c_start, (c_end - base + WIN - 1) // WIN, 0)
    return jnp.swapaxes(base, 1, 2), jnp.max(passes, axis=1)


GATHER_DT = 1024


def _gather_body(base_s, npass_s, sel_ref, aff_ref, basec_ref, v_ref, xs_ref, gs_ref, *, cap):
    b = pl.program_id(0)
    ne, s = sel_ref.shape
    nk = s // TOK_BLK
    first_cols = pl.program_id(1) == 0
    xs_ref[...] = jnp.zeros(xs_ref.shape, xs_ref.dtype)

    @pl.when(first_cols)
    def _():
        gs_ref[...] = jnp.zeros(gs_ref.shape, gs_ref.dtype)

    wiota = lax.broadcasted_iota(I32, (WIN, TOK_BLK), 0)
    for k in range(nk):
        toks = slice(k * TOK_BLK, (k + 1) * TOK_BLK)
        selk = sel_ref[:, toks]
        basek = basec_ref[k]

        def one_pass(p, carry, k=k, toks=toks, selk=selk, basek=basek):
            first = basek + p * WIN
            wbase = jnp.minimum(first, cap - WIN)
            rel = jnp.where(selk >= first, selk - wbase, -1)
            hits = [rel[e:e + 1, :] == wiota for e in range(ne)]
            onehot = jnp.concatenate([jnp.where(h, 1.0, 0.0) for h in hits], axis=0).astype(BF16)
            rows = jnp.dot(onehot, v_ref[toks, :], preferred_element_type=F32)
            offs = [pl.multiple_of(jnp.minimum(base_s[(b * nk + k) * ne + e] + p * WIN, cap - WIN), BF16_SUBLANES)
                    for e in range(ne)]
            for e in range(ne):
                xs_ref[e, pl.ds(offs[e], WIN), :] += rows[e * WIN:(e + 1) * WIN].astype(BF16)

            @pl.when(first_cols)
            def _():
                for e in range(ne):
                    gate = jnp.sum(jnp.where(hits[e], aff_ref[e:e + 1, toks], 0.0), axis=1, keepdims=True)
                    gs_ref[e, pl.ds(offs[e], WIN), :] += gate
            return carry

        lax.fori_loop(0, npass_s[b * nk + k], one_pass, 0)


def _gather(base, npass, sel, aff, v3, cap):
    nb, ne, s = sel.shape
    d = v3.shape[-1]
    nk = s // TOK_BLK
    grid_spec = pltpu.PrefetchScalarGridSpec(
        num_scalar_prefetch=2,
        grid=(nb, d // GATHER_DT),
        in_specs=[
            pl.BlockSpec((None, ne, s), lambda b, j, *_: (b, 0, 0)),
            pl.BlockSpec((None, ne, s), lambda b, j, *_: (b, 0, 0)),
            pl.BlockSpec((None, nk, ne, 1), lambda b, j, *_: (b, 0, 0, 0)),
            pl.BlockSpec((None, s, GATHER_DT), lambda b, j, *_: (b, 0, j)),
        ],
        out_specs=[
            pl.BlockSpec((ne, cap, GATHER_DT), lambda b, j, *_: (0, b, j)),
            pl.BlockSpec((ne, cap, 1), lambda b, j, *_: (0, b, 0)),
        ],
    )
    return pl.pallas_call(
        functools.partial(_gather_body, cap=cap),
        grid_spec=grid_spec,
        out_shape=[jax.ShapeDtypeStruct((ne, nb * cap, d), BF16),
                   jax.ShapeDtypeStruct((ne, nb * cap, 1), F32)],
        compiler_params=_params(("arbitrary", "arbitrary")),
        name="gather",
    )(base.reshape(-1), npass.reshape(-1), sel, aff, base[..., None], v3)


def _combine_body(base_s, npass_s, selc_ref, baser_ref, y_ref, h_ref, g_ref, o_ref, acc_ref, *, cap):
    b = pl.program_id(0)
    k = pl.program_id(1)
    nk = pl.num_programs(1)
    ne = y_ref.shape[0]
    acc_ref[...] = h_ref[...]
    selk = selc_ref[...]
    basek = baser_ref[...]
    lane = lax.broadcasted_iota(I32, (ne, ne * WIN), 1)
    spread = jnp.where(lane // WIN == lax.broadcasted_iota(I32, (ne, ne * WIN), 0), 1.0, 0.0).astype(BF16)
    wlane = (lax.broadcasted_iota(I32, (1, ne * WIN), 1) % WIN).astype(F32)

    def one_pass(p, carry):
        first = basek + p * WIN
        wbase = jnp.minimum(first, cap - WIN)
        rel = jnp.where(selk >= first, selk - wbase, -1)
        relx = jnp.dot(rel.astype(F32).astype(BF16), spread, preferred_element_type=F32)
        onehot = jnp.where(relx == wlane, 1.0, 0.0).astype(BF16)
        wins = []
        for e in range(ne):
            off = jnp.minimum(base_s[(b * nk + k) * ne + e] + p * WIN, cap - WIN)
            wins.append(y_ref[e, pl.ds(pl.multiple_of(off, BF16_SUBLANES), WIN), :])
        acc_ref[...] += jnp.dot(onehot, jnp.concatenate(wins, axis=0), preferred_element_type=F32)
        return carry

    lax.fori_loop(0, npass_s[b * nk + k], one_pass, 0)
    h = acc_ref[...]
    ms = jnp.mean(h * h, axis=-1, keepdims=True)
    o_ref[...] = h * lax.rsqrt(ms + EPS) * g_ref[...]


def _combine(base, npass, sel_col, ys, h3, g_final, cap):
    nb, s, ne = sel_col.shape
    d = h3.shape[-1]
    nk = s // TOK_BLK
    grid_spec = pltpu.PrefetchScalarGridSpec(
        num_scalar_prefetch=2,
        grid=(nb, nk),
        in_specs=[
            pl.BlockSpec((None, TOK_BLK, ne), lambda b, k, *_: (b, k, 0)),
            pl.BlockSpec((None, None, 1, ne), lambda b, k, *_: (b, k, 0, 0)),
            pl.BlockSpec((ne, cap, d), lambda b, k, *_: (0, b, 0)),
            pl.BlockSpec((None, TOK_BLK, d), lambda b, k, *_: (b, k, 0)),
            pl.BlockSpec((1, d), lambda b, k, *_: (0, 0)),
        ],
        out_specs=pl.BlockSpec((None, TOK_BLK, d), lambda b, k, *_: (b, k, 0)),
        scratch_shapes=[pltpu.VMEM((TOK_BLK, d), F32)],
    )
    return pl.pallas_call(
        functools.partial(_combine_body, cap=cap),
        grid_spec=grid_spec,
        out_shape=jax.ShapeDtypeStruct((nb, s, d), F32),
        compiler_params=_params(("arbitrary", "arbitrary")),
        name="combine",
    )(base.reshape(-1), npass.reshape(-1), sel_col, base[:, :, None, :], ys, h3, g_final)


FFN_TM = 512
FFN_STEPS = 8
FFN_NC = 512


def _ffn_body(xs_ref, gs_ref, wg_ref, wu_ref, wd_ref, y_ref, wgu_scr, wd_scr, hid_scr):
    e1 = pl.program_id(0)
    j = pl.program_id(1)
    ne = pl.num_programs(0) - 1
    ff = wg_ref.shape[1]
    nt = ff // LANES

    @pl.when(e1 < ne)
    def _stage():
        slot = e1 % 2
        kr = wg_ref.shape[0]
        r0 = pl.multiple_of(j * kr, kr)
        for t in range(nt):
            src = slice(t * LANES, (t + 1) * LANES)
            wgu_scr[slot, pl.ds(r0, kr), 2 * t * LANES:(2 * t + 1) * LANES] = wg_ref[:, src].astype(BF16)
            wgu_scr[slot, pl.ds(r0, kr), (2 * t + 1) * LANES:(2 * t + 2) * LANES] = wu_ref[:, src].astype(BF16)
        fr = wd_ref.shape[0]
        wd_scr[slot, pl.ds(pl.multiple_of(j * fr, BF16_SUBLANES), fr), :] = wd_ref[...].astype(BF16)

    @pl.when((e1 > 0) & (j % 2 == 0))
    def _gate_up():
        slot = (e1 - 1) % 2
        xs = xs_ref[...]
        for t in range(nt):
            res = jnp.dot(xs, wgu_scr[slot, :, 2 * t * LANES:(2 * t + 2) * LANES], preferred_element_type=F32)
            hid_scr[:, t * LANES:(t + 1) * LANES] = (jax.nn.silu(res[:, :LANES]) * res[:, LANES:]).astype(BF16)

    @pl.when((e1 > 0) & (j % 2 == 1))
    def _down():
        slot = (e1 - 1) % 2
        hid = hid_scr[...]
        for c in range(y_ref.shape[1] // FFN_NC):
            cols = slice(c * FFN_NC, (c + 1) * FFN_NC)
            y = jnp.dot(hid, wd_scr[slot, :, cols], preferred_element_type=F32)
            y_ref[:, cols] = (y * gs_ref[...]).astype(BF16)


def _ffn(xs, gs, w_gate_e, w_up_e, w_down_e):
    ne, m, d = xs.shape
    ff = w_gate_e.shape[-1]
    assert m == FFN_TM * FFN_STEPS // 2 and d % FFN_STEPS == 0 and ff % (FFN_STEPS * BF16_SUBLANES) == 0
    assert ff % LANES == 0
    prev = lambda e1, j: (jnp.maximum(e1 - 1, 0), jnp.where(e1 == 0, 0, j // 2), 0)
    nxt = lambda e1, j: (jnp.minimum(e1, ne - 1), j, 0)
    return pl.pallas_call(
        _ffn_body,
        grid=(ne + 1, FFN_STEPS),
        in_specs=[
            pl.BlockSpec((None, FFN_TM, d), prev),
            pl.BlockSpec((None, FFN_TM, 1), prev),
            pl.BlockSpec((None, d // FFN_STEPS, ff), nxt),
            pl.BlockSpec((None, d // FFN_STEPS, ff), nxt),
            pl.BlockSpec((None, ff // FFN_STEPS, d), nxt),
        ],
        out_specs=pl.BlockSpec((None, FFN_TM, d), prev),
        out_shape=jax.ShapeDtypeStruct((ne, m, d), BF16),
        scratch_shapes=[pltpu.VMEM((2, d, 2 * ff), BF16), pltpu.VMEM((2, ff, d), BF16),
                        pltpu.VMEM((FFN_TM, ff), BF16)],
        compiler_params=_params(("arbitrary", "arbitrary")),
        name="ffn",
    )(xs, gs, w_gate_e, w_up_e, w_down_e)


TRIG_SPLIT = 32


def _trig_rows(n, rows, cols):
    ang = ((rows[:, None] * cols[None, :]) % n).astype(F32) * (2.0 * math.pi / n)
    return jnp.cos(ang), jnp.sin(ang)


def _dft_tables(n, rows, cols):
    nr = rows.shape[0]
    r1 = lax.iota(I32, nr // TRIG_SPLIT) * TRIG_SPLIT
    r0 = lax.iota(I32, TRIG_SPLIT)
    c1, s1 = _trig_rows(n, r1, cols)
    c0, s0 = _trig_rows(n, r0, cols)
    scale = 1.0 / math.sqrt(n)
    c1, s1 = (c1 * scale)[:, None, :], (s1 * scale)[:, None, :]
    c0, s0 = c0[None], s0[None]
    cos = (c1 * c0 - s1 * s0).reshape(nr, -1)
    sin = (s1 * c0 + c1 * s0).reshape(nr, -1)
    return cos, sin


def _pool_tables(s):
    i = lax.iota(I32, s)[:, None]
    k = (i // POOL_TM) * POOL_TM - POOL_HALO + lax.iota(I32, POOL_BAND)[None, :]
    out = []
    for half in POOL_HALF:
        lo = jnp.clip(i - half, 0, s)
        hi = jnp.clip(i + half, 0, s)
        inside = (k >= lo) & (k < hi)
        cnt = (hi - lo).astype(F32)
        out.append((jnp.where(inside, 1.0 / cnt, 0.0) - jnp.where(i == k, 1.0, 0.0)).astype(BF16))
    return jnp.stack(out)


def kernel(x, norm_mix_g, w_in, w_fourier_mix, w_pool_mix, pool_scale, w_branch_f, w_branch_p, w_gate,
           b_gate, w_out, norm_moe_g, w_router, w_expert_gate, w_expert_up, w_expert_down, norm_final_g):
    nb, s, d = x.shape
    assert w_in.shape[0] == 1, "single-layer block only"
    assert s % (2 * TOK_BLK) == 0 and max(POOL_HALF) <= POOL_HALO
    cap = CAPACITY_FACTOR * s // N_EXPERTS
    t = nb * s

    freqs = lax.iota(I32, s // 2)
    cos_e, sin_e = _dft_tables(s, freqs, 2 * freqs)
    cos_o, sin_o = _dft_tables(s, freqs, 2 * freqs + 1)
    trig = jnp.stack([cos_e.astype(BF16), cos_o.astype(BF16), sin_e.astype(BF16), sin_o.astype(BF16)])
    chan = lax.iota(I32, GROUP)
    cos_c, sin_c = _trig_rows(GROUP, chan, chan)
    cos_c = (cos_c / math.sqrt(GROUP)).astype(BF16)
    sin_c = (sin_c / math.sqrt(GROUP)).astype(BF16)
    tri = (lax.iota(I32, s)[:, None] <= lax.iota(I32, s)[None, :]).astype(BF16)

    x2 = x.reshape(t, d)
    pf, pp, gates = _proj(x2, norm_mix_g[0][None], w_in[0].astype(BF16), w_gate[0].astype(BF16), b_gate[0][None])
    pc, ps = _fourier(trig, pf.reshape(nb, s // 2, -1), s)
    pooled = _pool(_pool_tables(s), pp.reshape(nb, s, -1))
    wr = w_router[0]
    wr_hi = wr.astype(BF16)
    wr_lo = (wr - wr_hi.astype(F32)).astype(BF16)
    h2, v, logits = _tail(
        pc.reshape(t, -1), ps.reshape(t, -1), pooled.reshape(t, -1), gates, x2, cos_c, sin_c,
        w_fourier_mix[0].astype(BF16), w_pool_mix[0].astype(BF16), pool_scale[0][None],
        w_branch_f[0].astype(BF16), w_branch_p[0].astype(BF16), w_out[0].astype(BF16),
        norm_moe_g[0][None], jnp.concatenate([wr_hi, wr_lo], axis=1))
    lg_t = jnp.swapaxes(logits.reshape(nb, s, N_EXPERTS), 1, 2)
    aff, sel, cum = _route(lg_t, tri, cap)
    base, npass = _window_plan(cum, cap)
    xs, gs = _gather(base, npass, sel, aff, v.reshape(nb, s, d), cap)
    ys = _ffn(xs, gs, w_expert_gate[0], w_expert_up[0], w_expert_down[0])
    return _combine(base, npass, jnp.swapaxes(sel, 1, 2), ys, h2.reshape(nb, s, d), norm_final_g[None], cap)
```

```python
import functools
import math

import jax
import jax.numpy as jnp
from jax import lax
from jax.experimental import pallas as pl
from jax.experimental.pallas import tpu as pltpu

F32 = jnp.float32
BF16 = jnp.bfloat16
I32 = jnp.int32

D_MODEL = 2048
FOURIER_WIDTH = 1024
N_GROUPS = 4
GROUP = 256
POOL_HALF = (1, 2, 4, 8)
N_EXPERTS = 16
EXPERT_FF = 1408
CAPACITY_FACTOR = 2
EPS = 1e-6

V7X_VMEM_LIMIT_BYTES = 58 * 1024 * 1024
BF16_SUBLANES = 16
LANES = 128


def _params(sem, vmem=V7X_VMEM_LIMIT_BYTES):
    return pltpu.CompilerParams(dimension_semantics=sem, vmem_limit_bytes=vmem)


def _const_spec(shape):
    nd = len(shape)
    return pl.BlockSpec(shape, lambda *_: (0,) * nd, pipeline_mode=pl.Buffered(1))


PROJ_TM = 512
PROJ_NC = 512
DFT_RADIX = 4


def _proj_body(x_ref, g_ref, win_ref, wg_ref, bg_ref, pf_ref, pp_ref, gate_ref, split_ref):
    x = x_ref[...]
    ms = jnp.mean(x * x, axis=-1, keepdims=True)
    u = (x * lax.rsqrt(ms + EPS) * g_ref[...]).astype(BF16)
    part = PROJ_TM // DFT_RADIX
    for j in range(FOURIER_WIDTH // PROJ_NC):
        res = jnp.dot(u, win_ref[:, j * PROJ_NC:(j + 1) * PROJ_NC], preferred_element_type=F32)
        for c in range(PROJ_NC // LANES):
            col = j * PROJ_NC + c * LANES
            split_ref[c] = res[:, c * LANES:(c + 1) * LANES]
            for r in range(DFT_RADIX):
                pf_ref[:, r * FOURIER_WIDTH + col:r * FOURIER_WIDTH + col + LANES] = (
                    split_ref[c, pl.ds(r, part, stride=DFT_RADIX), :].astype(BF16))
    for j in range((win_ref.shape[1] - FOURIER_WIDTH) // PROJ_NC):
        sl = slice(FOURIER_WIDTH + j * PROJ_NC, FOURIER_WIDTH + (j + 1) * PROJ_NC)
        pp_ref[:, j * PROJ_NC:(j + 1) * PROJ_NC] = jnp.dot(u, win_ref[:, sl], preferred_element_type=F32).astype(BF16)
    for j in range(wg_ref.shape[1] // PROJ_NC):
        sl = slice(j * PROJ_NC, (j + 1) * PROJ_NC)
        a = jnp.dot(u, wg_ref[:, sl], preferred_element_type=F32) + bg_ref[:, sl]
        gate_ref[:, sl] = jax.nn.sigmoid(a).astype(BF16)


def _proj(x2, g, w_in, w_gate, b_gate):
    t, d = x2.shape
    dm, dg = w_in.shape[1], w_gate.shape[1]
    return pl.pallas_call(
        _proj_body,
        grid=(t // PROJ_TM,),
        in_specs=[
            pl.BlockSpec((PROJ_TM, d), lambda i: (i, 0)),
            _const_spec((1, d)),
            _const_spec((d, dm)),
            _const_spec((d, dg)),
            _const_spec((1, dg)),
        ],
        out_specs=[
            pl.BlockSpec((PROJ_TM // DFT_RADIX, DFT_RADIX * FOURIER_WIDTH), lambda i: (i, 0)),
            pl.BlockSpec((PROJ_TM, dm - FOURIER_WIDTH), lambda i: (i, 0)),
            pl.BlockSpec((PROJ_TM, dg), lambda i: (i, 0)),
        ],
        out_shape=[jax.ShapeDtypeStruct((t // DFT_RADIX, DFT_RADIX * FOURIER_WIDTH), BF16),
                   jax.ShapeDtypeStruct((t, dm - FOURIER_WIDTH), BF16),
                   jax.ShapeDtypeStruct((t, dg), BF16)],
        scratch_shapes=[pltpu.VMEM((PROJ_NC // LANES, PROJ_TM, LANES), F32)],
        compiler_params=_params(("parallel",)),
        name="proj",
    )(x2, g, w_in, w_gate, b_gate)


DFT_MC = 256
DFT_NC = 512


def _fourier_body(trig_ref, *refs):
    p_refs, (pc_ref, ps_ref) = refs[:DFT_RADIX], refs[DFT_RADIX:]
    part = trig_ref.shape[1]
    p_in = [r[...] for r in p_refs]
    for i in range(part // DFT_MC):
        rows = slice(i * DFT_MC, (i + 1) * DFT_MC)
        c = [jnp.dot(trig_ref[2 * r, rows, :], p_in[r], preferred_element_type=F32) for r in range(DFT_RADIX)]
        s = [jnp.dot(trig_ref[2 * r + 1, rows, :], p_in[r], preferred_element_type=F32) for r in range(DFT_RADIX)]
        ec, oc, es, os_ = c[0] + c[2], c[0] - c[2], s[0] + s[2], s[0] - s[2]
        fc, gc, fs, gs = c[1] + c[3], c[1] - c[3], s[1] + s[3], s[1] - s[3]
        xc = (ec + fc, oc - gs, ec - fc, oc + gs)
        xs = (es + fs, os_ + gc, es - fs, os_ - gc)
        for q in range(DFT_RADIX):
            out_rows = slice(q * part + i * DFT_MC, q * part + (i + 1) * DFT_MC)
            pc_ref[out_rows, :] = xc[q].astype(BF16)
            ps_ref[out_rows, :] = xs[q].astype(BF16)


def _fourier(trig, p_split, s):
    assert DFT_RADIX == 4
    b, part, _ = p_split.shape
    per_res = FOURIER_WIDTH // DFT_NC
    out = jax.ShapeDtypeStruct((b, s, FOURIER_WIDTH), BF16)
    res_spec = lambda r: pl.BlockSpec((None, part, DFT_NC), lambda i, g: (i, 0, r * per_res + g))
    return pl.pallas_call(
        _fourier_body,
        grid=(b, per_res),
        in_specs=[_const_spec(trig.shape)] + [res_spec(r) for r in range(DFT_RADIX)],
        out_specs=[pl.BlockSpec((None, s, DFT_NC), lambda i, g: (i, 0, g))] * 2,
        out_shape=[out, out],
        compiler_params=_params(("arbitrary", "arbitrary")),
        name="fourier",
    )(trig, *([p_split] * DFT_RADIX))


POOL_TM = 256
POOL_HALO = 128


POOL_BAND = POOL_TM + 2 * POOL_HALO


def _pool_body(a_ref, p_ref, o_ref):
    s = a_ref.shape[1]
    for g in range(N_GROUPS):
        cols = slice(g * GROUP, (g + 1) * GROUP)
        for i in range(s // POOL_TM):
            rows = slice(i * POOL_TM, (i + 1) * POOL_TM)
            lo, hi = i * POOL_TM - POOL_HALO, (i + 1) * POOL_TM + POOL_HALO
            band = slice(max(-lo, 0), POOL_BAND - max(hi - s, 0))
            src = slice(max(lo, 0), min(hi, s))
            o_ref[rows, cols] = jnp.dot(a_ref[g, rows, band], p_ref[src, cols],
                                        preferred_element_type=F32).astype(BF16)


def _pool(pool_bands, pp3):
    b, s, pw = pp3.shape
    return pl.pallas_call(
        _pool_body,
        grid=(b,),
        in_specs=[
            _const_spec(pool_bands.shape),
            pl.BlockSpec((None, s, pw), lambda i: (i, 0, 0)),
        ],
        out_specs=pl.BlockSpec((None, s, pw), lambda i: (i, 0, 0)),
        out_shape=jax.ShapeDtypeStruct((b, s, pw), BF16),
        compiler_params=_params(("arbitrary",)),
        name="pool",
    )(pool_bands, pp3)


TAIL_TM = 256


def _tail_body(pc_ref, ps_ref, pl_ref, gate_ref, x_ref, cc_ref, sc_ref, wf_ref, wp_ref, psc_ref, wbf_ref, wbp_ref,
               wout_ref, gm_ref, wr_ref, h_ref, v_ref, lg_ref):
    yf, yp = [], []
    for g in range(N_GROUPS):
        cols = slice(g * GROUP, (g + 1) * GROUP)
        z = (jnp.dot(pc_ref[:, cols], cc_ref[...], preferred_element_type=F32)
             - jnp.dot(ps_ref[:, cols], sc_ref[...], preferred_element_type=F32))
        yf.append(jnp.dot(z.astype(BF16), wf_ref[g], preferred_element_type=F32).astype(BF16))
        ypg = jnp.dot(pl_ref[:, cols], wp_ref[g], preferred_element_type=F32) * psc_ref[:, cols]
        yp.append(ypg.astype(BF16))
    yf = jnp.concatenate(yf, axis=-1)
    yp = jnp.concatenate(yp, axis=-1)
    d = wout_ref.shape[0]
    bf = jnp.dot(yf, wbf_ref[...], preferred_element_type=F32)
    bp = jnp.dot(yp, wbp_ref[...], preferred_element_type=F32)
    merged = gate_ref[:, :d].astype(F32) * bf + gate_ref[:, d:].astype(F32) * bp
    h = x_ref[...] + jnp.dot(merged.astype(BF16), wout_ref[...], preferred_element_type=F32)
    h_ref[...] = h
    ms = jnp.mean(h * h, axis=-1, keepdims=True)
    v = h * lax.rsqrt(ms + EPS) * gm_ref[...]
    v_hi = v.astype(BF16)
    v_lo = (v - v_hi.astype(F32)).astype(BF16)
    v_ref[...] = v_hi
    acc = (jnp.dot(v_hi, wr_ref[...], preferred_element_type=F32)
           + jnp.dot(v_lo, wr_ref[...], preferred_element_type=F32))
    lg_ref[...] = acc[:, :N_EXPERTS] + acc[:, N_EXPERTS:]


def _tail(pc2, ps2, pl2, gates, x2, cc, sc, wf, wp, pscale, wbf, wbp, wout, gm, wr2):
    t, d = x2.shape
    row = lambda width: pl.BlockSpec((TAIL_TM, width), lambda i: (i, 0))
    return pl.pallas_call(
        _tail_body,
        grid=(t // TAIL_TM,),
        in_specs=[
            row(pc2.shape[1]), row(ps2.shape[1]), row(pl2.shape[1]), row(gates.shape[1]), row(d),
            _const_spec(cc.shape), _const_spec(sc.shape), _const_spec(wf.shape), _const_spec(wp.shape),
            _const_spec(pscale.shape), _const_spec(wbf.shape), _const_spec(wbp.shape),
            _const_spec(wout.shape), _const_spec(gm.shape), _const_spec(wr2.shape),
        ],
        out_specs=[row(d), row(d), row(N_EXPERTS)],
        out_shape=[jax.ShapeDtypeStruct((t, d), F32), jax.ShapeDtypeStruct((t, d), BF16),
                   jax.ShapeDtypeStruct((t, N_EXPERTS), F32)],
        compiler_params=_params(("parallel",)),
        name="tail",
    )(pc2, ps2, pl2, gates, x2, cc, sc, wf, wp, pscale, wbf, wbp, wout, gm, wr2)


def _route_body(lg_ref, tri_ref, aff_ref, sel_ref, cum_ref, *, cap):
    nb, ne, s = lg_ref.shape
    lg = lg_ref[...]
    mx = jnp.max(lg, axis=1, keepdims=True)
    ex = jnp.exp(lg - mx)
    aff = ex / jnp.sum(ex, axis=1, keepdims=True)
    aff_ref[...] = aff
    aff2 = aff.reshape(nb * ne, s)

    def step(i, bits):
        cand = bits | jnp.left_shift(jnp.int32(1), 30 - i)
        n_ge = jnp.sum(jnp.where(aff2 >= pltpu.bitcast(cand, F32), 1.0, 0.0), axis=1, keepdims=True)
        return jnp.where(n_ge >= cap, cand, bits)

    thr = pltpu.bitcast(lax.fori_loop(0, 31, step, jnp.zeros((nb * ne, 1), I32)), F32)
    above = aff2 > thr
    tie = aff2 == thr
    n_above = jnp.sum(jnp.where(above, 1.0, 0.0), axis=1, keepdims=True)
    tri = tri_ref[...]
    tie_rank = jnp.dot(jnp.where(tie, 1.0, 0.0).astype(BF16), tri, preferred_element_type=F32)
    chosen = above | (tie & (tie_rank <= (cap - n_above)))
    cum = jnp.dot(jnp.where(chosen, 1.0, 0.0).astype(BF16), tri, preferred_element_type=F32)
    cum_ref[...] = cum.astype(I32).reshape(nb, ne, s)
    sel_ref[...] = jnp.where(chosen, cum - 1.0, -1.0).astype(I32).reshape(nb, ne, s)


def _route(lg_t, tri, cap):
    nb, ne, s = lg_t.shape
    ints = jax.ShapeDtypeStruct((nb, ne, s), I32)
    return pl.pallas_call(
        functools.partial(_route_body, cap=cap),
        out_shape=[jax.ShapeDtypeStruct((nb, ne, s), F32), ints, ints],
        compiler_params=pltpu.CompilerParams(vmem_limit_bytes=V7X_VMEM_LIMIT_BYTES),
        name="route",
    )(lg_t, tri)


TOK_BLK = 256
WIN = 64


def _window_plan(cum, cap):
    c_end = cum[:, :, TOK_BLK - 1::TOK_BLK]
    c_start = jnp.concatenate([jnp.zeros_like(c_end[:, :, :1]), c_end[:, :, :-1]], axis=-1)
    base = (c_start // BF16_SUBLANES) * BF16_SUBLANES
    passes = jnp.where(c_end > c_start, (c_end - base + WIN - 1) // WIN, 0)
    return jnp.swapaxes(base, 1, 2), jnp.max(passes, axis=1)


GATHER_DT = 1024


def _gather_body(base_s, npass_s, sel_ref, aff_ref, basec_ref, v_ref, xs_ref, gs_ref, *, cap):
    b = pl.program_id(0)
    ne, s = sel_ref.shape
    nk = s // TOK_BLK
    first_cols = pl.program_id(1) == 0
    xs_ref[...] = jnp.zeros(xs_ref.shape, xs_ref.dtype)

    @pl.when(first_cols)
    def _():
        gs_ref[...] = jnp.zeros(gs_ref.shape, gs_ref.dtype)

    wiota = lax.broadcasted_iota(I32, (WIN, TOK_BLK), 0)
    for k in range(nk):
        toks = slice(k * TOK_BLK, (k + 1) * TOK_BLK)
        selk = sel_ref[:, toks]
        basek = basec_ref[k]

        def one_pass(p, carry, k=k, toks=toks, selk=selk, basek=basek):
            first = basek + p * WIN
            wbase = jnp.minimum(first, cap - WIN)
            rel = jnp.where(selk >= first, selk - wbase, -1)
            hits = [rel[e:e + 1, :] == wiota for e in range(ne)]
            onehot = jnp.concatenate([jnp.where(h, 1.0, 0.0) for h in hits], axis=0).astype(BF16)
            rows = jnp.dot(onehot, v_ref[toks, :], preferred_element_type=F32)
            offs = [pl.multiple_of(jnp.minimum(base_s[(b * nk + k) * ne + e] + p * WIN, cap - WIN), BF16_SUBLANES)
                    for e in range(ne)]
            for e in range(ne):
                xs_ref[e, pl.ds(offs[e], WIN), :] += rows[e * WIN:(e + 1) * WIN].astype(BF16)

            @pl.when(first_cols)
            def _():
                for e in range(ne):
                    gate = jnp.sum(jnp.where(hits[e], aff_ref[e:e + 1, toks], 0.0), axis=1, keepdims=True)
                    gs_ref[e, pl.ds(offs[e], WIN), :] += gate
            return carry

        lax.fori_loop(0, npass_s[b * nk + k], one_pass, 0)


def _gather(base, npass, sel, aff, v3, cap):
    nb, ne, s = sel.shape
    d = v3.shape[-1]
    nk = s // TOK_BLK
    grid_spec = pltpu.PrefetchScalarGridSpec(
        num_scalar_prefetch=2,
        grid=(nb, d // GATHER_DT),
        in_specs=[
            pl.BlockSpec((None, ne, s), lambda b, j, *_: (b, 0, 0)),
            pl.BlockSpec((None, ne, s), lambda b, j, *_: (b, 0, 0)),
            pl.BlockSpec((None, nk, ne, 1), lambda b, j, *_: (b, 0, 0, 0)),
            pl.BlockSpec((None, s, GATHER_DT), lambda b, j, *_: (b, 0, j)),
        ],
        out_specs=[
            pl.BlockSpec((ne, cap, GATHER_DT), lambda b, j, *_: (0, b, j)),
            pl.BlockSpec((ne, cap, 1), lambda b, j, *_: (0, b, 0)),
        ],
    )
    return pl.pallas_call(
        functools.partial(_gather_body, cap=cap),
        grid_spec=grid_spec,
        out_shape=[jax.ShapeDtypeStruct((ne, nb * cap, d), BF16),
                   jax.ShapeDtypeStruct((ne, nb * cap, 1), F32)],
        compiler_params=_params(("arbitrary", "arbitrary")),
        name="gather",
    )(base.reshape(-1), npass.reshape(-1), sel, aff, base[..., None], v3)


def _combine_body(base_s, npass_s, selc_ref, baser_ref, y_ref, h_ref, g_ref, o_ref, acc_ref, *, cap):
    b = pl.program_id(0)
    k = pl.program_id(1)
    nk = pl.num_programs(1)
    ne = y_ref.shape[0]
    acc_ref[...] = h_ref[...]
    selk = selc_ref[...]
    basek = baser_ref[...]
    lane = lax.broadcasted_iota(I32, (ne, ne * WIN), 1)
    spread = jnp.where(lane // WIN == lax.broadcasted_iota(I32, (ne, ne * WIN), 0), 1.0, 0.0).astype(BF16)
    wlane = (lax.broadcasted_iota(I32, (1, ne * WIN), 1) % WIN).astype(F32)

    def one_pass(p, carry):
        first = basek + p * WIN
        wbase = jnp.minimum(first, cap - WIN)
        rel = jnp.where(selk >= first, selk - wbase, -1)
        relx = jnp.dot(rel.astype(F32).astype(BF16), spread, preferred_element_type=F32)
        onehot = jnp.where(relx == wlane, 1.0, 0.0).astype(BF16)
        wins = []
        for e in range(ne):
            off = jnp.minimum(base_s[(b * nk + k) * ne + e] + p * WIN, cap - WIN)
            wins.append(y_ref[e, pl.ds(pl.multiple_of(off, BF16_SUBLANES), WIN), :])
        acc_ref[...] += jnp.dot(onehot, jnp.concatenate(wins, axis=0), preferred_element_type=F32)
        return carry

    lax.fori_loop(0, npass_s[b * nk + k], one_pass, 0)
    h = acc_ref[...]
    ms = jnp.mean(h * h, axis=-1, keepdims=True)
    o_ref[...] = h * lax.rsqrt(ms + EPS) * g_ref[...]


def _combine(base, npass, sel_col, ys, h3, g_final, cap):
    nb, s, ne = sel_col.shape
    d = h3.shape[-1]
    nk = s // TOK_BLK
    grid_spec = pltpu.PrefetchScalarGridSpec(
        num_scalar_prefetch=2,
        grid=(nb, nk),
        in_specs=[
            pl.BlockSpec((None, TOK_BLK, ne), lambda b, k, *_: (b, k, 0)),
            pl.BlockSpec((None, None, 1, ne), lambda b, k, *_: (b, k, 0, 0)),
            pl.BlockSpec((ne, cap, d), lambda b, k, *_: (0, b, 0)),
            pl.BlockSpec((None, TOK_BLK, d), lambda b, k, *_: (b, k, 0)),
            pl.BlockSpec((1, d), lambda b, k, *_: (0, 0)),
        ],
        out_specs=pl.BlockSpec((None, TOK_BLK, d), lambda b, k, *_: (b, k, 0)),
        scratch_shapes=[pltpu.VMEM((TOK_BLK, d), F32)],
    )
    return pl.pallas_call(
        functools.partial(_combine_body, cap=cap),
        grid_spec=grid_spec,
        out_shape=jax.ShapeDtypeStruct((nb, s, d), F32),
        compiler_params=_params(("arbitrary", "arbitrary")),
        name="combine",
    )(base.reshape(-1), npass.reshape(-1), sel_col, base[:, :, None, :], ys, h3, g_final)


FFN_TM = 512
FFN_STEPS = 8
FFN_NC = 512


def _ffn_body(xs_ref, gs_ref, wg_ref, wu_ref, wd_ref, y_ref, wgu_scr, wd_scr, hid_scr):
    e1 = pl.program_id(0)
    j = pl.program_id(1)
    ne = pl.num_programs(0) - 1
    ff = wg_ref.shape[1]
    nt = ff // LANES

    nc = y_ref.shape[1] // FFN_NC
    stage_slot = e1 % 2
    slot = (e1 - 1) % 2
    kr, fr = wg_ref.shape[0], wd_ref.shape[0]
    r0 = pl.multiple_of(j * kr, kr)
    f0 = pl.multiple_of(j * fr, BF16_SUBLANES)

    def stage_gate_up(t):
        src = slice(t * LANES, (t + 1) * LANES)
        wgu_scr[stage_slot, pl.ds(r0, kr), 2 * t * LANES:(2 * t + 1) * LANES] = wg_ref[:, src].astype(BF16)
        wgu_scr[stage_slot, pl.ds(r0, kr), (2 * t + 1) * LANES:(2 * t + 2) * LANES] = wu_ref[:, src].astype(BF16)

    def stage_down(c):
        cols = slice(c * FFN_NC, (c + 1) * FFN_NC)
        wd_scr[stage_slot, pl.ds(f0, fr), cols] = wd_ref[:, cols].astype(BF16)

    pieces = [functools.partial(stage_gate_up, t) for t in range(nt)] + [
        functools.partial(stage_down, c) for c in range(nc)]

    def spread(n_chunks):
        return [pieces[i::n_chunks] for i in range(n_chunks)]

    @pl.when(e1 == 0)
    def _load_first_expert():
        for piece in pieces:
            piece()

    @pl.when((e1 > 0) & (j % 2 == 0))
    def _gate_up():
        xs = xs_ref[...]
        for t, todo in enumerate(spread(nt)):
            for piece in todo:
                piece()
            res = jnp.dot(xs, wgu_scr[slot, :, 2 * t * LANES:(2 * t + 2) * LANES], preferred_element_type=F32)
            hid_scr[:, t * LANES:(t + 1) * LANES] = (jax.nn.silu(res[:, :LANES]) * res[:, LANES:]).astype(BF16)

    @pl.when((e1 > 0) & (j % 2 == 1))
    def _down():
        hid = hid_scr[...]
        for c, todo in enumerate(spread(nc)):
            for piece in todo:
                piece()
            cols = slice(c * FFN_NC, (c + 1) * FFN_NC)
            y = jnp.dot(hid, wd_scr[slot, :, cols], preferred_element_type=F32)
            y_ref[:, cols] = (y * gs_ref[...]).astype(BF16)


def _ffn(xs, gs, w_gate_e, w_up_e, w_down_e):
    ne, m, d = xs.shape
    ff = w_gate_e.shape[-1]
    assert m == FFN_TM * FFN_STEPS // 2 and d % FFN_STEPS == 0 and ff % (FFN_STEPS * BF16_SUBLANES) == 0
    assert ff % LANES == 0
    prev = lambda e1, j: (jnp.maximum(e1 - 1, 0), jnp.where(e1 == 0, 0, j // 2), 0)
    nxt = lambda e1, j: (jnp.minimum(e1, ne - 1), j, 0)
    return pl.pallas_call(
        _ffn_body,
        grid=(ne + 1, FFN_STEPS),
        in_specs=[
            pl.BlockSpec((None, FFN_TM, d), prev),
            pl.BlockSpec((None, FFN_TM, 1), prev),
            pl.BlockSpec((None, d // FFN_STEPS, ff), nxt),
            pl.BlockSpec((None, d // FFN_STEPS, ff), nxt),
            pl.BlockSpec((None, ff // FFN_STEPS, d), nxt),
        ],
        out_specs=pl.BlockSpec((None, FFN_TM, d), prev),
        out_shape=jax.ShapeDtypeStruct((ne, m, d), BF16),
        scratch_shapes=[pltpu.VMEM((2, d, 2 * ff), BF16), pltpu.VMEM((2, ff, d), BF16),
                        pltpu.VMEM((FFN_TM, ff), BF16)],
        compiler_params=_params(("arbitrary", "arbitrary")),
        name="ffn",
    )(xs, gs, w_gate_e, w_up_e, w_down_e)


TRIG_SPLIT = 32


def _trig_rows(n, rows, cols):
    ang = ((rows[:, None] * cols[None, :]) % n).astype(F32) * (2.0 * math.pi / n)
    return jnp.cos(ang), jnp.sin(ang)


def _dft_tables(n, rows, cols):
    nr = rows.shape[0]
    r1 = lax.iota(I32, nr // TRIG_SPLIT) * TRIG_SPLIT
    r0 = lax.iota(I32, TRIG_SPLIT)
    c1, s1 = _trig_rows(n, r1, cols)
    c0, s0 = _trig_rows(n, r0, cols)
    scale = 1.0 / math.sqrt(n)
    c1, s1 = (c1 * scale)[:, None, :], (s1 * scale)[:, None, :]
    c0, s0 = c0[None], s0[None]
    cos = (c1 * c0 - s1 * s0).reshape(nr, -1)
    sin = (s1 * c0 + c1 * s0).reshape(nr, -1)
    return cos, sin


def _pool_tables(s):
    i = lax.iota(I32, s)[:, None]
    k = (i // POOL_TM) * POOL_TM - POOL_HALO + lax.iota(I32, POOL_BAND)[None, :]
    out = []
    for half in POOL_HALF:
        lo = jnp.clip(i - half, 0, s)
        hi = jnp.clip(i + half, 0, s)
        inside = (k >= lo) & (k < hi)
        cnt = (hi - lo).astype(F32)
        out.append((jnp.where(inside, 1.0 / cnt, 0.0) - jnp.where(i == k, 1.0, 0.0)).astype(BF16))
    return jnp.stack(out)


def kernel(x, norm_mix_g, w_in, w_fourier_mix, w_pool_mix, pool_scale, w_branch_f, w_branch_p, w_gate,
           b_gate, w_out, norm_moe_g, w_router, w_expert_gate, w_expert_up, w_expert_down, norm_final_g):
    nb, s, d = x.shape
    assert w_in.shape[0] == 1, "single-layer block only"
    assert s % (2 * TOK_BLK) == 0 and max(POOL_HALF) <= POOL_HALO
    cap = CAPACITY_FACTOR * s // N_EXPERTS
    t = nb * s

    freqs = lax.iota(I32, s // DFT_RADIX)
    tabs = []
    for r in range(DFT_RADIX):
        cos_r, sin_r = _dft_tables(s, freqs, DFT_RADIX * freqs + r)
        tabs += [cos_r.astype(BF16), sin_r.astype(BF16)]
    trig = jnp.stack(tabs)
    chan = lax.iota(I32, GROUP)
    cos_c, sin_c = _trig_rows(GROUP, chan, chan)
    cos_c = (cos_c / math.sqrt(GROUP)).astype(BF16)
    sin_c = (sin_c / math.sqrt(GROUP)).astype(BF16)
    tri = (lax.iota(I32, s)[:, None] <= lax.iota(I32, s)[None, :]).astype(BF16)

    x2 = x.reshape(t, d)
    pf, pp, gates = _proj(x2, norm_mix_g[0][None], w_in[0].astype(BF16), w_gate[0].astype(BF16), b_gate[0][None])
    pc, ps = _fourier(trig, pf.reshape(nb, s // DFT_RADIX, -1), s)
    pooled = _pool(_pool_tables(s), pp.reshape(nb, s, -1))
    wr = w_router[0]
    wr_hi = wr.astype(BF16)
    wr_lo = (wr - wr_hi.astype(F32)).astype(BF16)
    h2, v, logits = _tail(
        pc.reshape(t, -1), ps.reshape(t, -1), pooled.reshape(t, -1), gates, x2, cos_c, sin_c,
        w_fourier_mix[0].astype(BF16), w_pool_mix[0].astype(BF16), pool_scale[0][None],
        w_branch_f[0].astype(BF16), w_branch_p[0].astype(BF16), w_out[0].astype(BF16),
        norm_moe_g[0][None], jnp.concatenate([wr_hi, wr_lo], axis=1))
    lg_t = jnp.swapaxes(logits.reshape(nb, s, N_EXPERTS), 1, 2)
    aff, sel, cum = _route(lg_t, tri, cap)
    base, npass = _window_plan(cum, cap)
    xs, gs = _gather(base, npass, sel, aff, v.reshape(nb, s, d), cap)
    ys = _ffn(xs, gs, w_expert_gate[0], w_expert_up[0], w_expert_down[0])
    return _combine(base, npass, jnp.swapaxes(sel, 1, 2), ys, h2.reshape(nb, s, d), norm_final_g[None], cap)
```

```python
import functools
import math

import jax
import jax.numpy as jnp
from jax import lax
from jax.experimental import pallas as pl
from jax.experimental.pallas import tpu as pltpu

F32 = jnp.float32
BF16 = jnp.bfloat16
I32 = jnp.int32

D_MODEL = 2048
FOURIER_WIDTH = 1024
N_GROUPS = 4
GROUP = 256
POOL_HALF = (1, 2, 4, 8)
N_EXPERTS = 16
EXPERT_FF = 1408
CAPACITY_FACTOR = 2
EPS = 1e-6

V7X_VMEM_LIMIT_BYTES = 58 * 1024 * 1024
BF16_SUBLANES = 16
LANES = 128


def _params(sem, vmem=V7X_VMEM_LIMIT_BYTES):
    return pltpu.CompilerParams(dimension_semantics=sem, vmem_limit_bytes=vmem)


def _const_spec(shape):
    nd = len(shape)
    return pl.BlockSpec(shape, lambda *_: (0,) * nd, pipeline_mode=pl.Buffered(1))


PROJ_TM = 512
PROJ_NC = 512
DFT_RADIX = 4


def _proj_body(x_ref, win_ref, wg_ref, bg_ref, pf_ref, pp_ref, gate_ref, split_ref):
    x = x_ref[...]
    xb = x.astype(BF16)
    inv = lax.rsqrt(jnp.mean(x * x, axis=-1, keepdims=True) + EPS)
    part = PROJ_TM // DFT_RADIX
    for j in range(FOURIER_WIDTH // PROJ_NC):
        res = jnp.dot(xb, win_ref[:, j * PROJ_NC:(j + 1) * PROJ_NC], preferred_element_type=F32) * inv
        for c in range(PROJ_NC // LANES):
            col = j * PROJ_NC + c * LANES
            split_ref[c] = res[:, c * LANES:(c + 1) * LANES]
            for r in range(DFT_RADIX):
                pf_ref[:, r * FOURIER_WIDTH + col:r * FOURIER_WIDTH + col + LANES] = (
                    split_ref[c, pl.ds(r, part, stride=DFT_RADIX), :].astype(BF16))
    for j in range((win_ref.shape[1] - FOURIER_WIDTH) // PROJ_NC):
        sl = slice(FOURIER_WIDTH + j * PROJ_NC, FOURIER_WIDTH + (j + 1) * PROJ_NC)
        res = jnp.dot(xb, win_ref[:, sl], preferred_element_type=F32) * inv
        pp_ref[:, j * PROJ_NC:(j + 1) * PROJ_NC] = res.astype(BF16)
    for j in range(wg_ref.shape[1] // PROJ_NC):
        sl = slice(j * PROJ_NC, (j + 1) * PROJ_NC)
        a = jnp.dot(xb, wg_ref[:, sl], preferred_element_type=F32) * inv + bg_ref[:, sl]
        gate_ref[:, sl] = jax.nn.sigmoid(a).astype(BF16)


def _proj(x2, w_in, w_gate, b_gate):
    t, d = x2.shape
    dm, dg = w_in.shape[1], w_gate.shape[1]
    return pl.pallas_call(
        _proj_body,
        grid=(t // PROJ_TM,),
        in_specs=[
            pl.BlockSpec((PROJ_TM, d), lambda i: (i, 0)),
            _const_spec((d, dm)),
            _const_spec((d, dg)),
            _const_spec((1, dg)),
        ],
        out_specs=[
            pl.BlockSpec((PROJ_TM // DFT_RADIX, DFT_RADIX * FOURIER_WIDTH), lambda i: (i, 0)),
            pl.BlockSpec((PROJ_TM, dm - FOURIER_WIDTH), lambda i: (i, 0)),
            pl.BlockSpec((PROJ_TM, dg), lambda i: (i, 0)),
        ],
        out_shape=[jax.ShapeDtypeStruct((t // DFT_RADIX, DFT_RADIX * FOURIER_WIDTH), BF16),
                   jax.ShapeDtypeStruct((t, dm - FOURIER_WIDTH), BF16),
                   jax.ShapeDtypeStruct((t, dg), BF16)],
        scratch_shapes=[pltpu.VMEM((PROJ_NC // LANES, PROJ_TM, LANES), F32)],
        compiler_params=_params(("parallel",)),
        name="proj",
    )(x2, w_in, w_gate, b_gate)


DFT_MC = 256
DFT_NC = 512


def _fourier_body(trig_ref, *refs):
    p_refs, (pc_ref, ps_ref) = refs[:DFT_RADIX], refs[DFT_RADIX:]
    part = trig_ref.shape[1]
    p_in = [r[...] for r in p_refs]
    for i in range(part // DFT_MC):
        rows = slice(i * DFT_MC, (i + 1) * DFT_MC)
        c = [jnp.dot(trig_ref[2 * r, rows, :], p_in[r], preferred_element_type=F32) for r in range(DFT_RADIX)]
        s = [jnp.dot(trig_ref[2 * r + 1, rows, :], p_in[r], preferred_element_type=F32) for r in range(DFT_RADIX)]
        ec, oc, es, os_ = c[0] + c[2], c[0] - c[2], s[0] + s[2], s[0] - s[2]
        fc, gc, fs, gs = c[1] + c[3], c[1] - c[3], s[1] + s[3], s[1] - s[3]
        xc = (ec + fc, oc - gs, ec - fc, oc + gs)
        xs = (es + fs, os_ + gc, es - fs, os_ - gc)
        for q in range(DFT_RADIX):
            out_rows = slice(q * part + i * DFT_MC, q * part + (i + 1) * DFT_MC)
            pc_ref[out_rows, :] = xc[q].astype(BF16)
            ps_ref[out_rows, :] = xs[q].astype(BF16)


def _fourier(trig, p_split, s):
    assert DFT_RADIX == 4
    b, part, _ = p_split.shape
    per_res = FOURIER_WIDTH // DFT_NC
    out = jax.ShapeDtypeStruct((b, s, FOURIER_WIDTH), BF16)
    res_spec = lambda r: pl.BlockSpec((None, part, DFT_NC), lambda i, g: (i, 0, r * per_res + g))
    return pl.pallas_call(
        _fourier_body,
        grid=(b, per_res),
        in_specs=[_const_spec(trig.shape)] + [res_spec(r) for r in range(DFT_RADIX)],
        out_specs=[pl.BlockSpec((None, s, DFT_NC), lambda i, g: (i, 0, g))] * 2,
        out_shape=[out, out],
        compiler_params=_params(("arbitrary", "arbitrary")),
        name="fourier",
    )(trig, *([p_split] * DFT_RADIX))


POOL_TM = 256
POOL_HALO = 128


POOL_BAND = POOL_TM + 2 * POOL_HALO


def _pool_body(a_ref, p_ref, o_ref):
    s = a_ref.shape[1]
    for g in range(N_GROUPS):
        cols = slice(g * GROUP, (g + 1) * GROUP)
        for i in range(s // POOL_TM):
            rows = slice(i * POOL_TM, (i + 1) * POOL_TM)
            lo, hi = i * POOL_TM - POOL_HALO, (i + 1) * POOL_TM + POOL_HALO
            band = slice(max(-lo, 0), POOL_BAND - max(hi - s, 0))
            src = slice(max(lo, 0), min(hi, s))
            o_ref[rows, cols] = jnp.dot(a_ref[g, rows, band], p_ref[src, cols],
                                        preferred_element_type=F32).astype(BF16)


def _pool(pool_bands, pp3):
    b, s, pw = pp3.shape
    return pl.pallas_call(
        _pool_body,
        grid=(b,),
        in_specs=[
            _const_spec(pool_bands.shape),
            pl.BlockSpec((None, s, pw), lambda i: (i, 0, 0)),
        ],
        out_specs=pl.BlockSpec((None, s, pw), lambda i: (i, 0, 0)),
        out_shape=jax.ShapeDtypeStruct((b, s, pw), BF16),
        compiler_params=_params(("arbitrary",)),
        name="pool",
    )(pool_bands, pp3)


TAIL_TM = 256


def _tail_body(pc_ref, ps_ref, pl_ref, gate_ref, x_ref, cc_ref, sc_ref, wf_ref, wp_ref, psc_ref, wbf_ref, wbp_ref,
               wout_ref, gm_ref, wr_ref, h_ref, v_ref, lg_ref):
    yf, yp = [], []
    for g in range(N_GROUPS):
        cols = slice(g * GROUP, (g + 1) * GROUP)
        z = (jnp.dot(pc_ref[:, cols], cc_ref[...], preferred_element_type=F32)
             - jnp.dot(ps_ref[:, cols], sc_ref[...], preferred_element_type=F32))
        yf.append(jnp.dot(z.astype(BF16), wf_ref[g], preferred_element_type=F32).astype(BF16))
        ypg = jnp.dot(pl_ref[:, cols], wp_ref[g], preferred_element_type=F32) * psc_ref[:, cols]
        yp.append(ypg.astype(BF16))
    yf = jnp.concatenate(yf, axis=-1)
    yp = jnp.concatenate(yp, axis=-1)
    d = wout_ref.shape[0]
    bf = jnp.dot(yf, wbf_ref[...], preferred_element_type=F32)
    bp = jnp.dot(yp, wbp_ref[...], preferred_element_type=F32)
    merged = gate_ref[:, :d].astype(F32) * bf + gate_ref[:, d:].astype(F32) * bp
    h = x_ref[...] + jnp.dot(merged.astype(BF16), wout_ref[...], preferred_element_type=F32)
    h_ref[...] = h
    inv = lax.rsqrt(jnp.mean(h * h, axis=-1, keepdims=True) + EPS)
    v_ref[...] = (h * inv * gm_ref[...]).astype(BF16)
    acc = jnp.dot(h.astype(BF16), wr_ref[...], preferred_element_type=F32)
    lg_ref[...] = (acc[:, :N_EXPERTS] + acc[:, N_EXPERTS:]) * inv


def _tail(pc2, ps2, pl2, gates, x2, cc, sc, wf, wp, pscale, wbf, wbp, wout, gm, wr2):
    t, d = x2.shape
    row = lambda width: pl.BlockSpec((TAIL_TM, width), lambda i: (i, 0))
    return pl.pallas_call(
        _tail_body,
        grid=(t // TAIL_TM,),
        in_specs=[
            row(pc2.shape[1]), row(ps2.shape[1]), row(pl2.shape[1]), row(gates.shape[1]), row(d),
            _const_spec(cc.shape), _const_spec(sc.shape), _const_spec(wf.shape), _const_spec(wp.shape),
            _const_spec(pscale.shape), _const_spec(wbf.shape), _const_spec(wbp.shape),
            _const_spec(wout.shape), _const_spec(gm.shape), _const_spec(wr2.shape),
        ],
        out_specs=[row(d), row(d), row(N_EXPERTS)],
        out_shape=[jax.ShapeDtypeStruct((t, d), F32), jax.ShapeDtypeStruct((t, d), BF16),
                   jax.ShapeDtypeStruct((t, N_EXPERTS), F32)],
        compiler_params=_params(("parallel",)),
        name="tail",
    )(pc2, ps2, pl2, gates, x2, cc, sc, wf, wp, pscale, wbf, wbp, wout, gm, wr2)


def _route_body(lg_ref, tri_ref, aff_ref, sel_ref, cum_ref, *, cap):
    nb, ne, s = lg_ref.shape
    lg = lg_ref[...]
    mx = jnp.max(lg, axis=1, keepdims=True)
    ex = jnp.exp(lg - mx)
    aff = ex / jnp.sum(ex, axis=1, keepdims=True)
    aff_ref[...] = aff
    aff2 = aff.reshape(nb * ne, s)

    def step(i, bits):
        cand = bits | jnp.left_shift(jnp.int32(1), 30 - i)
        n_ge = jnp.sum(jnp.where(aff2 >= pltpu.bitcast(cand, F32), 1.0, 0.0), axis=1, keepdims=True)
        return jnp.where(n_ge >= cap, cand, bits)

    thr = pltpu.bitcast(lax.fori_loop(0, 31, step, jnp.zeros((nb * ne, 1), I32)), F32)
    above = aff2 > thr
    tie = aff2 == thr
    n_above = jnp.sum(jnp.where(above, 1.0, 0.0), axis=1, keepdims=True)
    tri = tri_ref[...]
    tie_rank = jnp.dot(jnp.where(tie, 1.0, 0.0).astype(BF16), tri, preferred_element_type=F32)
    chosen = above | (tie & (tie_rank <= (cap - n_above)))
    cum = jnp.dot(jnp.where(chosen, 1.0, 0.0).astype(BF16), tri, preferred_element_type=F32)
    cum_ref[...] = cum.astype(I32).reshape(nb, ne, s)
    sel_ref[...] = jnp.where(chosen, cum - 1.0, -1.0).astype(I32).reshape(nb, ne, s)


def _route(lg_t, tri, cap):
    nb, ne, s = lg_t.shape
    ints = jax.ShapeDtypeStruct((nb, ne, s), I32)
    return pl.pallas_call(
        functools.partial(_route_body, cap=cap),
        out_shape=[jax.ShapeDtypeStruct((nb, ne, s), F32), ints, ints],
        compiler_params=pltpu.CompilerParams(vmem_limit_bytes=V7X_VMEM_LIMIT_BYTES),
        name="route",
    )(lg_t, tri)


TOK_BLK = 256
WIN = 64


def _window_plan(cum, cap):
    c_end = cum[:, :, TOK_BLK - 1::TOK_BLK]
    c_start = jnp.concatenate([jnp.zeros_like(c_end[:, :, :1]), c_end[:, :, :-1]], axis=-1)
    base = (c_start // BF16_SUBLANES) * BF16_SUBLANES
    passes = jnp.where(c_end > c_start, (c_end - base + WIN - 1) // WIN, 0)
    return jnp.swapaxes(base, 1, 2), jnp.max(passes, axis=1)


GATHER_DT = 1024


def _gather_body(base_s, npass_s, sel_ref, aff_ref, basec_ref, v_ref, xs_ref, gs_ref, *, cap):
    b = pl.program_id(0)
    ne, s = sel_ref.shape
    nk = s // TOK_BLK
    first_cols = pl.program_id(1) == 0
    xs_ref[...] = jnp.zeros(xs_ref.shape, xs_ref.dtype)

    @pl.when(first_cols)
    def _():
        gs_ref[...] = jnp.zeros(gs_ref.shape, gs_ref.dtype)

    wiota = lax.broadcasted_iota(I32, (WIN, TOK_BLK), 0)
    for k in range(nk):
        toks = slice(k * TOK_BLK, (k + 1) * TOK_BLK)
        selk = sel_ref[:, toks]
        basek = basec_ref[k]

        def one_pass(p, carry, k=k, toks=toks, selk=selk, basek=basek):
            first = basek + p * WIN
            wbase = jnp.minimum(first, cap - WIN)
            rel = jnp.where(selk >= first, selk - wbase, -1)
            hits = [rel[e:e + 1, :] == wiota for e in range(ne)]
            onehot = jnp.concatenate([jnp.where(h, 1.0, 0.0) for h in hits], axis=0).astype(BF16)
            rows = jnp.dot(onehot, v_ref[toks, :], preferred_element_type=F32)
            offs = [pl.multiple_of(jnp.minimum(base_s[(b * nk + k) * ne + e] + p * WIN, cap - WIN), BF16_SUBLANES)
                    for e in range(ne)]
            for e in range(ne):
                xs_ref[e, pl.ds(offs[e], WIN), :] += rows[e * WIN:(e + 1) * WIN].astype(BF16)

            @pl.when(first_cols)
            def _():
                for e in range(ne):
                    gate = jnp.sum(jnp.where(hits[e], aff_ref[e:e + 1, toks], 0.0), axis=1, keepdims=True)
                    gs_ref[e, pl.ds(offs[e], WIN), :] += gate
            return carry

        lax.fori_loop(0, npass_s[b * nk + k], one_pass, 0)


def _gather(base, npass, sel, aff, v3, cap):
    nb, ne, s = sel.shape
    d = v3.shape[-1]
    nk = s // TOK_BLK
    grid_spec = pltpu.PrefetchScalarGridSpec(
        num_scalar_prefetch=2,
        grid=(nb, d // GATHER_DT),
        in_specs=[
            pl.BlockSpec((None, ne, s), lambda b, j, *_: (b, 0, 0)),
            pl.BlockSpec((None, ne, s), lambda b, j, *_: (b, 0, 0)),
            pl.BlockSpec((None, nk, ne, 1), lambda b, j, *_: (b, 0, 0, 0)),
            pl.BlockSpec((None, s, GATHER_DT), lambda b, j, *_: (b, 0, j)),
        ],
        out_specs=[
            pl.BlockSpec((ne, cap, GATHER_DT), lambda b, j, *_: (0, b, j)),
            pl.BlockSpec((ne, cap, 1), lambda b, j, *_: (0, b, 0)),
        ],
    )
    return pl.pallas_call(
        functools.partial(_gather_body, cap=cap),
        grid_spec=grid_spec,
        out_shape=[jax.ShapeDtypeStruct((ne, nb * cap, d), BF16),
                   jax.ShapeDtypeStruct((ne, nb * cap, 1), F32)],
        compiler_params=_params(("arbitrary", "arbitrary")),
        name="gather",
    )(base.reshape(-1), npass.reshape(-1), sel, aff, base[..., None], v3)


def _combine_body(base_s, npass_s, selc_ref, baser_ref, y_ref, h_ref, g_ref, o_ref, acc_ref, *, cap):
    b = pl.program_id(0)
    k = pl.program_id(1)
    nk = pl.num_programs(1)
    ne = y_ref.shape[0]
    acc_ref[...] = h_ref[...]
    selk = selc_ref[...]
    basek = baser_ref[...]
    lane = lax.broadcasted_iota(I32, (ne, ne * WIN), 1)
    spread = jnp.where(lane // WIN == lax.broadcasted_iota(I32, (ne, ne * WIN), 0), 1.0, 0.0).astype(BF16)
    wlane = (lax.broadcasted_iota(I32, (1, ne * WIN), 1) % WIN).astype(F32)

    def one_pass(p, carry):
        first = basek + p * WIN
        wbase = jnp.minimum(first, cap - WIN)
        rel = jnp.where(selk >= first, selk - wbase, -1)
        relx = jnp.dot(rel.astype(F32).astype(BF16), spread, preferred_element_type=F32)
        onehot = jnp.where(relx == wlane, 1.0, 0.0).astype(BF16)
        wins = []
        for e in range(ne):
            off = jnp.minimum(base_s[(b * nk + k) * ne + e] + p * WIN, cap - WIN)
            wins.append(y_ref[e, pl.ds(pl.multiple_of(off, BF16_SUBLANES), WIN), :])
        acc_ref[...] += jnp.dot(onehot, jnp.concatenate(wins, axis=0), preferred_element_type=F32)
        return carry

    lax.fori_loop(0, npass_s[b * nk + k], one_pass, 0)
    h = acc_ref[...]
    ms = jnp.mean(h * h, axis=-1, keepdims=True)
    o_ref[...] = h * lax.rsqrt(ms + EPS) * g_ref[...]


def _combine(base, npass, sel_col, ys, h3, g_final, cap):
    nb, s, ne = sel_col.shape
    d = h3.shape[-1]
    nk = s // TOK_BLK
    grid_spec = pltpu.PrefetchScalarGridSpec(
        num_scalar_prefetch=2,
        grid=(nb, nk),
        in_specs=[
            pl.BlockSpec((None, TOK_BLK, ne), lambda b, k, *_: (b, k, 0)),
            pl.BlockSpec((None, None, 1, ne), lambda b, k, *_: (b, k, 0, 0)),
            pl.BlockSpec((ne, cap, d), lambda b, k, *_: (0, b, 0)),
            pl.BlockSpec((None, TOK_BLK, d), lambda b, k, *_: (b, k, 0)),
            pl.BlockSpec((1, d), lambda b, k, *_: (0, 0)),
        ],
        out_specs=pl.BlockSpec((None, TOK_BLK, d), lambda b, k, *_: (b, k, 0)),
        scratch_shapes=[pltpu.VMEM((TOK_BLK, d), F32)],
    )
    return pl.pallas_call(
        functools.partial(_combine_body, cap=cap),
        grid_spec=grid_spec,
        out_shape=jax.ShapeDtypeStruct((nb, s, d), F32),
        compiler_params=_params(("arbitrary", "arbitrary")),
        name="combine",
    )(base.reshape(-1), npass.reshape(-1), sel_col, base[:, :, None, :], ys, h3, g_final)


FFN_TM = 512
FFN_STEPS = 8
FFN_NC = 512


def _ffn_body(xs_ref, gs_ref, wg_ref, wu_ref, wd_ref, y_ref, wgu_scr, wd_scr, hid_scr):
    e1 = pl.program_id(0)
    j = pl.program_id(1)
    ne = pl.num_programs(0) - 1
    ff = wg_ref.shape[1]
    nt = ff // LANES

    nc = y_ref.shape[1] // FFN_NC
    stage_slot = e1 % 2
    slot = (e1 - 1) % 2
    kr, fr = wg_ref.shape[0], wd_ref.shape[0]
    r0 = pl.multiple_of(j * kr, kr)
    f0 = pl.multiple_of(j * fr, BF16_SUBLANES)

    def stage_gate_up(t):
        src = slice(t * LANES, (t + 1) * LANES)
        wgu_scr[stage_slot, pl.ds(r0, kr), 2 * t * LANES:(2 * t + 1) * LANES] = wg_ref[:, src].astype(BF16)
        wgu_scr[stage_slot, pl.ds(r0, kr), (2 * t + 1) * LANES:(2 * t + 2) * LANES] = wu_ref[:, src].astype(BF16)

    def stage_down(c):
        cols = slice(c * FFN_NC, (c + 1) * FFN_NC)
        wd_scr[stage_slot, pl.ds(f0, fr), cols] = wd_ref[:, cols].astype(BF16)

    pieces = [functools.partial(stage_gate_up, t) for t in range(nt)] + [
        functools.partial(stage_down, c) for c in range(nc)]

    def spread(n_chunks):
        return [pieces[i::n_chunks] for i in range(n_chunks)]

    @pl.when(e1 == 0)
    def _load_first_expert():
        for piece in pieces:
            piece()

    @pl.when((e1 > 0) & (j % 2 == 0))
    def _gate_up():
        xs = xs_ref[...]
        for t, todo in enumerate(spread(nt)):
            for piece in todo:
                piece()
            res = jnp.dot(xs, wgu_scr[slot, :, 2 * t * LANES:(2 * t + 2) * LANES], preferred_element_type=F32)
            hid_scr[:, t * LANES:(t + 1) * LANES] = (jax.nn.silu(res[:, :LANES]) * res[:, LANES:]).astype(BF16)

    @pl.when((e1 > 0) & (j % 2 == 1))
    def _down():
        hid = hid_scr[...]
        for c, todo in enumerate(spread(nc)):
            for piece in todo:
                piece()
            cols = slice(c * FFN_NC, (c + 1) * FFN_NC)
            y = jnp.dot(hid, wd_scr[slot, :, cols], preferred_element_type=F32)
            y_ref[:, cols] = (y * gs_ref[...]).astype(BF16)


def _ffn(xs, gs, w_gate_e, w_up_e, w_down_e):
    ne, m, d = xs.shape
    ff = w_gate_e.shape[-1]
    assert m == FFN_TM * FFN_STEPS // 2 and d % FFN_STEPS == 0 and ff % (FFN_STEPS * BF16_SUBLANES) == 0
    assert ff % LANES == 0
    prev = lambda e1, j: (jnp.maximum(e1 - 1, 0), jnp.where(e1 == 0, 0, j // 2), 0)
    nxt = lambda e1, j: (jnp.minimum(e1, ne - 1), j, 0)
    return pl.pallas_call(
        _ffn_body,
        grid=(ne + 1, FFN_STEPS),
        in_specs=[
            pl.BlockSpec((None, FFN_TM, d), prev),
            pl.BlockSpec((None, FFN_TM, 1), prev),
            pl.BlockSpec((None, d // FFN_STEPS, ff), nxt),
            pl.BlockSpec((None, d // FFN_STEPS, ff), nxt),
            pl.BlockSpec((None, ff // FFN_STEPS, d), nxt),
        ],
        out_specs=pl.BlockSpec((None, FFN_TM, d), prev),
        out_shape=jax.ShapeDtypeStruct((ne, m, d), BF16),
        scratch_shapes=[pltpu.VMEM((2, d, 2 * ff), BF16), pltpu.VMEM((2, ff, d), BF16),
                        pltpu.VMEM((FFN_TM, ff), BF16)],
        compiler_params=_params(("arbitrary", "arbitrary")),
        name="ffn",
    )(xs, gs, w_gate_e, w_up_e, w_down_e)


TRIG_SPLIT = 32


def _trig_rows(n, rows, cols):
    ang = ((rows[:, None] * cols[None, :]) % n).astype(F32) * (2.0 * math.pi / n)
    return jnp.cos(ang), jnp.sin(ang)


def _dft_tables(n, rows, cols):
    nr = rows.shape[0]
    r1 = lax.iota(I32, nr // TRIG_SPLIT) * TRIG_SPLIT
    r0 = lax.iota(I32, TRIG_SPLIT)
    c1, s1 = _trig_rows(n, r1, cols)
    c0, s0 = _trig_rows(n, r0, cols)
    scale = 1.0 / math.sqrt(n)
    c1, s1 = (c1 * scale)[:, None, :], (s1 * scale)[:, None, :]
    c0, s0 = c0[None], s0[None]
    cos = (c1 * c0 - s1 * s0).reshape(nr, -1)
    sin = (s1 * c0 + c1 * s0).reshape(nr, -1)
    return cos, sin


def _pool_tables(s):
    i = lax.iota(I32, s)[:, None]
    k = (i // POOL_TM) * POOL_TM - POOL_HALO + lax.iota(I32, POOL_BAND)[None, :]
    out = []
    for half in POOL_HALF:
        lo = jnp.clip(i - half, 0, s)
        hi = jnp.clip(i + half, 0, s)
        inside = (k >= lo) & (k < hi)
        cnt = (hi - lo).astype(F32)
        out.append((jnp.where(inside, 1.0 / cnt, 0.0) - jnp.where(i == k, 1.0, 0.0)).astype(BF16))
    return jnp.stack(out)


def kernel(x, norm_mix_g, w_in, w_fourier_mix, w_pool_mix, pool_scale, w_branch_f, w_branch_p, w_gate,
           b_gate, w_out, norm_moe_g, w_router, w_expert_gate, w_expert_up, w_expert_down, norm_final_g):
    nb, s, d = x.shape
    assert w_in.shape[0] == 1, "single-layer block only"
    assert s % (2 * TOK_BLK) == 0 and max(POOL_HALF) <= POOL_HALO
    cap = CAPACITY_FACTOR * s // N_EXPERTS
    t = nb * s

    freqs = lax.iota(I32, s // DFT_RADIX)
    tabs = []
    for r in range(DFT_RADIX):
        cos_r, sin_r = _dft_tables(s, freqs, DFT_RADIX * freqs + r)
        tabs += [cos_r.astype(BF16), sin_r.astype(BF16)]
    trig = jnp.stack(tabs)
    chan = lax.iota(I32, GROUP)
    cos_c, sin_c = _trig_rows(GROUP, chan, chan)
    cos_c = (cos_c / math.sqrt(GROUP)).astype(BF16)
    sin_c = (sin_c / math.sqrt(GROUP)).astype(BF16)
    tri = (lax.iota(I32, s)[:, None] <= lax.iota(I32, s)[None, :]).astype(BF16)

    x2 = x.reshape(t, d)
    g_mix = norm_mix_g[0][:, None]
    pf, pp, gates = _proj(x2, (g_mix * w_in[0]).astype(BF16), (g_mix * w_gate[0]).astype(BF16), b_gate[0][None])
    pc, ps = _fourier(trig, pf.reshape(nb, s // DFT_RADIX, -1), s)
    pooled = _pool(_pool_tables(s), pp.reshape(nb, s, -1))
    wr = norm_moe_g[0][:, None] * w_router[0]
    wr_hi = wr.astype(BF16)
    wr_lo = (wr - wr_hi.astype(F32)).astype(BF16)
    h2, v, logits = _tail(
        pc.reshape(t, -1), ps.reshape(t, -1), pooled.reshape(t, -1), gates, x2, cos_c, sin_c,
        w_fourier_mix[0].astype(BF16), w_pool_mix[0].astype(BF16), pool_scale[0][None],
        w_branch_f[0].astype(BF16), w_branch_p[0].astype(BF16), w_out[0].astype(BF16),
        norm_moe_g[0][None], jnp.concatenate([wr_hi, wr_lo], axis=1))
    lg_t = jnp.swapaxes(logits.reshape(nb, s, N_EXPERTS), 1, 2)
    aff, sel, cum = _route(lg_t, tri, cap)
    base, npass = _window_plan(cum, cap)
    xs, gs = _gather(base, npass, sel, aff, v.reshape(nb, s, d), cap)
    ys = _ffn(xs, gs, w_expert_gate[0], w_expert_up[0], w_expert_down[0])
    return _combine(base, npass, jnp.swapaxes(sel, 1, 2), ys, h2.reshape(nb, s, d), norm_final_g[None], cap)
```

```python
import functools
import math

import jax
import jax.numpy as jnp
from jax import lax
from jax.experimental import pallas as pl
from jax.experimental.pallas import tpu as pltpu

F32 = jnp.float32
BF16 = jnp.bfloat16
I32 = jnp.int32

D_MODEL = 2048
FOURIER_WIDTH = 1024
N_GROUPS = 4
GROUP = 256
POOL_HALF = (1, 2, 4, 8)
N_EXPERTS = 16
EXPERT_FF = 1408
CAPACITY_FACTOR = 2
EPS = 1e-6

V7X_VMEM_LIMIT_BYTES = 58 * 1024 * 1024
BF16_SUBLANES = 16
LANES = 128


def _params(sem, vmem=V7X_VMEM_LIMIT_BYTES):
    return pltpu.CompilerParams(dimension_semantics=sem, vmem_limit_bytes=vmem)


def _const_spec(shape):
    nd = len(shape)
    return pl.BlockSpec(shape, lambda *_: (0,) * nd, pipeline_mode=pl.Buffered(1))


PROJ_TM = 512
PROJ_NC = 512
DFT_RADIX = 4


def _proj_body(x_ref, win_ref, wg_ref, bg_ref, pf_ref, pp_ref, gate_ref, split_ref):
    x = x_ref[...]
    xb = x.astype(BF16)
    inv = lax.rsqrt(jnp.mean(x * x, axis=-1, keepdims=True) + EPS)
    part = PROJ_TM // DFT_RADIX
    for j in range(FOURIER_WIDTH // PROJ_NC):
        res = jnp.dot(xb, win_ref[:, j * PROJ_NC:(j + 1) * PROJ_NC], preferred_element_type=F32) * inv
        for c in range(PROJ_NC // LANES):
            col = j * PROJ_NC + c * LANES
            split_ref[c] = res[:, c * LANES:(c + 1) * LANES]
            for r in range(DFT_RADIX):
                pf_ref[:, r * FOURIER_WIDTH + col:r * FOURIER_WIDTH + col + LANES] = (
                    split_ref[c, pl.ds(r, part, stride=DFT_RADIX), :].astype(BF16))
    for j in range((win_ref.shape[1] - FOURIER_WIDTH) // PROJ_NC):
        sl = slice(FOURIER_WIDTH + j * PROJ_NC, FOURIER_WIDTH + (j + 1) * PROJ_NC)
        res = jnp.dot(xb, win_ref[:, sl], preferred_element_type=F32) * inv
        pp_ref[:, j * PROJ_NC:(j + 1) * PROJ_NC] = res.astype(BF16)
    for j in range(wg_ref.shape[1] // PROJ_NC):
        sl = slice(j * PROJ_NC, (j + 1) * PROJ_NC)
        a = jnp.dot(xb, wg_ref[:, sl], preferred_element_type=F32) * inv + bg_ref[:, sl]
        gate_ref[:, sl] = jax.nn.sigmoid(a).astype(BF16)


def _proj(x2, w_in, w_gate, b_gate):
    t, d = x2.shape
    dm, dg = w_in.shape[1], w_gate.shape[1]
    return pl.pallas_call(
        _proj_body,
        grid=(t // PROJ_TM,),
        in_specs=[
            pl.BlockSpec((PROJ_TM, d), lambda i: (i, 0)),
            _const_spec((d, dm)),
            _const_spec((d, dg)),
            _const_spec((1, dg)),
        ],
        out_specs=[
            pl.BlockSpec((PROJ_TM // DFT_RADIX, DFT_RADIX * FOURIER_WIDTH), lambda i: (i, 0)),
            pl.BlockSpec((PROJ_TM, dm - FOURIER_WIDTH), lambda i: (i, 0)),
            pl.BlockSpec((PROJ_TM, dg), lambda i: (i, 0)),
        ],
        out_shape=[jax.ShapeDtypeStruct((t // DFT_RADIX, DFT_RADIX * FOURIER_WIDTH), BF16),
                   jax.ShapeDtypeStruct((t, dm - FOURIER_WIDTH), BF16),
                   jax.ShapeDtypeStruct((t, dg), BF16)],
        scratch_shapes=[pltpu.VMEM((PROJ_NC // LANES, PROJ_TM, LANES), F32)],
        compiler_params=_params(("parallel",)),
        name="proj",
    )(x2, w_in, w_gate, b_gate)


DFT_MC = 256
DFT_NC = 512


def _fourier_body(trig_ref, *refs):
    p_refs, (pc_ref, ps_ref) = refs[:DFT_RADIX], refs[DFT_RADIX:]
    part = trig_ref.shape[1]
    p_in = [r[...] for r in p_refs]
    for i in range(part // DFT_MC):
        rows = slice(i * DFT_MC, (i + 1) * DFT_MC)
        c = [jnp.dot(trig_ref[2 * r, rows, :], p_in[r], preferred_element_type=F32) for r in range(DFT_RADIX)]
        s = [jnp.dot(trig_ref[2 * r + 1, rows, :], p_in[r], preferred_element_type=F32) for r in range(DFT_RADIX)]
        ec, oc, es, os_ = c[0] + c[2], c[0] - c[2], s[0] + s[2], s[0] - s[2]
        fc, gc, fs, gs = c[1] + c[3], c[1] - c[3], s[1] + s[3], s[1] - s[3]
        xc = (ec + fc, oc - gs, ec - fc, oc + gs)
        xs = (es + fs, os_ + gc, es - fs, os_ - gc)
        for q in range(DFT_RADIX):
            out_rows = slice(q * part + i * DFT_MC, q * part + (i + 1) * DFT_MC)
            pc_ref[out_rows, :] = xc[q].astype(BF16)
            ps_ref[out_rows, :] = xs[q].astype(BF16)


def _fourier(trig, p_split, s):
    assert DFT_RADIX == 4
    b, part, _ = p_split.shape
    per_res = FOURIER_WIDTH // DFT_NC
    out = jax.ShapeDtypeStruct((b, s, FOURIER_WIDTH), BF16)
    res_spec = lambda r: pl.BlockSpec((None, part, DFT_NC), lambda i, g: (i, 0, r * per_res + g))
    return pl.pallas_call(
        _fourier_body,
        grid=(b, per_res),
        in_specs=[_const_spec(trig.shape)] + [res_spec(r) for r in range(DFT_RADIX)],
        out_specs=[pl.BlockSpec((None, s, DFT_NC), lambda i, g: (i, 0, g))] * 2,
        out_shape=[out, out],
        compiler_params=_params(("arbitrary", "arbitrary")),
        name="fourier",
    )(trig, *([p_split] * DFT_RADIX))


POOL_TM = 256
POOL_HALO = 128


POOL_BAND = POOL_TM + 2 * POOL_HALO


def _pool_body(a_ref, p_ref, o_ref):
    s = a_ref.shape[1]
    for g in range(N_GROUPS):
        cols = slice(g * GROUP, (g + 1) * GROUP)
        for i in range(s // POOL_TM):
            rows = slice(i * POOL_TM, (i + 1) * POOL_TM)
            lo, hi = i * POOL_TM - POOL_HALO, (i + 1) * POOL_TM + POOL_HALO
            band = slice(max(-lo, 0), POOL_BAND - max(hi - s, 0))
            src = slice(max(lo, 0), min(hi, s))
            o_ref[rows, cols] = jnp.dot(a_ref[g, rows, band], p_ref[src, cols],
                                        preferred_element_type=F32).astype(BF16)


def _pool(pool_bands, pp3):
    b, s, pw = pp3.shape
    return pl.pallas_call(
        _pool_body,
        grid=(b,),
        in_specs=[
            _const_spec(pool_bands.shape),
            pl.BlockSpec((None, s, pw), lambda i: (i, 0, 0)),
        ],
        out_specs=pl.BlockSpec((None, s, pw), lambda i: (i, 0, 0)),
        out_shape=jax.ShapeDtypeStruct((b, s, pw), BF16),
        compiler_params=_params(("arbitrary",)),
        name="pool",
    )(pool_bands, pp3)


TAIL_TM = 256


def _tail_body(pc_ref, ps_ref, pl_ref, gate_ref, x_ref, cc_ref, sc_ref, wf_ref, wp_ref, psc_ref, wbf_ref, wbp_ref,
               wout_ref, gm_ref, wr_ref, h_ref, v_ref, lg_ref):
    yf, yp = [], []
    for g in range(N_GROUPS):
        cols = slice(g * GROUP, (g + 1) * GROUP)
        z = (jnp.dot(pc_ref[:, cols], cc_ref[...], preferred_element_type=F32)
             - jnp.dot(ps_ref[:, cols], sc_ref[...], preferred_element_type=F32))
        yf.append(jnp.dot(z.astype(BF16), wf_ref[g], preferred_element_type=F32).astype(BF16))
        ypg = jnp.dot(pl_ref[:, cols], wp_ref[g], preferred_element_type=F32) * psc_ref[:, cols]
        yp.append(ypg.astype(BF16))
    yf = jnp.concatenate(yf, axis=-1)
    yp = jnp.concatenate(yp, axis=-1)
    d = wout_ref.shape[0]
    bf = jnp.dot(yf, wbf_ref[...], preferred_element_type=F32)
    bp = jnp.dot(yp, wbp_ref[...], preferred_element_type=F32)
    merged = gate_ref[:, :d].astype(F32) * bf + gate_ref[:, d:].astype(F32) * bp
    h = x_ref[...] + jnp.dot(merged.astype(BF16), wout_ref[...], preferred_element_type=F32)
    h_ref[...] = h
    inv = lax.rsqrt(jnp.mean(h * h, axis=-1, keepdims=True) + EPS)
    v_ref[...] = (h * inv * gm_ref[...]).astype(BF16)
    acc = jnp.dot(h.astype(BF16), wr_ref[...], preferred_element_type=F32)
    lg_ref[...] = (acc[:, :N_EXPERTS] + acc[:, N_EXPERTS:]) * inv


def _tail(pc2, ps2, pl2, gates, x2, cc, sc, wf, wp, pscale, wbf, wbp, wout, gm, wr2):
    t, d = x2.shape
    row = lambda width: pl.BlockSpec((TAIL_TM, width), lambda i: (i, 0))
    return pl.pallas_call(
        _tail_body,
        grid=(t // TAIL_TM,),
        in_specs=[
            row(pc2.shape[1]), row(ps2.shape[1]), row(pl2.shape[1]), row(gates.shape[1]), row(d),
            _const_spec(cc.shape), _const_spec(sc.shape), _const_spec(wf.shape), _const_spec(wp.shape),
            _const_spec(pscale.shape), _const_spec(wbf.shape), _const_spec(wbp.shape),
            _const_spec(wout.shape), _const_spec(gm.shape), _const_spec(wr2.shape),
        ],
        out_specs=[row(d), row(d), row(N_EXPERTS)],
        out_shape=[jax.ShapeDtypeStruct((t, d), F32), jax.ShapeDtypeStruct((t, d), BF16),
                   jax.ShapeDtypeStruct((t, N_EXPERTS), F32)],
        compiler_params=_params(("parallel",)),
        name="tail",
    )(pc2, ps2, pl2, gates, x2, cc, sc, wf, wp, pscale, wbf, wbp, wout, gm, wr2)


def _route_body(lg_ref, tri_ref, aff_ref, sel_ref, cum_ref, *, cap):
    nb, ne, s = lg_ref.shape
    lg = lg_ref[...]
    mx = jnp.max(lg, axis=1, keepdims=True)
    ex = jnp.exp(lg - mx)
    aff = ex / jnp.sum(ex, axis=1, keepdims=True)
    aff_ref[...] = aff
    aff2 = aff.reshape(nb * ne, s)

    def step(i, bits):
        cand = bits | jnp.left_shift(jnp.int32(1), 30 - i)
        n_ge = jnp.sum(jnp.where(aff2 >= pltpu.bitcast(cand, F32), 1.0, 0.0), axis=1, keepdims=True)
        return jnp.where(n_ge >= cap, cand, bits)

    thr = pltpu.bitcast(lax.fori_loop(0, 31, step, jnp.zeros((nb * ne, 1), I32)), F32)
    above = aff2 > thr
    tie = aff2 == thr
    n_above = jnp.sum(jnp.where(above, 1.0, 0.0), axis=1, keepdims=True)
    tri = tri_ref[...]
    tie_rank = jnp.dot(jnp.where(tie, 1.0, 0.0).astype(BF16), tri, preferred_element_type=F32)
    chosen = above | (tie & (tie_rank <= (cap - n_above)))
    cum = jnp.dot(jnp.where(chosen, 1.0, 0.0).astype(BF16), tri, preferred_element_type=F32)
    cum_ref[...] = cum.astype(I32).reshape(nb, ne, s)
    sel_ref[...] = jnp.where(chosen, cum - 1.0, -1.0).astype(I32).reshape(nb, ne, s)


def _route(lg_t, tri, cap):
    nb, ne, s = lg_t.shape
    ints = jax.ShapeDtypeStruct((nb, ne, s), I32)
    return pl.pallas_call(
        functools.partial(_route_body, cap=cap),
        out_shape=[jax.ShapeDtypeStruct((nb, ne, s), F32), ints, ints],
        compiler_params=pltpu.CompilerParams(vmem_limit_bytes=V7X_VMEM_LIMIT_BYTES),
        name="route",
    )(lg_t, tri)


TOK_BLK = 256
WIN = 64


def _window_plan(cum, cap):
    c_end = cum[:, :, TOK_BLK - 1::TOK_BLK]
    c_start = jnp.concatenate([jnp.zeros_like(c_end[:, :, :1]), c_end[:, :, :-1]], axis=-1)
    base = (c_start // BF16_SUBLANES) * BF16_SUBLANES
    passes = jnp.where(c_end > c_start, (c_end - base + WIN - 1) // WIN, 0)
    return jnp.swapaxes(base, 1, 2), jnp.max(passes, axis=1)


GATHER_NE = 8


def _gather_body(base_s, npass_s, sel_ref, aff_ref, basec_ref, v_ref, xs_ref, gs_ref, *, cap, ne_total):
    b = pl.program_id(0)
    e0 = pl.program_id(1) * GATHER_NE
    ne, s = sel_ref.shape
    nk = s // TOK_BLK
    xs_ref[...] = jnp.zeros(xs_ref.shape, xs_ref.dtype)
    gs_ref[...] = jnp.zeros(gs_ref.shape, gs_ref.dtype)
    wiota = lax.broadcasted_iota(I32, (WIN, TOK_BLK), 0)
    for k in range(nk):
        toks = slice(k * TOK_BLK, (k + 1) * TOK_BLK)
        selk = sel_ref[:, toks]
        basek = basec_ref[k]

        def one_pass(p, carry, k=k, toks=toks, selk=selk, basek=basek):
            first = basek + p * WIN
            wbase = jnp.minimum(first, cap - WIN)
            rel = jnp.where(selk >= first, selk - wbase, -1)
            hits = [rel[e:e + 1, :] == wiota for e in range(ne)]
            onehot = jnp.concatenate([jnp.where(h, 1.0, 0.0) for h in hits], axis=0).astype(BF16)
            rows = jnp.dot(onehot, v_ref[toks, :], preferred_element_type=F32)
            for e in range(ne):
                off = jnp.minimum(base_s[(b * nk + k) * ne_total + e0 + e] + p * WIN, cap - WIN)
                off = pl.multiple_of(off, BF16_SUBLANES)
                xs_ref[e, pl.ds(off, WIN), :] += rows[e * WIN:(e + 1) * WIN].astype(BF16)
                gate = jnp.sum(jnp.where(hits[e], aff_ref[e:e + 1, toks], 0.0), axis=1, keepdims=True)
                gs_ref[e, pl.ds(off, WIN), :] += gate
            return carry

        lax.fori_loop(0, npass_s[b * nk + k], one_pass, 0)


def _gather(base, npass, sel, aff, v3, cap):
    nb, ne, s = sel.shape
    d = v3.shape[-1]
    nk = s // TOK_BLK
    grid_spec = pltpu.PrefetchScalarGridSpec(
        num_scalar_prefetch=2,
        grid=(nb, ne // GATHER_NE),
        in_specs=[
            pl.BlockSpec((None, GATHER_NE, s), lambda b, j, *_: (b, j, 0)),
            pl.BlockSpec((None, GATHER_NE, s), lambda b, j, *_: (b, j, 0)),
            pl.BlockSpec((None, nk, GATHER_NE, 1), lambda b, j, *_: (b, 0, j, 0)),
            pl.BlockSpec((None, s, d), lambda b, j, *_: (b, 0, 0)),
        ],
        out_specs=[
            pl.BlockSpec((GATHER_NE, cap, d), lambda b, j, *_: (j, b, 0)),
            pl.BlockSpec((GATHER_NE, cap, 1), lambda b, j, *_: (j, b, 0)),
        ],
    )
    return pl.pallas_call(
        functools.partial(_gather_body, cap=cap, ne_total=ne),
        grid_spec=grid_spec,
        out_shape=[jax.ShapeDtypeStruct((ne, nb * cap, d), BF16),
                   jax.ShapeDtypeStruct((ne, nb * cap, 1), F32)],
        compiler_params=_params(("arbitrary", "arbitrary")),
        name="gather",
    )(base.reshape(-1), npass.reshape(-1), sel, aff, base[..., None], v3)


def _combine_body(base_s, npass_s, selc_ref, baser_ref, y_ref, h_ref, g_ref, o_ref, acc_ref, *, cap):
    b = pl.program_id(0)
    k = pl.program_id(1)
    nk = pl.num_programs(1)
    ne = y_ref.shape[0]
    acc_ref[...] = h_ref[...]
    selk = selc_ref[...]
    basek = baser_ref[...]
    lane = lax.broadcasted_iota(I32, (ne, ne * WIN), 1)
    spread = jnp.where(lane // WIN == lax.broadcasted_iota(I32, (ne, ne * WIN), 0), 1.0, 0.0).astype(BF16)
    wlane = (lax.broadcasted_iota(I32, (1, ne * WIN), 1) % WIN).astype(F32)

    def one_pass(p, carry):
        first = basek + p * WIN
        wbase = jnp.minimum(first, cap - WIN)
        rel = jnp.where(selk >= first, selk - wbase, -1)
        relx = jnp.dot(rel.astype(F32).astype(BF16), spread, preferred_element_type=F32)
        onehot = jnp.where(relx == wlane, 1.0, 0.0).astype(BF16)
        wins = []
        for e in range(ne):
            off = jnp.minimum(base_s[(b * nk + k) * ne + e] + p * WIN, cap - WIN)
            wins.append(y_ref[e, pl.ds(pl.multiple_of(off, BF16_SUBLANES), WIN), :])
        acc_ref[...] += jnp.dot(onehot, jnp.concatenate(wins, axis=0), preferred_element_type=F32)
        return carry

    lax.fori_loop(0, npass_s[b * nk + k], one_pass, 0)
    h = acc_ref[...]
    ms = jnp.mean(h * h, axis=-1, keepdims=True)
    o_ref[...] = h * lax.rsqrt(ms + EPS) * g_ref[...]


def _combine(base, npass, sel_col, ys, h3, g_final, cap):
    nb, s, ne = sel_col.shape
    d = h3.shape[-1]
    nk = s // TOK_BLK
    grid_spec = pltpu.PrefetchScalarGridSpec(
        num_scalar_prefetch=2,
        grid=(nb, nk),
        in_specs=[
            pl.BlockSpec((None, TOK_BLK, ne), lambda b, k, *_: (b, k, 0)),
            pl.BlockSpec((None, None, 1, ne), lambda b, k, *_: (b, k, 0, 0)),
            pl.BlockSpec((ne, cap, d), lambda b, k, *_: (0, b, 0)),
            pl.BlockSpec((None, TOK_BLK, d), lambda b, k, *_: (b, k, 0)),
            pl.BlockSpec((1, d), lambda b, k, *_: (0, 0)),
        ],
        out_specs=pl.BlockSpec((None, TOK_BLK, d), lambda b, k, *_: (b, k, 0)),
        scratch_shapes=[pltpu.VMEM((TOK_BLK, d), F32)],
    )
    return pl.pallas_call(
        functools.partial(_combine_body, cap=cap),
        grid_spec=grid_spec,
        out_shape=jax.ShapeDtypeStruct((nb, s, d), F32),
        compiler_params=_params(("arbitrary", "arbitrary")),
        name="combine",
    )(base.reshape(-1), npass.reshape(-1), sel_col, base[:, :, None, :], ys, h3, g_final)


FFN_TM = 512
FFN_STEPS = 8
FFN_NC = 512


def _ffn_body(xs_ref, gs_ref, wg_ref, wu_ref, wd_ref, y_ref, wgu_scr, wd_scr, hid_scr):
    e1 = pl.program_id(0)
    j = pl.program_id(1)
    ne = pl.num_programs(0) - 1
    ff = wg_ref.shape[1]
    nt = ff // LANES

    nc = y_ref.shape[1] // FFN_NC
    stage_slot = e1 % 2
    slot = (e1 - 1) % 2
    kr, fr = wg_ref.shape[0], wd_ref.shape[0]
    r0 = pl.multiple_of(j * kr, kr)
    f0 = pl.multiple_of(j * fr, BF16_SUBLANES)

    def stage_gate_up(t):
        src = slice(t * LANES, (t + 1) * LANES)
        wgu_scr[stage_slot, pl.ds(r0, kr), 2 * t * LANES:(2 * t + 1) * LANES] = wg_ref[:, src].astype(BF16)
        wgu_scr[stage_slot, pl.ds(r0, kr), (2 * t + 1) * LANES:(2 * t + 2) * LANES] = wu_ref[:, src].astype(BF16)

    def stage_down(c):
        cols = slice(c * FFN_NC, (c + 1) * FFN_NC)
        wd_scr[stage_slot, pl.ds(f0, fr), cols] = wd_ref[:, cols].astype(BF16)

    pieces = [functools.partial(stage_gate_up, t) for t in range(nt)] + [
        functools.partial(stage_down, c) for c in range(nc)]

    def spread(n_chunks):
        return [pieces[i::n_chunks] for i in range(n_chunks)]

    @pl.when(e1 == 0)
    def _load_first_expert():
        for piece in pieces:
            piece()

    @pl.when((e1 > 0) & (j % 2 == 0))
    def _gate_up():
        xs = xs_ref[...]
        for t, todo in enumerate(spread(nt)):
            for piece in todo:
                piece()
            res = jnp.dot(xs, wgu_scr[slot, :, 2 * t * LANES:(2 * t + 2) * LANES], preferred_element_type=F32)
            hid_scr[:, t * LANES:(t + 1) * LANES] = (jax.nn.silu(res[:, :LANES]) * res[:, LANES:]).astype(BF16)

    @pl.when((e1 > 0) & (j % 2 == 1))
    def _down():
        hid = hid_scr[...]
        for c, todo in enumerate(spread(nc)):
            for piece in todo:
                piece()
            cols = slice(c * FFN_NC, (c + 1) * FFN_NC)
            y = jnp.dot(hid, wd_scr[slot, :, cols], preferred_element_type=F32)
            y_ref[:, cols] = (y * gs_ref[...]).astype(BF16)


def _ffn(xs, gs, w_gate_e, w_up_e, w_down_e):
    ne, m, d = xs.shape
    ff = w_gate_e.shape[-1]
    assert m == FFN_TM * FFN_STEPS // 2 and d % FFN_STEPS == 0 and ff % (FFN_STEPS * BF16_SUBLANES) == 0
    assert ff % LANES == 0
    prev = lambda e1, j: (jnp.maximum(e1 - 1, 0), jnp.where(e1 == 0, 0, j // 2), 0)
    nxt = lambda e1, j: (jnp.minimum(e1, ne - 1), j, 0)
    return pl.pallas_call(
        _ffn_body,
        grid=(ne + 1, FFN_STEPS),
        in_specs=[
            pl.BlockSpec((None, FFN_TM, d), prev),
            pl.BlockSpec((None, FFN_TM, 1), prev),
            pl.BlockSpec((None, d // FFN_STEPS, ff), nxt),
            pl.BlockSpec((None, d // FFN_STEPS, ff), nxt),
            pl.BlockSpec((None, ff // FFN_STEPS, d), nxt),
        ],
        out_specs=pl.BlockSpec((None, FFN_TM, d), prev),
        out_shape=jax.ShapeDtypeStruct((ne, m, d), BF16),
        scratch_shapes=[pltpu.VMEM((2, d, 2 * ff), BF16), pltpu.VMEM((2, ff, d), BF16),
                        pltpu.VMEM((FFN_TM, ff), BF16)],
        compiler_params=_params(("arbitrary", "arbitrary")),
        name="ffn",
    )(xs, gs, w_gate_e, w_up_e, w_down_e)


TRIG_SPLIT = 32


def _trig_rows(n, rows, cols):
    ang = ((rows[:, None] * cols[None, :]) % n).astype(F32) * (2.0 * math.pi / n)
    return jnp.cos(ang), jnp.sin(ang)


def _dft_tables(n, rows, cols):
    nr = rows.shape[0]
    r1 = lax.iota(I32, nr // TRIG_SPLIT) * TRIG_SPLIT
    r0 = lax.iota(I32, TRIG_SPLIT)
    c1, s1 = _trig_rows(n, r1, cols)
    c0, s0 = _trig_rows(n, r0, cols)
    scale = 1.0 / math.sqrt(n)
    c1, s1 = (c1 * scale)[:, None, :], (s1 * scale)[:, None, :]
    c0, s0 = c0[None], s0[None]
    cos = (c1 * c0 - s1 * s0).reshape(nr, -1)
    sin = (s1 * c0 + c1 * s0).reshape(nr, -1)
    return cos, sin


def _pool_tables(s):
    i = lax.iota(I32, s)[:, None]
    k = (i // POOL_TM) * POOL_TM - POOL_HALO + lax.iota(I32, POOL_BAND)[None, :]
    out = []
    for half in POOL_HALF:
        lo = jnp.clip(i - half, 0, s)
        hi = jnp.clip(i + half, 0, s)
        inside = (k >= lo) & (k < hi)
        cnt = (hi - lo).astype(F32)
        out.append((jnp.where(inside, 1.0 / cnt, 0.0) - jnp.where(i == k, 1.0, 0.0)).astype(BF16))
    return jnp.stack(out)


def kernel(x, norm_mix_g, w_in, w_fourier_mix, w_pool_mix, pool_scale, w_branch_f, w_branch_p, w_gate,
           b_gate, w_out, norm_moe_g, w_router, w_expert_gate, w_expert_up, w_expert_down, norm_final_g):
    nb, s, d = x.shape
    assert w_in.shape[0] == 1, "single-layer block only"
    assert s % (2 * TOK_BLK) == 0 and max(POOL_HALF) <= POOL_HALO
    cap = CAPACITY_FACTOR * s // N_EXPERTS
    t = nb * s

    freqs = lax.iota(I32, s // DFT_RADIX)
    tabs = []
    for r in range(DFT_RADIX):
        cos_r, sin_r = _dft_tables(s, freqs, DFT_RADIX * freqs + r)
        tabs += [cos_r.astype(BF16), sin_r.astype(BF16)]
    trig = jnp.stack(tabs)
    chan = lax.iota(I32, GROUP)
    cos_c, sin_c = _trig_rows(GROUP, chan, chan)
    cos_c = (cos_c / math.sqrt(GROUP)).astype(BF16)
    sin_c = (sin_c / math.sqrt(GROUP)).astype(BF16)
    tri = (lax.iota(I32, s)[:, None] <= lax.iota(I32, s)[None, :]).astype(BF16)

    x2 = x.reshape(t, d)
    g_mix = norm_mix_g[0][:, None]
    pf, pp, gates = _proj(x2, (g_mix * w_in[0]).astype(BF16), (g_mix * w_gate[0]).astype(BF16), b_gate[0][None])
    pc, ps = _fourier(trig, pf.reshape(nb, s // DFT_RADIX, -1), s)
    pooled = _pool(_pool_tables(s), pp.reshape(nb, s, -1))
    wr = norm_moe_g[0][:, None] * w_router[0]
    wr_hi = wr.astype(BF16)
    wr_lo = (wr - wr_hi.astype(F32)).astype(BF16)
    h2, v, logits = _tail(
        pc.reshape(t, -1), ps.reshape(t, -1), pooled.reshape(t, -1), gates, x2, cos_c, sin_c,
        w_fourier_mix[0].astype(BF16), w_pool_mix[0].astype(BF16), pool_scale[0][None],
        w_branch_f[0].astype(BF16), w_branch_p[0].astype(BF16), w_out[0].astype(BF16),
        norm_moe_g[0][None], jnp.concatenate([wr_hi, wr_lo], axis=1))
    lg_t = jnp.swapaxes(logits.reshape(nb, s, N_EXPERTS), 1, 2)
    aff, sel, cum = _route(lg_t, tri, cap)
    base, npass = _window_plan(cum, cap)
    xs, gs = _gather(base, npass, sel, aff, v.reshape(nb, s, d), cap)
    ys = _ffn(xs, gs, w_expert_gate[0], w_expert_up[0], w_expert_down[0])
    return _combine(base, npass, jnp.swapaxes(sel, 1, 2), ys, h2.reshape(nb, s, d), norm_final_g[None], cap)
```

```python
import functools
import math

import jax
import jax.numpy as jnp
from jax import lax
from jax.experimental import pallas as pl
from jax.experimental.pallas import tpu as pltpu

F32 = jnp.float32
BF16 = jnp.bfloat16
I32 = jnp.int32

D_MODEL = 2048
FOURIER_WIDTH = 1024
N_GROUPS = 4
GROUP = 256
POOL_HALF = (1, 2, 4, 8)
N_EXPERTS = 16
EXPERT_FF = 1408
CAPACITY_FACTOR = 2
EPS = 1e-6

V7X_VMEM_LIMIT_BYTES = 58 * 1024 * 1024
BF16_SUBLANES = 16
LANES = 128


def _params(sem, vmem=V7X_VMEM_LIMIT_BYTES):
    return pltpu.CompilerParams(dimension_semantics=sem, vmem_limit_bytes=vmem)


def _const_spec(shape):
    nd = len(shape)
    return pl.BlockSpec(shape, lambda *_: (0,) * nd, pipeline_mode=pl.Buffered(1))


PROJ_TM = 512
PROJ_NC = 512
DFT_RADIX = 4


def _proj_body(x_ref, win_ref, wg_ref, bg_ref, pf_ref, pp_ref, gate_ref, split_ref):
    x = x_ref[...]
    xb = x.astype(BF16)
    inv = lax.rsqrt(jnp.mean(x * x, axis=-1, keepdims=True) + EPS)
    part = PROJ_TM // DFT_RADIX
    for j in range(FOURIER_WIDTH // PROJ_NC):
        res = jnp.dot(xb, win_ref[:, j * PROJ_NC:(j + 1) * PROJ_NC], preferred_element_type=F32) * inv
        for c in range(PROJ_NC // LANES):
            col = j * PROJ_NC + c * LANES
            split_ref[c] = res[:, c * LANES:(c + 1) * LANES]
            for r in range(DFT_RADIX):
                pf_ref[:, r * FOURIER_WIDTH + col:r * FOURIER_WIDTH + col + LANES] = (
                    split_ref[c, pl.ds(r, part, stride=DFT_RADIX), :].astype(BF16))
    for j in range((win_ref.shape[1] - FOURIER_WIDTH) // PROJ_NC):
        sl = slice(FOURIER_WIDTH + j * PROJ_NC, FOURIER_WIDTH + (j + 1) * PROJ_NC)
        res = jnp.dot(xb, win_ref[:, sl], preferred_element_type=F32) * inv
        pp_ref[:, j * PROJ_NC:(j + 1) * PROJ_NC] = res.astype(BF16)
    for j in range(wg_ref.shape[1] // PROJ_NC):
        sl = slice(j * PROJ_NC, (j + 1) * PROJ_NC)
        a = jnp.dot(xb, wg_ref[:, sl], preferred_element_type=F32) * inv + bg_ref[:, sl]
        gate_ref[:, sl] = jax.nn.sigmoid(a).astype(BF16)


def _proj(x2, w_in, w_gate, b_gate):
    t, d = x2.shape
    dm, dg = w_in.shape[1], w_gate.shape[1]
    return pl.pallas_call(
        _proj_body,
        grid=(t // PROJ_TM,),
        in_specs=[
            pl.BlockSpec((PROJ_TM, d), lambda i: (i, 0)),
            _const_spec((d, dm)),
            _const_spec((d, dg)),
            _const_spec((1, dg)),
        ],
        out_specs=[
            pl.BlockSpec((PROJ_TM // DFT_RADIX, DFT_RADIX * FOURIER_WIDTH), lambda i: (i, 0)),
            pl.BlockSpec((PROJ_TM, dm - FOURIER_WIDTH), lambda i: (i, 0)),
            pl.BlockSpec((PROJ_TM, dg), lambda i: (i, 0)),
        ],
        out_shape=[jax.ShapeDtypeStruct((t // DFT_RADIX, DFT_RADIX * FOURIER_WIDTH), BF16),
                   jax.ShapeDtypeStruct((t, dm - FOURIER_WIDTH), BF16),
                   jax.ShapeDtypeStruct((t, dg), BF16)],
        scratch_shapes=[pltpu.VMEM((PROJ_NC // LANES, PROJ_TM, LANES), F32)],
        compiler_params=_params(("parallel",)),
        name="proj",
    )(x2, w_in, w_gate, b_gate)


DFT_MC = 256
DFT_NC = 512


def _fourier_body(trig_ref, *refs):
    p_refs, (pc_ref, ps_ref) = refs[:DFT_RADIX], refs[DFT_RADIX:]
    part = trig_ref.shape[1]
    p_in = [r[...] for r in p_refs]
    for i in range(part // DFT_MC):
        rows = slice(i * DFT_MC, (i + 1) * DFT_MC)
        c = [jnp.dot(trig_ref[2 * r, rows, :], p_in[r], preferred_element_type=F32) for r in range(DFT_RADIX)]
        s = [jnp.dot(trig_ref[2 * r + 1, rows, :], p_in[r], preferred_element_type=F32) for r in range(DFT_RADIX)]
        ec, oc, es, os_ = c[0] + c[2], c[0] - c[2], s[0] + s[2], s[0] - s[2]
        fc, gc, fs, gs = c[1] + c[3], c[1] - c[3], s[1] + s[3], s[1] - s[3]
        xc = (ec + fc, oc - gs, ec - fc, oc + gs)
        xs = (es + fs, os_ + gc, es - fs, os_ - gc)
        for q in range(DFT_RADIX):
            out_rows = slice(q * part + i * DFT_MC, q * part + (i + 1) * DFT_MC)
            pc_ref[out_rows, :] = xc[q].astype(BF16)
            ps_ref[out_rows, :] = xs[q].astype(BF16)


def _fourier(trig, p_split, s):
    assert DFT_RADIX == 4
    b, part, _ = p_split.shape
    per_res = FOURIER_WIDTH // DFT_NC
    out = jax.ShapeDtypeStruct((b, s, FOURIER_WIDTH), BF16)
    res_spec = lambda r: pl.BlockSpec((None, part, DFT_NC), lambda i, g: (i, 0, r * per_res + g))
    return pl.pallas_call(
        _fourier_body,
        grid=(b, per_res),
        in_specs=[_const_spec(trig.shape)] + [res_spec(r) for r in range(DFT_RADIX)],
        out_specs=[pl.BlockSpec((None, s, DFT_NC), lambda i, g: (i, 0, g))] * 2,
        out_shape=[out, out],
        compiler_params=_params(("arbitrary", "arbitrary")),
        name="fourier",
    )(trig, *([p_split] * DFT_RADIX))


POOL_TM = 256
POOL_HALO = 128


POOL_BAND = POOL_TM + 2 * POOL_HALO


def _pool_body(a_ref, p_ref, o_ref):
    s = a_ref.shape[1]
    for g in range(N_GROUPS):
        cols = slice(g * GROUP, (g + 1) * GROUP)
        for i in range(s // POOL_TM):
            rows = slice(i * POOL_TM, (i + 1) * POOL_TM)
            lo, hi = i * POOL_TM - POOL_HALO, (i + 1) * POOL_TM + POOL_HALO
            band = slice(max(-lo, 0), POOL_BAND - max(hi - s, 0))
            src = slice(max(lo, 0), min(hi, s))
            o_ref[rows, cols] = jnp.dot(a_ref[g, rows, band], p_ref[src, cols],
                                        preferred_element_type=F32).astype(BF16)


def _pool(pool_bands, pp3):
    b, s, pw = pp3.shape
    return pl.pallas_call(
        _pool_body,
        grid=(b,),
        in_specs=[
            _const_spec(pool_bands.shape),
            pl.BlockSpec((None, s, pw), lambda i: (i, 0, 0)),
        ],
        out_specs=pl.BlockSpec((None, s, pw), lambda i: (i, 0, 0)),
        out_shape=jax.ShapeDtypeStruct((b, s, pw), BF16),
        compiler_params=_params(("arbitrary",)),
        name="pool",
    )(pool_bands, pp3)


TAIL_TM = 256


def _tail_body(pc_ref, ps_ref, pl_ref, gate_ref, x_ref, cc_ref, sc_ref, wf_ref, wp_ref, psc_ref, wbf_ref, wbp_ref,
               wout_ref, gm_ref, wr_ref, h_ref, v_ref, lg_ref):
    yf, yp = [], []
    for g in range(N_GROUPS):
        cols = slice(g * GROUP, (g + 1) * GROUP)
        z = (jnp.dot(pc_ref[:, cols], cc_ref[...], preferred_element_type=F32)
             - jnp.dot(ps_ref[:, cols], sc_ref[...], preferred_element_type=F32))
        yf.append(jnp.dot(z.astype(BF16), wf_ref[g], preferred_element_type=F32).astype(BF16))
        ypg = jnp.dot(pl_ref[:, cols], wp_ref[g], preferred_element_type=F32) * psc_ref[:, cols]
        yp.append(ypg.astype(BF16))
    yf = jnp.concatenate(yf, axis=-1)
    yp = jnp.concatenate(yp, axis=-1)
    d = wout_ref.shape[0]
    bf = jnp.dot(yf, wbf_ref[...], preferred_element_type=F32)
    bp = jnp.dot(yp, wbp_ref[...], preferred_element_type=F32)
    merged = gate_ref[:, :d].astype(F32) * bf + gate_ref[:, d:].astype(F32) * bp
    h = x_ref[...] + jnp.dot(merged.astype(BF16), wout_ref[...], preferred_element_type=F32)
    h_ref[...] = h
    inv = lax.rsqrt(jnp.mean(h * h, axis=-1, keepdims=True) + EPS)
    v_ref[...] = (h * inv * gm_ref[...]).astype(BF16)
    acc = jnp.dot(h.astype(BF16), wr_ref[...], preferred_element_type=F32)
    lg_ref[...] = (acc[:, :N_EXPERTS] + acc[:, N_EXPERTS:]) * inv


def _tail(pc2, ps2, pl2, gates, x2, cc, sc, wf, wp, pscale, wbf, wbp, wout, gm, wr2):
    t, d = x2.shape
    row = lambda width: pl.BlockSpec((TAIL_TM, width), lambda i: (i, 0))
    return pl.pallas_call(
        _tail_body,
        grid=(t // TAIL_TM,),
        in_specs=[
            row(pc2.shape[1]), row(ps2.shape[1]), row(pl2.shape[1]), row(gates.shape[1]), row(d),
            _const_spec(cc.shape), _const_spec(sc.shape), _const_spec(wf.shape), _const_spec(wp.shape),
            _const_spec(pscale.shape), _const_spec(wbf.shape), _const_spec(wbp.shape),
            _const_spec(wout.shape), _const_spec(gm.shape), _const_spec(wr2.shape),
        ],
        out_specs=[row(d), row(d), row(N_EXPERTS)],
        out_shape=[jax.ShapeDtypeStruct((t, d), F32), jax.ShapeDtypeStruct((t, d), BF16),
                   jax.ShapeDtypeStruct((t, N_EXPERTS), F32)],
        compiler_params=_params(("parallel",)),
        name="tail",
    )(pc2, ps2, pl2, gates, x2, cc, sc, wf, wp, pscale, wbf, wbp, wout, gm, wr2)


def _route_body(lg_ref, tri_ref, aff_ref, sel_ref, cum_ref, *, cap):
    nb, ne, s = lg_ref.shape
    lg = lg_ref[...]
    mx = jnp.max(lg, axis=1, keepdims=True)
    ex = jnp.exp(lg - mx)
    aff = ex / jnp.sum(ex, axis=1, keepdims=True)
    aff_ref[...] = aff
    aff2 = aff.reshape(nb * ne, s)

    def step(i, bits):
        cand = bits | jnp.left_shift(jnp.int32(1), 30 - i)
        n_ge = jnp.sum(jnp.where(aff2 >= pltpu.bitcast(cand, F32), 1.0, 0.0), axis=1, keepdims=True)
        return jnp.where(n_ge >= cap, cand, bits)

    thr = pltpu.bitcast(lax.fori_loop(0, 31, step, jnp.zeros((nb * ne, 1), I32)), F32)
    above = aff2 > thr
    tie = aff2 == thr
    n_above = jnp.sum(jnp.where(above, 1.0, 0.0), axis=1, keepdims=True)
    tri = tri_ref[...]
    tie_rank = jnp.dot(jnp.where(tie, 1.0, 0.0).astype(BF16), tri, preferred_element_type=F32)
    chosen = above | (tie & (tie_rank <= (cap - n_above)))
    cum = jnp.dot(jnp.where(chosen, 1.0, 0.0).astype(BF16), tri, preferred_element_type=F32)
    cum_ref[...] = cum.astype(I32).reshape(nb, ne, s)
    sel_ref[...] = jnp.where(chosen, cum - 1.0, -1.0).astype(I32).reshape(nb, ne, s)


def _route(lg_t, tri, cap):
    nb, ne, s = lg_t.shape
    ints = jax.ShapeDtypeStruct((nb, ne, s), I32)
    return pl.pallas_call(
        functools.partial(_route_body, cap=cap),
        out_shape=[jax.ShapeDtypeStruct((nb, ne, s), F32), ints, ints],
        compiler_params=pltpu.CompilerParams(vmem_limit_bytes=V7X_VMEM_LIMIT_BYTES),
        name="route",
    )(lg_t, tri)


TOK_BLK = 256
WIN = 64


def _window_plan(cum, cap):
    c_end = cum[:, :, TOK_BLK - 1::TOK_BLK]
    c_start = jnp.concatenate([jnp.zeros_like(c_end[:, :, :1]), c_end[:, :, :-1]], axis=-1)
    base = (c_start // BF16_SUBLANES) * BF16_SUBLANES
    passes = jnp.where(c_end > c_start, (c_end - base + WIN - 1) // WIN, 0)
    return jnp.swapaxes(base, 1, 2), jnp.max(passes, axis=1)


GATHER_DT = 1024


def _gather_body(base_s, npass_s, sel_ref, aff_ref, basec_ref, v_ref, xs_ref, gs_ref, *, cap):
    b = pl.program_id(0)
    ne, s = sel_ref.shape
    nk = s // TOK_BLK
    first_cols = pl.program_id(1) == 0
    xs_ref[...] = jnp.zeros(xs_ref.shape, xs_ref.dtype)

    @pl.when(first_cols)
    def _():
        gs_ref[...] = jnp.zeros(gs_ref.shape, gs_ref.dtype)

    wiota = lax.broadcasted_iota(I32, (WIN, TOK_BLK), 0)
    for k in range(nk):
        toks = slice(k * TOK_BLK, (k + 1) * TOK_BLK)
        selk = sel_ref[:, toks]
        basek = basec_ref[k]

        def one_pass(p, carry, k=k, toks=toks, selk=selk, basek=basek):
            first = basek + p * WIN
            wbase = jnp.minimum(first, cap - WIN)
            rel = jnp.where(selk >= first, selk - wbase, -1)
            hits = [rel[e:e + 1, :] == wiota for e in range(ne)]
            onehot = jnp.concatenate([jnp.where(h, 1.0, 0.0) for h in hits], axis=0).astype(BF16)
            rows = jnp.dot(onehot, v_ref[toks, :], preferred_element_type=F32)
            offs = [pl.multiple_of(jnp.minimum(base_s[(b * nk + k) * ne + e] + p * WIN, cap - WIN), BF16_SUBLANES)
                    for e in range(ne)]
            for e in range(ne):
                xs_ref[e, pl.ds(offs[e], WIN), :] += rows[e * WIN:(e + 1) * WIN].astype(BF16)

            @pl.when(first_cols)
            def _():
                for e in range(ne):
                    gate = jnp.sum(jnp.where(hits[e], aff_ref[e:e + 1, toks], 0.0), axis=1, keepdims=True)
                    gs_ref[e, pl.ds(offs[e], WIN), :] += gate
            return carry

        lax.fori_loop(0, npass_s[b * nk + k], one_pass, 0)


def _gather(base, npass, sel, aff, v3, cap):
    nb, ne, s = sel.shape
    d = v3.shape[-1]
    nk = s // TOK_BLK
    grid_spec = pltpu.PrefetchScalarGridSpec(
        num_scalar_prefetch=2,
        grid=(nb, d // GATHER_DT),
        in_specs=[
            pl.BlockSpec((None, ne, s), lambda b, j, *_: (b, 0, 0)),
            pl.BlockSpec((None, ne, s), lambda b, j, *_: (b, 0, 0)),
            pl.BlockSpec((None, nk, ne, 1), lambda b, j, *_: (b, 0, 0, 0)),
            pl.BlockSpec((None, s, GATHER_DT), lambda b, j, *_: (b, 0, j)),
        ],
        out_specs=[
            pl.BlockSpec((ne, cap, GATHER_DT), lambda b, j, *_: (0, b, j)),
            pl.BlockSpec((ne, cap, 1), lambda b, j, *_: (0, b, 0)),
        ],
    )
    return pl.pallas_call(
        functools.partial(_gather_body, cap=cap),
        grid_spec=grid_spec,
        out_shape=[jax.ShapeDtypeStruct((ne, nb * cap, d), BF16),
                   jax.ShapeDtypeStruct((ne, nb * cap, 1), F32)],
        compiler_params=_params(("arbitrary", "arbitrary")),
        name="gather",
    )(base.reshape(-1), npass.reshape(-1), sel, aff, base[..., None], v3)


COMB_HSLOTS = 3


def _combine_body(base_s, npass_s, selc_ref, baser_ref, y_hbm, h_hbm, g_ref, o_ref, acc_ref, hbuf, ybuf, hsem, ysem,
                  *, cap, nb, nk):
    b = pl.program_id(0)
    k = pl.program_id(1)
    ne = ybuf.shape[1]
    per = ne // nk
    step = b * nk + k

    def h_copy(st):
        st = jnp.asarray(st, I32)
        rows = pl.ds(pl.multiple_of((st % nk) * TOK_BLK, TOK_BLK), TOK_BLK)
        slot = st % COMB_HSLOTS
        return pltpu.make_async_copy(h_hbm.at[st // nk, rows, :], hbuf.at[slot], hsem.at[slot])

    def y_copy(batch, j):
        batch = jnp.asarray(batch, I32)
        experts = pl.ds(j * per, per)
        rows = pl.ds(pl.multiple_of(batch * cap, cap), cap)
        return pltpu.make_async_copy(y_hbm.at[experts, rows, :], ybuf.at[batch % 2, experts], ysem.at[batch % 2, j])

    @pl.when(step == 0)
    def _prime():
        h_copy(0).start()
        h_copy(1).start()
        for j in range(nk):
            y_copy(0, j).start()

    @pl.when(step + 2 < nb * nk)
    def _():
        h_copy(step + 2).start()

    @pl.when(b + 1 < nb)
    def _():
        y_copy(b + 1, k).start()

    h_copy(step).wait()

    @pl.when(k == 0)
    def _():
        for j in range(nk):
            y_copy(b, j).wait()

    y_ref = ybuf.at[b % 2]
    acc_ref[...] = hbuf[step % COMB_HSLOTS]
    selk = selc_ref[...]
    basek = baser_ref[...]
    lane = lax.broadcasted_iota(I32, (ne, ne * WIN), 1)
    spread = jnp.where(lane // WIN == lax.broadcasted_iota(I32, (ne, ne * WIN), 0), 1.0, 0.0).astype(BF16)
    wlane = (lax.broadcasted_iota(I32, (1, ne * WIN), 1) % WIN).astype(F32)

    def one_pass(p, carry):
        first = basek + p * WIN
        wbase = jnp.minimum(first, cap - WIN)
        rel = jnp.where(selk >= first, selk - wbase, -1)
        relx = jnp.dot(rel.astype(F32).astype(BF16), spread, preferred_element_type=F32)
        onehot = jnp.where(relx == wlane, 1.0, 0.0).astype(BF16)
        wins = []
        for e in range(ne):
            off = jnp.minimum(base_s[(b * nk + k) * ne + e] + p * WIN, cap - WIN)
            wins.append(y_ref[e, pl.ds(pl.multiple_of(off, BF16_SUBLANES), WIN), :])
        acc_ref[...] += jnp.dot(onehot, jnp.concatenate(wins, axis=0), preferred_element_type=F32)
        return carry

    lax.fori_loop(0, npass_s[b * nk + k], one_pass, 0)
    h = acc_ref[...]
    ms = jnp.mean(h * h, axis=-1, keepdims=True)
    o_ref[...] = h * lax.rsqrt(ms + EPS) * g_ref[...]


def _combine(base, npass, sel_col, ys, h3, g_final, cap):
    nb, s, ne = sel_col.shape
    d = h3.shape[-1]
    nk = s // TOK_BLK
    grid_spec = pltpu.PrefetchScalarGridSpec(
        num_scalar_prefetch=2,
        grid=(nb, nk),
        in_specs=[
            pl.BlockSpec((None, TOK_BLK, ne), lambda b, k, *_: (b, k, 0)),
            pl.BlockSpec((None, None, 1, ne), lambda b, k, *_: (b, k, 0, 0)),
            pl.BlockSpec(memory_space=pl.ANY),
            pl.BlockSpec(memory_space=pl.ANY),
            pl.BlockSpec((1, d), lambda b, k, *_: (0, 0)),
        ],
        out_specs=pl.BlockSpec((None, TOK_BLK, d), lambda b, k, *_: (b, k, 0)),
        scratch_shapes=[
            pltpu.VMEM((TOK_BLK, d), F32),
            pltpu.VMEM((COMB_HSLOTS, TOK_BLK, d), F32),
            pltpu.VMEM((2, ne, cap, d), BF16),
            pltpu.SemaphoreType.DMA((COMB_HSLOTS,)),
            pltpu.SemaphoreType.DMA((2, nk)),
        ],
    )
    assert ne % nk == 0 and nb * nk >= 2
    return pl.pallas_call(
        functools.partial(_combine_body, cap=cap, nb=nb, nk=nk),
        grid_spec=grid_spec,
        out_shape=jax.ShapeDtypeStruct((nb, s, d), F32),
        compiler_params=_params(("arbitrary", "arbitrary")),
        name="combine",
    )(base.reshape(-1), npass.reshape(-1), sel_col, base[:, :, None, :], ys, h3, g_final)


FFN_TM = 512
FFN_STEPS = 8
FFN_NC = 512


def _ffn_body(xs_ref, gs_ref, wg_ref, wu_ref, wd_ref, y_ref, wgu_scr, wd_scr, hid_scr):
    e1 = pl.program_id(0)
    j = pl.program_id(1)
    ne = pl.num_programs(0) - 1
    ff = wg_ref.shape[1]
    nt = ff // LANES

    nc = y_ref.shape[1] // FFN_NC
    stage_slot = e1 % 2
    slot = (e1 - 1) % 2
    kr, fr = wg_ref.shape[0], wd_ref.shape[0]
    r0 = pl.multiple_of(j * kr, kr)
    f0 = pl.multiple_of(j * fr, BF16_SUBLANES)

    def stage_gate_up(t):
        src = slice(t * LANES, (t + 1) * LANES)
        wgu_scr[stage_slot, pl.ds(r0, kr), 2 * t * LANES:(2 * t + 1) * LANES] = wg_ref[:, src].astype(BF16)
        wgu_scr[stage_slot, pl.ds(r0, kr), (2 * t + 1) * LANES:(2 * t + 2) * LANES] = wu_ref[:, src].astype(BF16)

    def stage_down(c):
        cols = slice(c * FFN_NC, (c + 1) * FFN_NC)
        wd_scr[stage_slot, pl.ds(f0, fr), cols] = wd_ref[:, cols].astype(BF16)

    pieces = [functools.partial(stage_gate_up, t) for t in range(nt)] + [
        functools.partial(stage_down, c) for c in range(nc)]

    def spread(n_chunks):
        return [pieces[i::n_chunks] for i in range(n_chunks)]

    @pl.when(e1 == 0)
    def _load_first_expert():
        for piece in pieces:
            piece()

    @pl.when((e1 > 0) & (j % 2 == 0))
    def _gate_up():
        xs = xs_ref[...]
        for t, todo in enumerate(spread(nt)):
            for piece in todo:
                piece()
            res = jnp.dot(xs, wgu_scr[slot, :, 2 * t * LANES:(2 * t + 2) * LANES], preferred_element_type=F32)
            hid_scr[:, t * LANES:(t + 1) * LANES] = (jax.nn.silu(res[:, :LANES]) * res[:, LANES:]).astype(BF16)

    @pl.when((e1 > 0) & (j % 2 == 1))
    def _down():
        hid = hid_scr[...]
        for c, todo in enumerate(spread(nc)):
            for piece in todo:
                piece()
            cols = slice(c * FFN_NC, (c + 1) * FFN_NC)
            y = jnp.dot(hid, wd_scr[slot, :, cols], preferred_element_type=F32)
            y_ref[:, cols] = (y * gs_ref[...]).astype(BF16)


def _ffn(xs, gs, w_gate_e, w_up_e, w_down_e):
    ne, m, d = xs.shape
    ff = w_gate_e.shape[-1]
    assert m == FFN_TM * FFN_STEPS // 2 and d % FFN_STEPS == 0 and ff % (FFN_STEPS * BF16_SUBLANES) == 0
    assert ff % LANES == 0
    prev = lambda e1, j: (jnp.maximum(e1 - 1, 0), jnp.where(e1 == 0, 0, j // 2), 0)
    nxt = lambda e1, j: (jnp.minimum(e1, ne - 1), j, 0)
    return pl.pallas_call(
        _ffn_body,
        grid=(ne + 1, FFN_STEPS),
        in_specs=[
            pl.BlockSpec((None, FFN_TM, d), prev),
            pl.BlockSpec((None, FFN_TM, 1), prev),
            pl.BlockSpec((None, d // FFN_STEPS, ff), nxt),
            pl.BlockSpec((None, d // FFN_STEPS, ff), nxt),
            pl.BlockSpec((None, ff // FFN_STEPS, d), nxt),
        ],
        out_specs=pl.BlockSpec((None, FFN_TM, d), prev),
        out_shape=jax.ShapeDtypeStruct((ne, m, d), BF16),
        scratch_shapes=[pltpu.VMEM((2, d, 2 * ff), BF16), pltpu.VMEM((2, ff, d), BF16),
                        pltpu.VMEM((FFN_TM, ff), BF16)],
        compiler_params=_params(("arbitrary", "arbitrary")),
        name="ffn",
    )(xs, gs, w_gate_e, w_up_e, w_down_e)


TRIG_SPLIT = 32


def _trig_rows(n, rows, cols):
    ang = ((rows[:, None] * cols[None, :]) % n).astype(F32) * (2.0 * math.pi / n)
    return jnp.cos(ang), jnp.sin(ang)


def _dft_tables(n, rows, cols):
    nr = rows.shape[0]
    r1 = lax.iota(I32, nr // TRIG_SPLIT) * TRIG_SPLIT
    r0 = lax.iota(I32, TRIG_SPLIT)
    c1, s1 = _trig_rows(n, r1, cols)
    c0, s0 = _trig_rows(n, r0, cols)
    scale = 1.0 / math.sqrt(n)
    c1, s1 = (c1 * scale)[:, None, :], (s1 * scale)[:, None, :]
    c0, s0 = c0[None], s0[None]
    cos = (c1 * c0 - s1 * s0).reshape(nr, -1)
    sin = (s1 * c0 + c1 * s0).reshape(nr, -1)
    return cos, sin


def _pool_tables(s):
    i = lax.iota(I32, s)[:, None]
    k = (i // POOL_TM) * POOL_TM - POOL_HALO + lax.iota(I32, POOL_BAND)[None, :]
    out = []
    for half in POOL_HALF:
        lo = jnp.clip(i - half, 0, s)
        hi = jnp.clip(i + half, 0, s)
        inside = (k >= lo) & (k < hi)
        cnt = (hi - lo).astype(F32)
        out.append((jnp.where(inside, 1.0 / cnt, 0.0) - jnp.where(i == k, 1.0, 0.0)).astype(BF16))
    return jnp.stack(out)


def kernel(x, norm_mix_g, w_in, w_fourier_mix, w_pool_mix, pool_scale, w_branch_f, w_branch_p, w_gate,
           b_gate, w_out, norm_moe_g, w_router, w_expert_gate, w_expert_up, w_expert_down, norm_final_g):
    nb, s, d = x.shape
    assert w_in.shape[0] == 1, "single-layer block only"
    assert s % (2 * TOK_BLK) == 0 and max(POOL_HALF) <= POOL_HALO
    cap = CAPACITY_FACTOR * s // N_EXPERTS
    t = nb * s

    freqs = lax.iota(I32, s // DFT_RADIX)
    tabs = []
    for r in range(DFT_RADIX):
        cos_r, sin_r = _dft_tables(s, freqs, DFT_RADIX * freqs + r)
        tabs += [cos_r.astype(BF16), sin_r.astype(BF16)]
    trig = jnp.stack(tabs)
    chan = lax.iota(I32, GROUP)
    cos_c, sin_c = _trig_rows(GROUP, chan, chan)
    cos_c = (cos_c / math.sqrt(GROUP)).astype(BF16)
    sin_c = (sin_c / math.sqrt(GROUP)).astype(BF16)
    tri = (lax.iota(I32, s)[:, None] <= lax.iota(I32, s)[None, :]).astype(BF16)

    x2 = x.reshape(t, d)
    g_mix = norm_mix_g[0][:, None]
    pf, pp, gates = _proj(x2, (g_mix * w_in[0]).astype(BF16), (g_mix * w_gate[0]).astype(BF16), b_gate[0][None])
    pc, ps = _fourier(trig, pf.reshape(nb, s // DFT_RADIX, -1), s)
    pooled = _pool(_pool_tables(s), pp.reshape(nb, s, -1))
    wr = norm_moe_g[0][:, None] * w_router[0]
    wr_hi = wr.astype(BF16)
    wr_lo = (wr - wr_hi.astype(F32)).astype(BF16)
    h2, v, logits = _tail(
        pc.reshape(t, -1), ps.reshape(t, -1), pooled.reshape(t, -1), gates, x2, cos_c, sin_c,
        w_fourier_mix[0].astype(BF16), w_pool_mix[0].astype(BF16), pool_scale[0][None],
        w_branch_f[0].astype(BF16), w_branch_p[0].astype(BF16), w_out[0].astype(BF16),
        norm_moe_g[0][None], jnp.concatenate([wr_hi, wr_lo], axis=1))
    lg_t = jnp.swapaxes(logits.reshape(nb, s, N_EXPERTS), 1, 2)
    aff, sel, cum = _route(lg_t, tri, cap)
    base, npass = _window_plan(cum, cap)
    xs, gs = _gather(base, npass, sel, aff, v.reshape(nb, s, d), cap)
    ys = _ffn(xs, gs, w_expert_gate[0], w_expert_up[0], w_expert_down[0])
    return _combine(base, npass, jnp.swapaxes(sel, 1, 2), ys, h2.reshape(nb, s, d), norm_final_g[None], cap)
```

```python
import functools
import math

import jax
import jax.numpy as jnp
from jax import lax
from jax.experimental import pallas as pl
from jax.experimental.pallas import tpu as pltpu

F32 = jnp.float32
BF16 = jnp.bfloat16
I32 = jnp.int32

D_MODEL = 2048
FOURIER_WIDTH = 1024
N_GROUPS = 4
GROUP = 256
POOL_HALF = (1, 2, 4, 8)
N_EXPERTS = 16
EXPERT_FF = 1408
CAPACITY_FACTOR = 2
EPS = 1e-6

V7X_VMEM_LIMIT_BYTES = 58 * 1024 * 1024
BF16_SUBLANES = 16
LANES = 128


def _params(sem, vmem=V7X_VMEM_LIMIT_BYTES):
    return pltpu.CompilerParams(dimension_semantics=sem, vmem_limit_bytes=vmem)


def _const_spec(shape):
    nd = len(shape)
    return pl.BlockSpec(shape, lambda *_: (0,) * nd, pipeline_mode=pl.Buffered(1))


PROJ_TM = 512
PROJ_NC = 512
DFT_RADIX = 4


def _proj_body(x_ref, win_ref, wg_ref, bg_ref, pf_ref, pp_ref, gate_ref, split_ref):
    x = x_ref[...]
    xb = x.astype(BF16)
    inv = lax.rsqrt(jnp.mean(x * x, axis=-1, keepdims=True) + EPS)
    part = PROJ_TM // DFT_RADIX
    for j in range(FOURIER_WIDTH // PROJ_NC):
        res = jnp.dot(xb, win_ref[:, j * PROJ_NC:(j + 1) * PROJ_NC], preferred_element_type=F32) * inv
        for c in range(PROJ_NC // LANES):
            col = j * PROJ_NC + c * LANES
            split_ref[c] = res[:, c * LANES:(c + 1) * LANES]
            for r in range(DFT_RADIX):
                pf_ref[:, r * FOURIER_WIDTH + col:r * FOURIER_WIDTH + col + LANES] = (
                    split_ref[c, pl.ds(r, part, stride=DFT_RADIX), :].astype(BF16))
    for j in range((win_ref.shape[1] - FOURIER_WIDTH) // PROJ_NC):
        sl = slice(FOURIER_WIDTH + j * PROJ_NC, FOURIER_WIDTH + (j + 1) * PROJ_NC)
        res = jnp.dot(xb, win_ref[:, sl], preferred_element_type=F32) * inv
        pp_ref[:, j * PROJ_NC:(j + 1) * PROJ_NC] = res.astype(BF16)
    for j in range(wg_ref.shape[1] // PROJ_NC):
        sl = slice(j * PROJ_NC, (j + 1) * PROJ_NC)
        a = jnp.dot(xb, wg_ref[:, sl], preferred_element_type=F32) * inv + bg_ref[:, sl]
        gate_ref[:, sl] = jax.nn.sigmoid(a).astype(BF16)


def _proj(x2, w_in, w_gate, b_gate):
    t, d = x2.shape
    dm, dg = w_in.shape[1], w_gate.shape[1]
    return pl.pallas_call(
        _proj_body,
        grid=(t // PROJ_TM,),
        in_specs=[
            pl.BlockSpec((PROJ_TM, d), lambda i: (i, 0)),
            _const_spec((d, dm)),
            _const_spec((d, dg)),
            _const_spec((1, dg)),
        ],
        out_specs=[
            pl.BlockSpec((PROJ_TM // DFT_RADIX, DFT_RADIX * FOURIER_WIDTH), lambda i: (i, 0)),
            pl.BlockSpec((PROJ_TM, dm - FOURIER_WIDTH), lambda i: (i, 0)),
            pl.BlockSpec((PROJ_TM, dg), lambda i: (i, 0)),
        ],
        out_shape=[jax.ShapeDtypeStruct((t // DFT_RADIX, DFT_RADIX * FOURIER_WIDTH), BF16),
                   jax.ShapeDtypeStruct((t, dm - FOURIER_WIDTH), BF16),
                   jax.ShapeDtypeStruct((t, dg), BF16)],
        scratch_shapes=[pltpu.VMEM((PROJ_NC // LANES, PROJ_TM, LANES), F32)],
        compiler_params=_params(("parallel",)),
        name="proj",
    )(x2, w_in, w_gate, b_gate)


DFT_MC = 256
DFT_NC = 512


def _fourier_body(trig_ref, *refs):
    p_refs, (pc_ref, ps_ref) = refs[:DFT_RADIX], refs[DFT_RADIX:]
    part = trig_ref.shape[1]
    p_in = [r[...] for r in p_refs]
    for i in range(part // DFT_MC):
        rows = slice(i * DFT_MC, (i + 1) * DFT_MC)
        c = [jnp.dot(trig_ref[2 * r, rows, :], p_in[r], preferred_element_type=F32) for r in range(DFT_RADIX)]
        s = [jnp.dot(trig_ref[2 * r + 1, rows, :], p_in[r], preferred_element_type=F32) for r in range(DFT_RADIX)]
        ec, oc, es, os_ = c[0] + c[2], c[0] - c[2], s[0] + s[2], s[0] - s[2]
        fc, gc, fs, gs = c[1] + c[3], c[1] - c[3], s[1] + s[3], s[1] - s[3]
        xc = (ec + fc, oc - gs, ec - fc, oc + gs)
        xs = (es + fs, os_ + gc, es - fs, os_ - gc)
        for q in range(DFT_RADIX):
            out_rows = slice(q * part + i * DFT_MC, q * part + (i + 1) * DFT_MC)
            pc_ref[out_rows, :] = xc[q].astype(BF16)
            ps_ref[out_rows, :] = xs[q].astype(BF16)


POOL_TM = 256
POOL_HALO = 128


POOL_BAND = POOL_TM + 2 * POOL_HALO


def _pool_body(a_ref, p_ref, o_ref, first_group):
    s = a_ref.shape[1]
    for g in range(p_ref.shape[1] // GROUP):
        cols = slice(g * GROUP, (g + 1) * GROUP)
        for i in range(s // POOL_TM):
            rows = slice(i * POOL_TM, (i + 1) * POOL_TM)
            lo, hi = i * POOL_TM - POOL_HALO, (i + 1) * POOL_TM + POOL_HALO
            band = slice(max(-lo, 0), POOL_BAND - max(hi - s, 0))
            src = slice(max(lo, 0), min(hi, s))
            o_ref[rows, cols] = jnp.dot(a_ref[first_group + g, rows, band], p_ref[src, cols],
                                        preferred_element_type=F32).astype(BF16)


def _seqmix_body(trig_ref, bands_ref, *refs):
    p_refs, pp_ref = refs[:DFT_RADIX], refs[DFT_RADIX]
    pc_ref, ps_ref, po_ref = refs[DFT_RADIX + 1:]
    _fourier_body(trig_ref, *p_refs, pc_ref, ps_ref)
    _pool_body(bands_ref, pp_ref, po_ref, pl.program_id(1) * (DFT_NC // GROUP))


def _seqmix(trig, pool_bands, p_split, pp3, s):
    assert DFT_RADIX == 4 and pp3.shape[-1] == FOURIER_WIDTH
    b, part, _ = p_split.shape
    per_res = FOURIER_WIDTH // DFT_NC
    out = jax.ShapeDtypeStruct((b, s, FOURIER_WIDTH), BF16)
    res_spec = lambda r: pl.BlockSpec((None, part, DFT_NC), lambda i, g: (i, 0, r * per_res + g))
    cols = pl.BlockSpec((None, s, DFT_NC), lambda i, g: (i, 0, g))
    return pl.pallas_call(
        _seqmix_body,
        grid=(b, per_res),
        in_specs=[_const_spec(trig.shape), _const_spec(pool_bands.shape)]
        + [res_spec(r) for r in range(DFT_RADIX)] + [cols],
        out_specs=[cols] * 3,
        out_shape=[out, out, out],
        compiler_params=_params(("arbitrary", "arbitrary")),
        name="seqmix",
    )(trig, pool_bands, *([p_split] * DFT_RADIX), pp3)


TAIL_TM = 256


def _tail_body(pc_ref, ps_ref, pl_ref, gate_ref, x_ref, cc_ref, sc_ref, wf_ref, wp_ref, psc_ref, wbf_ref, wbp_ref,
               wout_ref, gm_ref, wr_ref, h_ref, v_ref, lg_ref):
    yf, yp = [], []
    for g in range(N_GROUPS):
        cols = slice(g * GROUP, (g + 1) * GROUP)
        z = (jnp.dot(pc_ref[:, cols], cc_ref[...], preferred_element_type=F32)
             - jnp.dot(ps_ref[:, cols], sc_ref[...], preferred_element_type=F32))
        yf.append(jnp.dot(z.astype(BF16), wf_ref[g], preferred_element_type=F32).astype(BF16))
        ypg = jnp.dot(pl_ref[:, cols], wp_ref[g], preferred_element_type=F32) * psc_ref[:, cols]
        yp.append(ypg.astype(BF16))
    yf = jnp.concatenate(yf, axis=-1)
    yp = jnp.concatenate(yp, axis=-1)
    d = wout_ref.shape[0]
    bf = jnp.dot(yf, wbf_ref[...], preferred_element_type=F32)
    bp = jnp.dot(yp, wbp_ref[...], preferred_element_type=F32)
    merged = gate_ref[:, :d].astype(F32) * bf + gate_ref[:, d:].astype(F32) * bp
    h = x_ref[...] + jnp.dot(merged.astype(BF16), wout_ref[...], preferred_element_type=F32)
    h_ref[...] = h
    inv = lax.rsqrt(jnp.mean(h * h, axis=-1, keepdims=True) + EPS)
    v_ref[...] = (h * inv * gm_ref[...]).astype(BF16)
    acc = jnp.dot(h.astype(BF16), wr_ref[...], preferred_element_type=F32)
    lg_ref[...] = (acc[:, :N_EXPERTS] + acc[:, N_EXPERTS:]) * inv


def _tail(pc2, ps2, pl2, gates, x2, cc, sc, wf, wp, pscale, wbf, wbp, wout, gm, wr2):
    t, d = x2.shape
    row = lambda width: pl.BlockSpec((TAIL_TM, width), lambda i: (i, 0))
    return pl.pallas_call(
        _tail_body,
        grid=(t // TAIL_TM,),
        in_specs=[
            row(pc2.shape[1]), row(ps2.shape[1]), row(pl2.shape[1]), row(gates.shape[1]), row(d),
            _const_spec(cc.shape), _const_spec(sc.shape), _const_spec(wf.shape), _const_spec(wp.shape),
            _const_spec(pscale.shape), _const_spec(wbf.shape), _const_spec(wbp.shape),
            _const_spec(wout.shape), _const_spec(gm.shape), _const_spec(wr2.shape),
        ],
        out_specs=[row(d), row(d), row(N_EXPERTS)],
        out_shape=[jax.ShapeDtypeStruct((t, d), F32), jax.ShapeDtypeStruct((t, d), BF16),
                   jax.ShapeDtypeStruct((t, N_EXPERTS), F32)],
        compiler_params=_params(("parallel",)),
        name="tail",
    )(pc2, ps2, pl2, gates, x2, cc, sc, wf, wp, pscale, wbf, wbp, wout, gm, wr2)


def _route_body(lg_ref, tri_ref, aff_ref, sel_ref, cum_ref, *, cap):
    nb, ne, s = lg_ref.shape
    lg = lg_ref[...]
    mx = jnp.max(lg, axis=1, keepdims=True)
    ex = jnp.exp(lg - mx)
    aff = ex / jnp.sum(ex, axis=1, keepdims=True)
    aff_ref[...] = aff
    aff2 = aff.reshape(nb * ne, s)

    def step(i, bits):
        cand = bits | jnp.left_shift(jnp.int32(1), 30 - i)
        n_ge = jnp.sum(jnp.where(aff2 >= pltpu.bitcast(cand, F32), 1.0, 0.0), axis=1, keepdims=True)
        return jnp.where(n_ge >= cap, cand, bits)

    thr = pltpu.bitcast(lax.fori_loop(0, 31, step, jnp.zeros((nb * ne, 1), I32)), F32)
    above = aff2 > thr
    tie = aff2 == thr
    n_above = jnp.sum(jnp.where(above, 1.0, 0.0), axis=1, keepdims=True)
    tri = tri_ref[...]
    tie_rank = jnp.dot(jnp.where(tie, 1.0, 0.0).astype(BF16), tri, preferred_element_type=F32)
    chosen = above | (tie & (tie_rank <= (cap - n_above)))
    cum = jnp.dot(jnp.where(chosen, 1.0, 0.0).astype(BF16), tri, preferred_element_type=F32)
    cum_ref[...] = cum.astype(I32).reshape(nb, ne, s)
    sel_ref[...] = jnp.where(chosen, cum - 1.0, -1.0).astype(I32).reshape(nb, ne, s)


def _route(lg_t, tri, cap):
    nb, ne, s = lg_t.shape
    ints = jax.ShapeDtypeStruct((nb, ne, s), I32)
    return pl.pallas_call(
        functools.partial(_route_body, cap=cap),
        out_shape=[jax.ShapeDtypeStruct((nb, ne, s), F32), ints, ints],
        compiler_params=pltpu.CompilerParams(vmem_limit_bytes=V7X_VMEM_LIMIT_BYTES),
        name="route",
    )(lg_t, tri)


TOK_BLK = 256
WIN = 64


def _window_plan(cum, cap):
    c_end = cum[:, :, TOK_BLK - 1::TOK_BLK]
    c_start = jnp.concatenate([jnp.zeros_like(c_end[:, :, :1]), c_end[:, :, :-1]], axis=-1)
    base = (c_start // BF16_SUBLANES) * BF16_SUBLANES
    passes = jnp.where(c_end > c_start, (c_end - base + WIN - 1) // WIN, 0)
    return jnp.swapaxes(base, 1, 2), jnp.max(passes, axis=1)


GATHER_DT = 1024


def _gather_body(base_s, npass_s, sel_ref, aff_ref, basec_ref, v_ref, xs_ref, gs_ref, *, cap):
    b = pl.program_id(0)
    ne, s = sel_ref.shape
    nk = s // TOK_BLK
    first_cols = pl.program_id(1) == 0
    xs_ref[...] = jnp.zeros(xs_ref.shape, xs_ref.dtype)

    @pl.when(first_cols)
    def _():
        gs_ref[...] = jnp.zeros(gs_ref.shape, gs_ref.dtype)

    wiota = lax.broadcasted_iota(I32, (WIN, TOK_BLK), 0)
    for k in range(nk):
        toks = slice(k * TOK_BLK, (k + 1) * TOK_BLK)
        selk = sel_ref[:, toks]
        basek = basec_ref[k]

        def one_pass(p, carry, k=k, toks=toks, selk=selk, basek=basek):
            first = basek + p * WIN
            wbase = jnp.minimum(first, cap - WIN)
            rel = jnp.where(selk >= first, selk - wbase, -1)
            hits = [rel[e:e + 1, :] == wiota for e in range(ne)]
            onehot = jnp.concatenate([jnp.where(h, 1.0, 0.0) for h in hits], axis=0).astype(BF16)
            rows = jnp.dot(onehot, v_ref[toks, :], preferred_element_type=F32)
            offs = [pl.multiple_of(jnp.minimum(base_s[(b * nk + k) * ne + e] + p * WIN, cap - WIN), BF16_SUBLANES)
                    for e in range(ne)]
            for e in range(ne):
                xs_ref[e, pl.ds(offs[e], WIN), :] += rows[e * WIN:(e + 1) * WIN].astype(BF16)

            @pl.when(first_cols)
            def _():
                for e in range(ne):
                    gate = jnp.sum(jnp.where(hits[e], aff_ref[e:e + 1, toks], 0.0), axis=1, keepdims=True)
                    gs_ref[e, pl.ds(offs[e], WIN), :] += gate
            return carry

        lax.fori_loop(0, npass_s[b * nk + k], one_pass, 0)


def _gather(base, npass, sel, aff, v3, cap):
    nb, ne, s = sel.shape
    d = v3.shape[-1]
    nk = s // TOK_BLK
    grid_spec = pltpu.PrefetchScalarGridSpec(
        num_scalar_prefetch=2,
        grid=(nb, d // GATHER_DT),
        in_specs=[
            pl.BlockSpec((None, ne, s), lambda b, j, *_: (b, 0, 0)),
            pl.BlockSpec((None, ne, s), lambda b, j, *_: (b, 0, 0)),
            pl.BlockSpec((None, nk, ne, 1), lambda b, j, *_: (b, 0, 0, 0)),
            pl.BlockSpec((None, s, GATHER_DT), lambda b, j, *_: (b, 0, j)),
        ],
        out_specs=[
            pl.BlockSpec((ne, cap, GATHER_DT), lambda b, j, *_: (0, b, j)),
            pl.BlockSpec((ne, cap, 1), lambda b, j, *_: (0, b, 0)),
        ],
    )
    return pl.pallas_call(
        functools.partial(_gather_body, cap=cap),
        grid_spec=grid_spec,
        out_shape=[jax.ShapeDtypeStruct((ne, nb * cap, d), BF16),
                   jax.ShapeDtypeStruct((ne, nb * cap, 1), F32)],
        compiler_params=_params(("arbitrary", "arbitrary")),
        name="gather",
    )(base.reshape(-1), npass.reshape(-1), sel, aff, base[..., None], v3)


COMB_HSLOTS = 3


def _combine_body(base_s, npass_s, selc_ref, baser_ref, y_hbm, h_hbm, g_ref, o_ref, acc_ref, hbuf, ybuf, hsem, ysem,
                  *, cap, nb, nk):
    b = pl.program_id(0)
    k = pl.program_id(1)
    ne = ybuf.shape[1]
    per = ne // nk
    step = b * nk + k

    def h_copy(st):
        st = jnp.asarray(st, I32)
        rows = pl.ds(pl.multiple_of((st % nk) * TOK_BLK, TOK_BLK), TOK_BLK)
        slot = st % COMB_HSLOTS
        return pltpu.make_async_copy(h_hbm.at[st // nk, rows, :], hbuf.at[slot], hsem.at[slot])

    def y_copy(batch, j):
        batch = jnp.asarray(batch, I32)
        experts = pl.ds(j * per, per)
        rows = pl.ds(pl.multiple_of(batch * cap, cap), cap)
        return pltpu.make_async_copy(y_hbm.at[experts, rows, :], ybuf.at[batch % 2, experts], ysem.at[batch % 2, j])

    @pl.when(step == 0)
    def _prime():
        h_copy(0).start()
        h_copy(1).start()
        for j in range(nk):
            y_copy(0, j).start()

    @pl.when(step + 2 < nb * nk)
    def _():
        h_copy(step + 2).start()

    @pl.when(b + 1 < nb)
    def _():
        y_copy(b + 1, k).start()

    h_copy(step).wait()

    @pl.when(k == 0)
    def _():
        for j in range(nk):
            y_copy(b, j).wait()

    y_ref = ybuf.at[b % 2]
    acc_ref[...] = hbuf[step % COMB_HSLOTS]
    selk = selc_ref[...]
    basek = baser_ref[...]
    lane = lax.broadcasted_iota(I32, (ne, ne * WIN), 1)
    spread = jnp.where(lane // WIN == lax.broadcasted_iota(I32, (ne, ne * WIN), 0), 1.0, 0.0).astype(BF16)
    wlane = (lax.broadcasted_iota(I32, (1, ne * WIN), 1) % WIN).astype(F32)

    def one_pass(p, carry):
        first = basek + p * WIN
        wbase = jnp.minimum(first, cap - WIN)
        rel = jnp.where(selk >= first, selk - wbase, -1)
        relx = jnp.dot(rel.astype(F32).astype(BF16), spread, preferred_element_type=F32)
        onehot = jnp.where(relx == wlane, 1.0, 0.0).astype(BF16)
        wins = []
        for e in range(ne):
            off = jnp.minimum(base_s[(b * nk + k) * ne + e] + p * WIN, cap - WIN)
            wins.append(y_ref[e, pl.ds(pl.multiple_of(off, BF16_SUBLANES), WIN), :])
        acc_ref[...] += jnp.dot(onehot, jnp.concatenate(wins, axis=0), preferred_element_type=F32)
        return carry

    lax.fori_loop(0, npass_s[b * nk + k], one_pass, 0)
    h = acc_ref[...]
    ms = jnp.mean(h * h, axis=-1, keepdims=True)
    o_ref[...] = h * lax.rsqrt(ms + EPS) * g_ref[...]


def _combine(base, npass, sel_col, ys, h3, g_final, cap):
    nb, s, ne = sel_col.shape
    d = h3.shape[-1]
    nk = s // TOK_BLK
    grid_spec = pltpu.PrefetchScalarGridSpec(
        num_scalar_prefetch=2,
        grid=(nb, nk),
        in_specs=[
            pl.BlockSpec((None, TOK_BLK, ne), lambda b, k, *_: (b, k, 0)),
            pl.BlockSpec((None, None, 1, ne), lambda b, k, *_: (b, k, 0, 0)),
            pl.BlockSpec(memory_space=pl.ANY),
            pl.BlockSpec(memory_space=pl.ANY),
            pl.BlockSpec((1, d), lambda b, k, *_: (0, 0)),
        ],
        out_specs=pl.BlockSpec((None, TOK_BLK, d), lambda b, k, *_: (b, k, 0)),
        scratch_shapes=[
            pltpu.VMEM((TOK_BLK, d), F32),
            pltpu.VMEM((COMB_HSLOTS, TOK_BLK, d), F32),
            pltpu.VMEM((2, ne, cap, d), BF16),
            pltpu.SemaphoreType.DMA((COMB_HSLOTS,)),
            pltpu.SemaphoreType.DMA((2, nk)),
        ],
    )
    assert ne % nk == 0 and nb * nk >= 2
    return pl.pallas_call(
        functools.partial(_combine_body, cap=cap, nb=nb, nk=nk),
        grid_spec=grid_spec,
        out_shape=jax.ShapeDtypeStruct((nb, s, d), F32),
        compiler_params=_params(("arbitrary", "arbitrary")),
        name="combine",
    )(base.reshape(-1), npass.reshape(-1), sel_col, base[:, :, None, :], ys, h3, g_final)


FFN_TM = 512
FFN_STEPS = 8
FFN_NC = 512


def _ffn_body(xs_ref, gs_ref, wg_ref, wu_ref, wd_ref, y_ref, wgu_scr, wd_scr, hid_scr):
    e1 = pl.program_id(0)
    j = pl.program_id(1)
    ne = pl.num_programs(0) - 1
    ff = wg_ref.shape[1]
    nt = ff // LANES

    nc = y_ref.shape[1] // FFN_NC
    stage_slot = e1 % 2
    slot = (e1 - 1) % 2
    kr, fr = wg_ref.shape[0], wd_ref.shape[0]
    r0 = pl.multiple_of(j * kr, kr)
    f0 = pl.multiple_of(j * fr, BF16_SUBLANES)

    def stage_gate_up(t):
        src = slice(t * LANES, (t + 1) * LANES)
        wgu_scr[stage_slot, pl.ds(r0, kr), 2 * t * LANES:(2 * t + 1) * LANES] = wg_ref[:, src].astype(BF16)
        wgu_scr[stage_slot, pl.ds(r0, kr), (2 * t + 1) * LANES:(2 * t + 2) * LANES] = wu_ref[:, src].astype(BF16)

    def stage_down(c):
        cols = slice(c * FFN_NC, (c + 1) * FFN_NC)
        wd_scr[stage_slot, pl.ds(f0, fr), cols] = wd_ref[:, cols].astype(BF16)

    pieces = [functools.partial(stage_gate_up, t) for t in range(nt)] + [
        functools.partial(stage_down, c) for c in range(nc)]

    def spread(n_chunks):
        return [pieces[i::n_chunks] for i in range(n_chunks)]

    @pl.when(e1 == 0)
    def _load_first_expert():
        for piece in pieces:
            piece()

    @pl.when((e1 > 0) & (j % 2 == 0))
    def _gate_up():
        xs = xs_ref[...]
        for t, todo in enumerate(spread(nt)):
            for piece in todo:
                piece()
            res = jnp.dot(xs, wgu_scr[slot, :, 2 * t * LANES:(2 * t + 2) * LANES], preferred_element_type=F32)
            hid_scr[:, t * LANES:(t + 1) * LANES] = (jax.nn.silu(res[:, :LANES]) * res[:, LANES:]).astype(BF16)

    @pl.when((e1 > 0) & (j % 2 == 1))
    def _down():
        hid = hid_scr[...]
        for c, todo in enumerate(spread(nc)):
            for piece in todo:
                piece()
            cols = slice(c * FFN_NC, (c + 1) * FFN_NC)
            y = jnp.dot(hid, wd_scr[slot, :, cols], preferred_element_type=F32)
            y_ref[:, cols] = (y * gs_ref[...]).astype(BF16)


def _ffn(xs, gs, w_gate_e, w_up_e, w_down_e):
    ne, m, d = xs.shape
    ff = w_gate_e.shape[-1]
    assert m == FFN_TM * FFN_STEPS // 2 and d % FFN_STEPS == 0 and ff % (FFN_STEPS * BF16_SUBLANES) == 0
    assert ff % LANES == 0
    prev = lambda e1, j: (jnp.maximum(e1 - 1, 0), jnp.where(e1 == 0, 0, j // 2), 0)
    nxt = lambda e1, j: (jnp.minimum(e1, ne - 1), j, 0)
    return pl.pallas_call(
        _ffn_body,
        grid=(ne + 1, FFN_STEPS),
        in_specs=[
            pl.BlockSpec((None, FFN_TM, d), prev),
            pl.BlockSpec((None, FFN_TM, 1), prev),
            pl.BlockSpec((None, d // FFN_STEPS, ff), nxt),
            pl.BlockSpec((None, d // FFN_STEPS, ff), nxt),
            pl.BlockSpec((None, ff // FFN_STEPS, d), nxt),
        ],
        out_specs=pl.BlockSpec((None, FFN_TM, d), prev),
        out_shape=jax.ShapeDtypeStruct((ne, m, d), BF16),
        scratch_shapes=[pltpu.VMEM((2, d, 2 * ff), BF16), pltpu.VMEM((2, ff, d), BF16),
                        pltpu.VMEM((FFN_TM, ff), BF16)],
        compiler_params=_params(("arbitrary", "arbitrary")),
        name="ffn",
    )(xs, gs, w_gate_e, w_up_e, w_down_e)


TRIG_SPLIT = 32


def _trig_rows(n, rows, cols):
    ang = ((rows[:, None] * cols[None, :]) % n).astype(F32) * (2.0 * math.pi / n)
    return jnp.cos(ang), jnp.sin(ang)


def _dft_tables(n, rows, cols):
    nr = rows.shape[0]
    r1 = lax.iota(I32, nr // TRIG_SPLIT) * TRIG_SPLIT
    r0 = lax.iota(I32, TRIG_SPLIT)
    c1, s1 = _trig_rows(n, r1, cols)
    c0, s0 = _trig_rows(n, r0, cols)
    scale = 1.0 / math.sqrt(n)
    c1, s1 = (c1 * scale)[:, None, :], (s1 * scale)[:, None, :]
    c0, s0 = c0[None], s0[None]
    cos = (c1 * c0 - s1 * s0).reshape(nr, -1)
    sin = (s1 * c0 + c1 * s0).reshape(nr, -1)
    return cos, sin


def _pool_tables(s):
    i = lax.iota(I32, s)[:, None]
    k = (i // POOL_TM) * POOL_TM - POOL_HALO + lax.iota(I32, POOL_BAND)[None, :]
    out = []
    for half in POOL_HALF:
        lo = jnp.clip(i - half, 0, s)
        hi = jnp.clip(i + half, 0, s)
        inside = (k >= lo) & (k < hi)
        cnt = (hi - lo).astype(F32)
        out.append((jnp.where(inside, 1.0 / cnt, 0.0) - jnp.where(i == k, 1.0, 0.0)).astype(BF16))
    return jnp.stack(out)


def kernel(x, norm_mix_g, w_in, w_fourier_mix, w_pool_mix, pool_scale, w_branch_f, w_branch_p, w_gate,
           b_gate, w_out, norm_moe_g, w_router, w_expert_gate, w_expert_up, w_expert_down, norm_final_g):
    nb, s, d = x.shape
    assert w_in.shape[0] == 1, "single-layer block only"
    assert s % (2 * TOK_BLK) == 0 and max(POOL_HALF) <= POOL_HALO
    cap = CAPACITY_FACTOR * s // N_EXPERTS
    t = nb * s

    freqs = lax.iota(I32, s // DFT_RADIX)
    tabs = []
    for r in range(DFT_RADIX):
        cos_r, sin_r = _dft_tables(s, freqs, DFT_RADIX * freqs + r)
        tabs += [cos_r.astype(BF16), sin_r.astype(BF16)]
    trig = jnp.stack(tabs)
    chan = lax.iota(I32, GROUP)
    cos_c, sin_c = _trig_rows(GROUP, chan, chan)
    cos_c = (cos_c / math.sqrt(GROUP)).astype(BF16)
    sin_c = (sin_c / math.sqrt(GROUP)).astype(BF16)
    tri = (lax.iota(I32, s)[:, None] <= lax.iota(I32, s)[None, :]).astype(BF16)

    x2 = x.reshape(t, d)
    g_mix = norm_mix_g[0][:, None]
    pf, pp, gates = _proj(x2, (g_mix * w_in[0]).astype(BF16), (g_mix * w_gate[0]).astype(BF16), b_gate[0][None])
    pc, ps, pooled = _seqmix(trig, _pool_tables(s), pf.reshape(nb, s // DFT_RADIX, -1), pp.reshape(nb, s, -1), s)
    wr = norm_moe_g[0][:, None] * w_router[0]
    wr_hi = wr.astype(BF16)
    wr_lo = (wr - wr_hi.astype(F32)).astype(BF16)
    h2, v, logits = _tail(
        pc.reshape(t, -1), ps.reshape(t, -1), pooled.reshape(t, -1), gates, x2, cos_c, sin_c,
        w_fourier_mix[0].astype(BF16), w_pool_mix[0].astype(BF16), pool_scale[0][None],
        w_branch_f[0].astype(BF16), w_branch_p[0].astype(BF16), w_out[0].astype(BF16),
        norm_moe_g[0][None], jnp.concatenate([wr_hi, wr_lo], axis=1))
    lg_t = jnp.swapaxes(logits.reshape(nb, s, N_EXPERTS), 1, 2)
    aff, sel, cum = _route(lg_t, tri, cap)
    base, npass = _window_plan(cum, cap)
    xs, gs = _gather(base, npass, sel, aff, v.reshape(nb, s, d), cap)
    ys = _ffn(xs, gs, w_expert_gate[0], w_expert_up[0], w_expert_down[0])
    return _combine(base, npass, jnp.swapaxes(sel, 1, 2), ys, h2.reshape(nb, s, d), norm_final_g[None], cap)
```

```python
import functools
import math

import jax
import jax.numpy as jnp
from jax import lax
from jax.experimental import pallas as pl
from jax.experimental.pallas import tpu as pltpu

F32 = jnp.float32
BF16 = jnp.bfloat16
I32 = jnp.int32

D_MODEL = 2048
FOURIER_WIDTH = 1024
N_GROUPS = 4
GROUP = 256
POOL_HALF = (1, 2, 4, 8)
N_EXPERTS = 16
EXPERT_FF = 1408
CAPACITY_FACTOR = 2
EPS = 1e-6

V7X_VMEM_LIMIT_BYTES = 58 * 1024 * 1024
BF16_SUBLANES = 16
LANES = 128


def _params(sem, vmem=V7X_VMEM_LIMIT_BYTES):
    return pltpu.CompilerParams(dimension_semantics=sem, vmem_limit_bytes=vmem)


def _const_spec(shape):
    nd = len(shape)
    return pl.BlockSpec(shape, lambda *_: (0,) * nd, pipeline_mode=pl.Buffered(1))


PROJ_TM = 512
PROJ_NC = 512
DFT_RADIX = 4


def _proj_body(x_ref, win_ref, wg_ref, bg_ref, pf_ref, pp_ref, gate_ref, split_ref):
    x = x_ref[...]
    xb = x.astype(BF16)
    inv = lax.rsqrt(jnp.mean(x * x, axis=-1, keepdims=True) + EPS)
    part = PROJ_TM // DFT_RADIX
    for j in range(FOURIER_WIDTH // PROJ_NC):
        res = jnp.dot(xb, win_ref[:, j * PROJ_NC:(j + 1) * PROJ_NC], preferred_element_type=F32) * inv
        for c in range(PROJ_NC // LANES):
            col = j * PROJ_NC + c * LANES
            split_ref[c] = res[:, c * LANES:(c + 1) * LANES]
            for r in range(DFT_RADIX):
                pf_ref[:, r * FOURIER_WIDTH + col:r * FOURIER_WIDTH + col + LANES] = (
                    split_ref[c, pl.ds(r, part, stride=DFT_RADIX), :].astype(BF16))
    for j in range((win_ref.shape[1] - FOURIER_WIDTH) // PROJ_NC):
        sl = slice(FOURIER_WIDTH + j * PROJ_NC, FOURIER_WIDTH + (j + 1) * PROJ_NC)
        res = jnp.dot(xb, win_ref[:, sl], preferred_element_type=F32) * inv
        pp_ref[:, j * PROJ_NC:(j + 1) * PROJ_NC] = res.astype(BF16)
    for j in range(wg_ref.shape[1] // PROJ_NC):
        sl = slice(j * PROJ_NC, (j + 1) * PROJ_NC)
        a = jnp.dot(xb, wg_ref[:, sl], preferred_element_type=F32) * inv + bg_ref[:, sl]
        gate_ref[:, sl] = jax.nn.sigmoid(a).astype(BF16)


def _proj(x2, w_in, w_gate, b_gate):
    t, d = x2.shape
    dm, dg = w_in.shape[1], w_gate.shape[1]
    return pl.pallas_call(
        _proj_body,
        grid=(t // PROJ_TM,),
        in_specs=[
            pl.BlockSpec((PROJ_TM, d), lambda i: (i, 0)),
            _const_spec((d, dm)),
            _const_spec((d, dg)),
            _const_spec((1, dg)),
        ],
        out_specs=[
            pl.BlockSpec((PROJ_TM // DFT_RADIX, DFT_RADIX * FOURIER_WIDTH), lambda i: (i, 0)),
            pl.BlockSpec((PROJ_TM, dm - FOURIER_WIDTH), lambda i: (i, 0)),
            pl.BlockSpec((PROJ_TM, dg), lambda i: (i, 0)),
        ],
        out_shape=[jax.ShapeDtypeStruct((t // DFT_RADIX, DFT_RADIX * FOURIER_WIDTH), BF16),
                   jax.ShapeDtypeStruct((t, dm - FOURIER_WIDTH), BF16),
                   jax.ShapeDtypeStruct((t, dg), BF16)],
        scratch_shapes=[pltpu.VMEM((PROJ_NC // LANES, PROJ_TM, LANES), F32)],
        compiler_params=_params(("parallel",)),
        name="proj",
    )(x2, w_in, w_gate, b_gate)


DFT_MC = 256
DFT_NC = 512


def _fourier_body(trig_ref, *refs):
    p_refs, (pc_ref, ps_ref) = refs[:DFT_RADIX], refs[DFT_RADIX:]
    part = trig_ref.shape[1]
    p_in = [r[...] for r in p_refs]
    for i in range(part // DFT_MC):
        rows = slice(i * DFT_MC, (i + 1) * DFT_MC)
        c = [jnp.dot(trig_ref[2 * r, rows, :], p_in[r], preferred_element_type=F32) for r in range(DFT_RADIX)]
        s = [jnp.dot(trig_ref[2 * r + 1, rows, :], p_in[r], preferred_element_type=F32) for r in range(DFT_RADIX)]
        ec, oc, es, os_ = c[0] + c[2], c[0] - c[2], s[0] + s[2], s[0] - s[2]
        fc, gc, fs, gs = c[1] + c[3], c[1] - c[3], s[1] + s[3], s[1] - s[3]
        xc = (ec + fc, oc - gs, ec - fc, oc + gs)
        xs = (es + fs, os_ + gc, es - fs, os_ - gc)
        for q in range(DFT_RADIX):
            out_rows = slice(q * part + i * DFT_MC, q * part + (i + 1) * DFT_MC)
            pc_ref[out_rows, :] = xc[q].astype(BF16)
            ps_ref[out_rows, :] = xs[q].astype(BF16)


POOL_TM = 256
POOL_HALO = 128


POOL_BAND = POOL_TM + 2 * POOL_HALO


def _pool_body(a_ref, p_ref, o_ref, first_group):
    s = a_ref.shape[1]
    for g in range(p_ref.shape[1] // GROUP):
        cols = slice(g * GROUP, (g + 1) * GROUP)
        for i in range(s // POOL_TM):
            rows = slice(i * POOL_TM, (i + 1) * POOL_TM)
            lo, hi = i * POOL_TM - POOL_HALO, (i + 1) * POOL_TM + POOL_HALO
            band = slice(max(-lo, 0), POOL_BAND - max(hi - s, 0))
            src = slice(max(lo, 0), min(hi, s))
            o_ref[rows, cols] = jnp.dot(a_ref[first_group + g, rows, band], p_ref[src, cols],
                                        preferred_element_type=F32).astype(BF16)


def _seqmix_body(trig_ref, bands_ref, *refs):
    p_refs, pp_ref = refs[:DFT_RADIX], refs[DFT_RADIX]
    pc_ref, ps_ref, po_ref = refs[DFT_RADIX + 1:]
    _fourier_body(trig_ref, *p_refs, pc_ref, ps_ref)
    _pool_body(bands_ref, pp_ref, po_ref, pl.program_id(1) * (DFT_NC // GROUP))


def _seqmix(trig, pool_bands, p_split, pp3, s):
    assert DFT_RADIX == 4 and pp3.shape[-1] == FOURIER_WIDTH
    b, part, _ = p_split.shape
    per_res = FOURIER_WIDTH // DFT_NC
    out = jax.ShapeDtypeStruct((b, s, FOURIER_WIDTH), BF16)
    res_spec = lambda r: pl.BlockSpec((None, part, DFT_NC), lambda i, g: (i, 0, r * per_res + g))
    cols = pl.BlockSpec((None, s, DFT_NC), lambda i, g: (i, 0, g))
    return pl.pallas_call(
        _seqmix_body,
        grid=(b, per_res),
        in_specs=[_const_spec(trig.shape), _const_spec(pool_bands.shape)]
        + [res_spec(r) for r in range(DFT_RADIX)] + [cols],
        out_specs=[cols] * 3,
        out_shape=[out, out, out],
        compiler_params=_params(("arbitrary", "arbitrary")),
        name="seqmix",
    )(trig, pool_bands, *([p_split] * DFT_RADIX), pp3)


TAIL_TM = 256


def _tail_body(pc_ref, ps_ref, pl_ref, gate_ref, x_ref, cc_ref, sc_ref, wf_ref, wp_ref, psc_ref, wbf_ref, wbp_ref,
               wout_ref, gm_ref, wr_ref, h_ref, v_ref, lg_ref):
    yf, yp = [], []
    for g in range(N_GROUPS):
        cols = slice(g * GROUP, (g + 1) * GROUP)
        z = (jnp.dot(pc_ref[:, cols], cc_ref[...], preferred_element_type=F32)
             - jnp.dot(ps_ref[:, cols], sc_ref[...], preferred_element_type=F32))
        yf.append(jnp.dot(z.astype(BF16), wf_ref[g], preferred_element_type=F32).astype(BF16))
        ypg = jnp.dot(pl_ref[:, cols], wp_ref[g], preferred_element_type=F32) * psc_ref[:, cols]
        yp.append(ypg.astype(BF16))
    yf = jnp.concatenate(yf, axis=-1)
    yp = jnp.concatenate(yp, axis=-1)
    d = wout_ref.shape[0]
    bf = jnp.dot(yf, wbf_ref[...], preferred_element_type=F32)
    bp = jnp.dot(yp, wbp_ref[...], preferred_element_type=F32)
    merged = gate_ref[:, :d].astype(F32) * bf + gate_ref[:, d:].astype(F32) * bp
    h = x_ref[...] + jnp.dot(merged.astype(BF16), wout_ref[...], preferred_element_type=F32)
    h_ref[...] = h
    inv = lax.rsqrt(jnp.mean(h * h, axis=-1, keepdims=True) + EPS)
    v_ref[...] = (h * inv * gm_ref[...]).astype(BF16)
    acc = jnp.dot(h.astype(BF16), wr_ref[...], preferred_element_type=F32)
    lg_ref[...] = (acc[:, :N_EXPERTS] + acc[:, N_EXPERTS:]) * inv


def _tail(pc2, ps2, pl2, gates, x2, cc, sc, wf, wp, pscale, wbf, wbp, wout, gm, wr2):
    t, d = x2.shape
    row = lambda width: pl.BlockSpec((TAIL_TM, width), lambda i: (i, 0))
    return pl.pallas_call(
        _tail_body,
        grid=(t // TAIL_TM,),
        in_specs=[
            row(pc2.shape[1]), row(ps2.shape[1]), row(pl2.shape[1]), row(gates.shape[1]), row(d),
            _const_spec(cc.shape), _const_spec(sc.shape), _const_spec(wf.shape), _const_spec(wp.shape),
            _const_spec(pscale.shape), _const_spec(wbf.shape), _const_spec(wbp.shape),
            _const_spec(wout.shape), _const_spec(gm.shape), _const_spec(wr2.shape),
        ],
        out_specs=[row(d), row(d), row(N_EXPERTS)],
        out_shape=[jax.ShapeDtypeStruct((t, d), F32), jax.ShapeDtypeStruct((t, d), BF16),
                   jax.ShapeDtypeStruct((t, N_EXPERTS), F32)],
        compiler_params=_params(("parallel",)),
        name="tail",
    )(pc2, ps2, pl2, gates, x2, cc, sc, wf, wp, pscale, wbf, wbp, wout, gm, wr2)


def _route_body(lg_ref, tri_ref, aff_ref, sel_ref, cum_ref, *, cap):
    nb, ne, s = lg_ref.shape
    lg = lg_ref[...]
    mx = jnp.max(lg, axis=1, keepdims=True)
    ex = jnp.exp(lg - mx)
    aff = ex / jnp.sum(ex, axis=1, keepdims=True)
    aff_ref[...] = aff
    aff2 = aff.reshape(nb * ne, s)

    def step(i, bits):
        cand = bits | jnp.left_shift(jnp.int32(1), 30 - i)
        n_ge = jnp.sum(jnp.where(aff2 >= pltpu.bitcast(cand, F32), 1.0, 0.0), axis=1, keepdims=True)
        return jnp.where(n_ge >= cap, cand, bits)

    thr = pltpu.bitcast(lax.fori_loop(0, 31, step, jnp.zeros((nb * ne, 1), I32)), F32)
    above = aff2 > thr
    tie = aff2 == thr
    n_above = jnp.sum(jnp.where(above, 1.0, 0.0), axis=1, keepdims=True)
    tri = tri_ref[...]
    tie_rank = jnp.dot(jnp.where(tie, 1.0, 0.0).astype(BF16), tri, preferred_element_type=F32)
    chosen = above | (tie & (tie_rank <= (cap - n_above)))
    cum = jnp.dot(jnp.where(chosen, 1.0, 0.0).astype(BF16), tri, preferred_element_type=F32)
    cum_ref[...] = cum.astype(I32).reshape(nb, ne, s)
    sel_ref[...] = jnp.where(chosen, cum - 1.0, -1.0).astype(I32).reshape(nb, ne, s)


def _route(lg_t, tri, cap):
    nb, ne, s = lg_t.shape
    ints = jax.ShapeDtypeStruct((nb, ne, s), I32)
    return pl.pallas_call(
        functools.partial(_route_body, cap=cap),
        out_shape=[jax.ShapeDtypeStruct((nb, ne, s), F32), ints, ints],
        compiler_params=pltpu.CompilerParams(vmem_limit_bytes=V7X_VMEM_LIMIT_BYTES),
        name="route",
    )(lg_t, tri)


TOK_BLK = 256
WIN = 64


def _window_plan(cum, cap):
    c_end = cum[:, :, TOK_BLK - 1::TOK_BLK]
    c_start = jnp.concatenate([jnp.zeros_like(c_end[:, :, :1]), c_end[:, :, :-1]], axis=-1)
    base = (c_start // BF16_SUBLANES) * BF16_SUBLANES
    passes = jnp.where(c_end > c_start, (c_end - base + WIN - 1) // WIN, 0)
    return jnp.swapaxes(base, 1, 2), jnp.max(passes, axis=1)


GATHER_DT = 1024


def _gather_body(base_s, npass_s, sel_ref, aff_ref, basec_ref, v_ref, xs_ref, gs_ref, *, cap):
    b = pl.program_id(0)
    ne, s = sel_ref.shape
    nk = s // TOK_BLK
    xs_ref[...] = jnp.zeros(xs_ref.shape, xs_ref.dtype)
    gs_ref[...] = jnp.zeros(gs_ref.shape, gs_ref.dtype)
    wiota = lax.broadcasted_iota(I32, (WIN, TOK_BLK), 0)
    passes = []
    for k in range(nk):
        toks = slice(k * TOK_BLK, (k + 1) * TOK_BLK)
        selk = sel_ref[:, toks]
        basek = basec_ref[k]

        def one_pass(p, carry, k=k, toks=toks, selk=selk, basek=basek):
            first = basek + p * WIN
            wbase = jnp.minimum(first, cap - WIN)
            rel = jnp.where(selk >= first, selk - wbase, -1)
            hits = [rel[e:e + 1, :] == wiota for e in range(ne)]
            onehot = jnp.concatenate([jnp.where(h, 1.0, 0.0) for h in hits], axis=0).astype(BF16)
            rows = jnp.dot(onehot, v_ref[toks, :], preferred_element_type=F32)
            offs = [pl.multiple_of(jnp.minimum(base_s[(b * nk + k) * ne + e] + p * WIN, cap - WIN), BF16_SUBLANES)
                    for e in range(ne)]
            for e in range(ne):
                xs_ref[e, pl.ds(offs[e], WIN), :] += rows[e * WIN:(e + 1) * WIN].astype(BF16)
                gate = jnp.sum(jnp.where(hits[e], aff_ref[e:e + 1, toks], 0.0), axis=1, keepdims=True)
                gs_ref[e, pl.ds(offs[e], WIN), :] += gate
            return carry

        passes.append(one_pass)

    for k in range(nk):
        passes[k](0, 0)
    for k in range(nk):
        lax.fori_loop(1, npass_s[b * nk + k], passes[k], 0)


def _gather(base, npass, sel, aff, v3, cap):
    nb, ne, s = sel.shape
    d = v3.shape[-1]
    nk = s // TOK_BLK
    grid_spec = pltpu.PrefetchScalarGridSpec(
        num_scalar_prefetch=2,
        grid=(nb, d // GATHER_DT),
        in_specs=[
            pl.BlockSpec((None, ne, s), lambda b, j, *_: (b, 0, 0)),
            pl.BlockSpec((None, ne, s), lambda b, j, *_: (b, 0, 0)),
            pl.BlockSpec((None, nk, ne, 1), lambda b, j, *_: (b, 0, 0, 0)),
            pl.BlockSpec((None, s, GATHER_DT), lambda b, j, *_: (b, 0, j)),
        ],
        out_specs=[
            pl.BlockSpec((ne, cap, GATHER_DT), lambda b, j, *_: (0, b, j)),
            pl.BlockSpec((ne, cap, 1), lambda b, j, *_: (0, b, 0)),
        ],
    )
    return pl.pallas_call(
        functools.partial(_gather_body, cap=cap),
        grid_spec=grid_spec,
        out_shape=[jax.ShapeDtypeStruct((ne, nb * cap, d), BF16),
                   jax.ShapeDtypeStruct((ne, nb * cap, 1), F32)],
        compiler_params=_params(("arbitrary", "arbitrary")),
        name="gather",
    )(base.reshape(-1), npass.reshape(-1), sel, aff, base[..., None], v3)


COMB_HSLOTS = 3


def _combine_body(base_s, npass_s, selc_ref, baser_ref, y_hbm, h_hbm, g_ref, o_ref, acc_ref, hbuf, ybuf, hsem, ysem,
                  *, cap, nb, nk):
    b = pl.program_id(0)
    k = pl.program_id(1)
    ne = ybuf.shape[1]
    per = ne // nk
    step = b * nk + k

    def h_copy(st):
        st = jnp.asarray(st, I32)
        rows = pl.ds(pl.multiple_of((st % nk) * TOK_BLK, TOK_BLK), TOK_BLK)
        slot = st % COMB_HSLOTS
        return pltpu.make_async_copy(h_hbm.at[st // nk, rows, :], hbuf.at[slot], hsem.at[slot])

    def y_copy(batch, j):
        batch = jnp.asarray(batch, I32)
        experts = pl.ds(j * per, per)
        rows = pl.ds(pl.multiple_of(batch * cap, cap), cap)
        return pltpu.make_async_copy(y_hbm.at[experts, rows, :], ybuf.at[batch % 2, experts], ysem.at[batch % 2, j])

    @pl.when(step == 0)
    def _prime():
        h_copy(0).start()
        h_copy(1).start()
        for j in range(nk):
            y_copy(0, j).start()

    @pl.when(step + 2 < nb * nk)
    def _():
        h_copy(step + 2).start()

    @pl.when(b + 1 < nb)
    def _():
        y_copy(b + 1, k).start()

    h_copy(step).wait()

    @pl.when(k == 0)
    def _():
        for j in range(nk):
            y_copy(b, j).wait()

    y_ref = ybuf.at[b % 2]
    acc_ref[...] = hbuf[step % COMB_HSLOTS]
    selk = selc_ref[...]
    basek = baser_ref[...]
    lane = lax.broadcasted_iota(I32, (ne, ne * WIN), 1)
    spread = jnp.where(lane // WIN == lax.broadcasted_iota(I32, (ne, ne * WIN), 0), 1.0, 0.0).astype(BF16)
    wlane = (lax.broadcasted_iota(I32, (1, ne * WIN), 1) % WIN).astype(F32)

    def one_pass(p, carry):
        first = basek + p * WIN
        wbase = jnp.minimum(first, cap - WIN)
        rel = jnp.where(selk >= first, selk - wbase, -1)
        relx = jnp.dot(rel.astype(F32).astype(BF16), spread, preferred_element_type=F32)
        onehot = jnp.where(relx == wlane, 1.0, 0.0).astype(BF16)
        wins = []
        for e in range(ne):
            off = jnp.minimum(base_s[(b * nk + k) * ne + e] + p * WIN, cap - WIN)
            wins.append(y_ref[e, pl.ds(pl.multiple_of(off, BF16_SUBLANES), WIN), :])
        acc_ref[...] += jnp.dot(onehot, jnp.concatenate(wins, axis=0), preferred_element_type=F32)
        return carry

    lax.fori_loop(0, npass_s[b * nk + k], one_pass, 0)
    h = acc_ref[...]
    ms = jnp.mean(h * h, axis=-1, keepdims=True)
    o_ref[...] = h * lax.rsqrt(ms + EPS) * g_ref[...]


def _combine(base, npass, sel_col, ys, h3, g_final, cap):
    nb, s, ne = sel_col.shape
    d = h3.shape[-1]
    nk = s // TOK_BLK
    grid_spec = pltpu.PrefetchScalarGridSpec(
        num_scalar_prefetch=2,
        grid=(nb, nk),
        in_specs=[
            pl.BlockSpec((None, TOK_BLK, ne), lambda b, k, *_: (b, k, 0)),
            pl.BlockSpec((None, None, 1, ne), lambda b, k, *_: (b, k, 0, 0)),
            pl.BlockSpec(memory_space=pl.ANY),
            pl.BlockSpec(memory_space=pl.ANY),
            pl.BlockSpec((1, d), lambda b, k, *_: (0, 0)),
        ],
        out_specs=pl.BlockSpec((None, TOK_BLK, d), lambda b, k, *_: (b, k, 0)),
        scratch_shapes=[
            pltpu.VMEM((TOK_BLK, d), F32),
            pltpu.VMEM((COMB_HSLOTS, TOK_BLK, d), F32),
            pltpu.VMEM((2, ne, cap, d), BF16),
            pltpu.SemaphoreType.DMA((COMB_HSLOTS,)),
            pltpu.SemaphoreType.DMA((2, nk)),
        ],
    )
    assert ne % nk == 0 and nb * nk >= 2
    return pl.pallas_call(
        functools.partial(_combine_body, cap=cap, nb=nb, nk=nk),
        grid_spec=grid_spec,
        out_shape=jax.ShapeDtypeStruct((nb, s, d), F32),
        compiler_params=_params(("arbitrary", "arbitrary")),
        name="combine",
    )(base.reshape(-1), npass.reshape(-1), sel_col, base[:, :, None, :], ys, h3, g_final)


FFN_TM = 512
FFN_STEPS = 8
FFN_NC = 512


def _ffn_body(xs_ref, gs_ref, wg_ref, wu_ref, wd_ref, y_ref, wgu_scr, wd_scr, hid_scr):
    e1 = pl.program_id(0)
    j = pl.program_id(1)
    ne = pl.num_programs(0) - 1
    ff = wg_ref.shape[1]
    nt = ff // LANES

    nc = y_ref.shape[1] // FFN_NC
    stage_slot = e1 % 2
    slot = (e1 - 1) % 2
    kr, fr = wg_ref.shape[0], wd_ref.shape[0]
    r0 = pl.multiple_of(j * kr, kr)
    f0 = pl.multiple_of(j * fr, BF16_SUBLANES)

    def stage_gate_up(t):
        src = slice(t * LANES, (t + 1) * LANES)
        wgu_scr[stage_slot, pl.ds(r0, kr), 2 * t * LANES:(2 * t + 1) * LANES] = wg_ref[:, src].astype(BF16)
        wgu_scr[stage_slot, pl.ds(r0, kr), (2 * t + 1) * LANES:(2 * t + 2) * LANES] = wu_ref[:, src].astype(BF16)

    def stage_down(c):
        cols = slice(c * FFN_NC, (c + 1) * FFN_NC)
        wd_scr[stage_slot, pl.ds(f0, fr), cols] = wd_ref[:, cols].astype(BF16)

    pieces = [functools.partial(stage_gate_up, t) for t in range(nt)] + [
        functools.partial(stage_down, c) for c in range(nc)]

    def spread(n_chunks):
        return [pieces[i::n_chunks] for i in range(n_chunks)]

    @pl.when(e1 == 0)
    def _load_first_expert():
        for piece in pieces:
            piece()

    @pl.when((e1 > 0) & (j % 2 == 0))
    def _gate_up():
        xs = xs_ref[...]
        for t, todo in enumerate(spread(nt)):
            for piece in todo:
                piece()
            res = jnp.dot(xs, wgu_scr[slot, :, 2 * t * LANES:(2 * t + 2) * LANES], preferred_element_type=F32)
            hid_scr[:, t * LANES:(t + 1) * LANES] = (jax.nn.silu(res[:, :LANES]) * res[:, LANES:]).astype(BF16)

    @pl.when((e1 > 0) & (j % 2 == 1))
    def _down():
        hid = hid_scr[...]
        for c, todo in enumerate(spread(nc)):
            for piece in todo:
                piece()
            cols = slice(c * FFN_NC, (c + 1) * FFN_NC)
            y = jnp.dot(hid, wd_scr[slot, :, cols], preferred_element_type=F32)
            y_ref[:, cols] = (y * gs_ref[...]).astype(BF16)


def _ffn(xs, gs, w_gate_e, w_up_e, w_down_e):
    ne, m, d = xs.shape
    ff = w_gate_e.shape[-1]
    assert m == FFN_TM * FFN_STEPS // 2 and d % FFN_STEPS == 0 and ff % (FFN_STEPS * BF16_SUBLANES) == 0
    assert ff % LANES == 0
    prev = lambda e1, j: (jnp.maximum(e1 - 1, 0), jnp.where(e1 == 0, 0, j // 2), 0)
    nxt = lambda e1, j: (jnp.minimum(e1, ne - 1), j, 0)
    return pl.pallas_call(
        _ffn_body,
        grid=(ne + 1, FFN_STEPS),
        in_specs=[
            pl.BlockSpec((None, FFN_TM, d), prev),
            pl.BlockSpec((None, FFN_TM, 1), prev),
            pl.BlockSpec((None, d // FFN_STEPS, ff), nxt),
            pl.BlockSpec((None, d // FFN_STEPS, ff), nxt),
            pl.BlockSpec((None, ff // FFN_STEPS, d), nxt),
        ],
        out_specs=pl.BlockSpec((None, FFN_TM, d), prev),
        out_shape=jax.ShapeDtypeStruct((ne, m, d), BF16),
        scratch_shapes=[pltpu.VMEM((2, d, 2 * ff), BF16), pltpu.VMEM((2, ff, d), BF16),
                        pltpu.VMEM((FFN_TM, ff), BF16)],
        compiler_params=_params(("arbitrary", "arbitrary")),
        name="ffn",
    )(xs, gs, w_gate_e, w_up_e, w_down_e)


TRIG_SPLIT = 32


def _trig_rows(n, rows, cols):
    ang = ((rows[:, None] * cols[None, :]) % n).astype(F32) * (2.0 * math.pi / n)
    return jnp.cos(ang), jnp.sin(ang)


def _dft_tables(n, rows, cols):
    nr = rows.shape[0]
    r1 = lax.iota(I32, nr // TRIG_SPLIT) * TRIG_SPLIT
    r0 = lax.iota(I32, TRIG_SPLIT)
    c1, s1 = _trig_rows(n, r1, cols)
    c0, s0 = _trig_rows(n, r0, cols)
    scale = 1.0 / math.sqrt(n)
    c1, s1 = (c1 * scale)[:, None, :], (s1 * scale)[:, None, :]
    c0, s0 = c0[None], s0[None]
    cos = (c1 * c0 - s1 * s0).reshape(nr, -1)
    sin = (s1 * c0 + c1 * s0).reshape(nr, -1)
    return cos, sin


def _pool_tables(s):
    i = lax.iota(I32, s)[:, None]
    k = (i // POOL_TM) * POOL_TM - POOL_HALO + lax.iota(I32, POOL_BAND)[None, :]
    out = []
    for half in POOL_HALF:
        lo = jnp.clip(i - half, 0, s)
        hi = jnp.clip(i + half, 0, s)
        inside = (k >= lo) & (k < hi)
        cnt = (hi - lo).astype(F32)
        out.append((jnp.where(inside, 1.0 / cnt, 0.0) - jnp.where(i == k, 1.0, 0.0)).astype(BF16))
    return jnp.stack(out)


def kernel(x, norm_mix_g, w_in, w_fourier_mix, w_pool_mix, pool_scale, w_branch_f, w_branch_p, w_gate,
           b_gate, w_out, norm_moe_g, w_router, w_expert_gate, w_expert_up, w_expert_down, norm_final_g):
    nb, s, d = x.shape
    assert w_in.shape[0] == 1, "single-layer block only"
    assert s % (2 * TOK_BLK) == 0 and max(POOL_HALF) <= POOL_HALO
    cap = CAPACITY_FACTOR * s // N_EXPERTS
    t = nb * s

    freqs = lax.iota(I32, s // DFT_RADIX)
    tabs = []
    for r in range(DFT_RADIX):
        cos_r, sin_r = _dft_tables(s, freqs, DFT_RADIX * freqs + r)
        tabs += [cos_r.astype(BF16), sin_r.astype(BF16)]
    trig = jnp.stack(tabs)
    chan = lax.iota(I32, GROUP)
    cos_c, sin_c = _trig_rows(GROUP, chan, chan)
    cos_c = (cos_c / math.sqrt(GROUP)).astype(BF16)
    sin_c = (sin_c / math.sqrt(GROUP)).astype(BF16)
    tri = (lax.iota(I32, s)[:, None] <= lax.iota(I32, s)[None, :]).astype(BF16)

    x2 = x.reshape(t, d)
    g_mix = norm_mix_g[0][:, None]
    pf, pp, gates = _proj(x2, (g_mix * w_in[0]).astype(BF16), (g_mix * w_gate[0]).astype(BF16), b_gate[0][None])
    pc, ps, pooled = _seqmix(trig, _pool_tables(s), pf.reshape(nb, s // DFT_RADIX, -1), pp.reshape(nb, s, -1), s)
    wr = norm_moe_g[0][:, None] * w_router[0]
    wr_hi = wr.astype(BF16)
    wr_lo = (wr - wr_hi.astype(F32)).astype(BF16)
    h2, v, logits = _tail(
        pc.reshape(t, -1), ps.reshape(t, -1), pooled.reshape(t, -1), gates, x2, cos_c, sin_c,
        w_fourier_mix[0].astype(BF16), w_pool_mix[0].astype(BF16), pool_scale[0][None],
        w_branch_f[0].astype(BF16), w_branch_p[0].astype(BF16), w_out[0].astype(BF16),
        norm_moe_g[0][None], jnp.concatenate([wr_hi, wr_lo], axis=1))
    lg_t = jnp.swapaxes(logits.reshape(nb, s, N_EXPERTS), 1, 2)
    aff, sel, cum = _route(lg_t, tri, cap)
    base, npass = _window_plan(cum, cap)
    xs, gs = _gather(base, npass, sel, aff, v.reshape(nb, s, d), cap)
    ys = _ffn(xs, gs, w_expert_gate[0], w_expert_up[0], w_expert_down[0])
    return _combine(base, npass, jnp.swapaxes(sel, 1, 2), ys, h2.reshape(nb, s, d), norm_final_g[None], cap)
```

```python
import functools
import math

import jax
import jax.numpy as jnp
from jax import lax
from jax.experimental import pallas as pl
from jax.experimental.pallas import tpu as pltpu

F32 = jnp.float32
BF16 = jnp.bfloat16
I32 = jnp.int32

D_MODEL = 2048
FOURIER_WIDTH = 1024
N_GROUPS = 4
GROUP = 256
POOL_HALF = (1, 2, 4, 8)
N_EXPERTS = 16
EXPERT_FF = 1408
CAPACITY_FACTOR = 2
EPS = 1e-6

V7X_VMEM_LIMIT_BYTES = 58 * 1024 * 1024
BF16_SUBLANES = 16
LANES = 128


def _params(sem, vmem=V7X_VMEM_LIMIT_BYTES):
    return pltpu.CompilerParams(dimension_semantics=sem, vmem_limit_bytes=vmem)


def _const_spec(shape):
    nd = len(shape)
    return pl.BlockSpec(shape, lambda *_: (0,) * nd, pipeline_mode=pl.Buffered(1))


PROJ_TM = 512
PROJ_NC = 512
DFT_RADIX = 4


def _proj_body(x_ref, win_ref, wg_ref, bg_ref, *refs):
    n_later = (len(refs) - 4) // 2
    later_in, (pf_ref, pp_ref, gate_ref) = refs[:n_later], refs[n_later:n_later + 3]
    later_out, split_ref = refs[n_later + 3:n_later + 3 + n_later], refs[-1]
    for src, dst in zip(later_in, later_out):
        dst[...] = src[...].astype(BF16)
    x = x_ref[...]
    xb = x.astype(BF16)
    inv = lax.rsqrt(jnp.mean(x * x, axis=-1, keepdims=True) + EPS)
    part = PROJ_TM // DFT_RADIX
    for j in range(FOURIER_WIDTH // PROJ_NC):
        res = jnp.dot(xb, win_ref[:, j * PROJ_NC:(j + 1) * PROJ_NC], preferred_element_type=F32) * inv
        for c in range(PROJ_NC // LANES):
            col = j * PROJ_NC + c * LANES
            split_ref[c] = res[:, c * LANES:(c + 1) * LANES]
            for r in range(DFT_RADIX):
                pf_ref[:, r * FOURIER_WIDTH + col:r * FOURIER_WIDTH + col + LANES] = (
                    split_ref[c, pl.ds(r, part, stride=DFT_RADIX), :].astype(BF16))
    for j in range((win_ref.shape[1] - FOURIER_WIDTH) // PROJ_NC):
        sl = slice(FOURIER_WIDTH + j * PROJ_NC, FOURIER_WIDTH + (j + 1) * PROJ_NC)
        res = jnp.dot(xb, win_ref[:, sl], preferred_element_type=F32) * inv
        pp_ref[:, j * PROJ_NC:(j + 1) * PROJ_NC] = res.astype(BF16)
    for j in range(wg_ref.shape[1] // PROJ_NC):
        sl = slice(j * PROJ_NC, (j + 1) * PROJ_NC)
        a = jnp.dot(xb, wg_ref[:, sl], preferred_element_type=F32) * inv + bg_ref[:, sl]
        gate_ref[:, sl] = jax.nn.sigmoid(a).astype(BF16)


def _proj(x2, w_in, w_gate, b_gate, later_weights):
    t, d = x2.shape
    dm, dg = w_in.shape[1], w_gate.shape[1]
    steps = t // PROJ_TM
    slices = [pl.BlockSpec((w.shape[0] // steps, w.shape[1]), lambda i: (i, 0)) for w in later_weights]
    assert all(w.shape[0] % (steps * BF16_SUBLANES) == 0 for w in later_weights)
    return pl.pallas_call(
        _proj_body,
        grid=(steps,),
        in_specs=[
            pl.BlockSpec((PROJ_TM, d), lambda i: (i, 0)),
            _const_spec((d, dm)),
            _const_spec((d, dg)),
            _const_spec((1, dg)),
        ] + slices,
        out_specs=[
            pl.BlockSpec((PROJ_TM // DFT_RADIX, DFT_RADIX * FOURIER_WIDTH), lambda i: (i, 0)),
            pl.BlockSpec((PROJ_TM, dm - FOURIER_WIDTH), lambda i: (i, 0)),
            pl.BlockSpec((PROJ_TM, dg), lambda i: (i, 0)),
        ] + slices,
        out_shape=[jax.ShapeDtypeStruct((t // DFT_RADIX, DFT_RADIX * FOURIER_WIDTH), BF16),
                   jax.ShapeDtypeStruct((t, dm - FOURIER_WIDTH), BF16),
                   jax.ShapeDtypeStruct((t, dg), BF16)]
        + [jax.ShapeDtypeStruct(w.shape, BF16) for w in later_weights],
        scratch_shapes=[pltpu.VMEM((PROJ_NC // LANES, PROJ_TM, LANES), F32)],
        compiler_params=_params(("arbitrary",)),
        name="proj",
    )(x2, w_in, w_gate, b_gate, *later_weights)


DFT_MC = 256
DFT_NC = 512


def _fourier_body(trig_ref, *refs):
    p_refs, (pc_ref, ps_ref) = refs[:DFT_RADIX], refs[DFT_RADIX:]
    part = trig_ref.shape[1]
    p_in = [r[...] for r in p_refs]
    for i in range(part // DFT_MC):
        rows = slice(i * DFT_MC, (i + 1) * DFT_MC)
        c = [jnp.dot(trig_ref[2 * r, rows, :], p_in[r], preferred_element_type=F32) for r in range(DFT_RADIX)]
        s = [jnp.dot(trig_ref[2 * r + 1, rows, :], p_in[r], preferred_element_type=F32) for r in range(DFT_RADIX)]
        ec, oc, es, os_ = c[0] + c[2], c[0] - c[2], s[0] + s[2], s[0] - s[2]
        fc, gc, fs, gs = c[1] + c[3], c[1] - c[3], s[1] + s[3], s[1] - s[3]
        xc = (ec + fc, oc - gs, ec - fc, oc + gs)
        xs = (es + fs, os_ + gc, es - fs, os_ - gc)
        for q in range(DFT_RADIX):
            out_rows = slice(q * part + i * DFT_MC, q * part + (i + 1) * DFT_MC)
            pc_ref[out_rows, :] = xc[q].astype(BF16)
            ps_ref[out_rows, :] = xs[q].astype(BF16)


POOL_TM = 256
POOL_HALO = 128


POOL_BAND = POOL_TM + 2 * POOL_HALO


def _pool_body(a_ref, p_ref, o_ref, first_group):
    s = a_ref.shape[1]
    for g in range(p_ref.shape[1] // GROUP):
        cols = slice(g * GROUP, (g + 1) * GROUP)
        for i in range(s // POOL_TM):
            rows = slice(i * POOL_TM, (i + 1) * POOL_TM)
            lo, hi = i * POOL_TM - POOL_HALO, (i + 1) * POOL_TM + POOL_HALO
            band = slice(max(-lo, 0), POOL_BAND - max(hi - s, 0))
            src = slice(max(lo, 0), min(hi, s))
            o_ref[rows, cols] = jnp.dot(a_ref[first_group + g, rows, band], p_ref[src, cols],
                                        preferred_element_type=F32).astype(BF16)


def _seqmix_body(trig_ref, bands_ref, *refs):
    p_refs, pp_ref = refs[:DFT_RADIX], refs[DFT_RADIX]
    pc_ref, ps_ref, po_ref = refs[DFT_RADIX + 1:]
    _fourier_body(trig_ref, *p_refs, pc_ref, ps_ref)
    _pool_body(bands_ref, pp_ref, po_ref, pl.program_id(1) * (DFT_NC // GROUP))


def _seqmix(trig, pool_bands, p_split, pp3, s):
    assert DFT_RADIX == 4 and pp3.shape[-1] == FOURIER_WIDTH
    b, part, _ = p_split.shape
    per_res = FOURIER_WIDTH // DFT_NC
    out = jax.ShapeDtypeStruct((b, s, FOURIER_WIDTH), BF16)
    res_spec = lambda r: pl.BlockSpec((None, part, DFT_NC), lambda i, g: (i, 0, r * per_res + g))
    cols = pl.BlockSpec((None, s, DFT_NC), lambda i, g: (i, 0, g))
    return pl.pallas_call(
        _seqmix_body,
        grid=(b, per_res),
        in_specs=[_const_spec(trig.shape), _const_spec(pool_bands.shape)]
        + [res_spec(r) for r in range(DFT_RADIX)] + [cols],
        out_specs=[cols] * 3,
        out_shape=[out, out, out],
        compiler_params=_params(("arbitrary", "arbitrary")),
        name="seqmix",
    )(trig, pool_bands, *([p_split] * DFT_RADIX), pp3)


TAIL_TM = 256


def _tail_body(pc_ref, ps_ref, pl_ref, gate_ref, x_ref, cc_ref, sc_ref, wf_ref, wp_ref, psc_ref, wbf_ref, wbp_ref,
               wout_ref, gm_ref, wr_ref, h_ref, v_ref, lg_ref):
    yf, yp = [], []
    for g in range(N_GROUPS):
        cols = slice(g * GROUP, (g + 1) * GROUP)
        z = (jnp.dot(pc_ref[:, cols], cc_ref[...], preferred_element_type=F32)
             - jnp.dot(ps_ref[:, cols], sc_ref[...], preferred_element_type=F32))
        yf.append(jnp.dot(z.astype(BF16), wf_ref[g], preferred_element_type=F32).astype(BF16))
        ypg = jnp.dot(pl_ref[:, cols], wp_ref[g], preferred_element_type=F32) * psc_ref[:, cols]
        yp.append(ypg.astype(BF16))
    yf = jnp.concatenate(yf, axis=-1)
    yp = jnp.concatenate(yp, axis=-1)
    d = wout_ref.shape[0]
    bf = jnp.dot(yf, wbf_ref[...], preferred_element_type=F32)
    bp = jnp.dot(yp, wbp_ref[...], preferred_element_type=F32)
    merged = gate_ref[:, :d].astype(F32) * bf + gate_ref[:, d:].astype(F32) * bp
    h = x_ref[...] + jnp.dot(merged.astype(BF16), wout_ref[...], preferred_element_type=F32)
    h_ref[...] = h
    inv = lax.rsqrt(jnp.mean(h * h, axis=-1, keepdims=True) + EPS)
    v_ref[...] = (h * inv * gm_ref[...]).astype(BF16)
    acc = jnp.dot(h.astype(BF16), wr_ref[...], preferred_element_type=F32)
    lg_ref[...] = (acc[:, :N_EXPERTS] + acc[:, N_EXPERTS:]) * inv


def _tail(pc2, ps2, pl2, gates, x2, cc, sc, wf, wp, pscale, wbf, wbp, wout, gm, wr2):
    t, d = x2.shape
    row = lambda width: pl.BlockSpec((TAIL_TM, width), lambda i: (i, 0))
    return pl.pallas_call(
        _tail_body,
        grid=(t // TAIL_TM,),
        in_specs=[
            row(pc2.shape[1]), row(ps2.shape[1]), row(pl2.shape[1]), row(gates.shape[1]), row(d),
            _const_spec(cc.shape), _const_spec(sc.shape), _const_spec(wf.shape), _const_spec(wp.shape),
            _const_spec(pscale.shape), _const_spec(wbf.shape), _const_spec(wbp.shape),
            _const_spec(wout.shape), _const_spec(gm.shape), _const_spec(wr2.shape),
        ],
        out_specs=[row(d), row(d), row(N_EXPERTS)],
        out_shape=[jax.ShapeDtypeStruct((t, d), F32), jax.ShapeDtypeStruct((t, d), BF16),
                   jax.ShapeDtypeStruct((t, N_EXPERTS), F32)],
        compiler_params=_params(("parallel",)),
        name="tail",
    )(pc2, ps2, pl2, gates, x2, cc, sc, wf, wp, pscale, wbf, wbp, wout, gm, wr2)


def _route_body(lg_ref, aff_ref, sel_ref, cum_ref, *, cap):
    nb, ne, s = lg_ref.shape
    lg = lg_ref[...]
    mx = jnp.max(lg, axis=1, keepdims=True)
    ex = jnp.exp(lg - mx)
    aff = ex / jnp.sum(ex, axis=1, keepdims=True)
    aff_ref[...] = aff
    aff2 = aff.reshape(nb * ne, s)

    def step(i, bits):
        cand = bits | jnp.left_shift(jnp.int32(1), 30 - i)
        n_ge = jnp.sum(jnp.where(aff2 >= pltpu.bitcast(cand, F32), 1.0, 0.0), axis=1, keepdims=True)
        return jnp.where(n_ge >= cap, cand, bits)

    thr = pltpu.bitcast(lax.fori_loop(0, 31, step, jnp.zeros((nb * ne, 1), I32)), F32)
    above = aff2 > thr
    tie = aff2 == thr
    n_above = jnp.sum(jnp.where(above, 1.0, 0.0), axis=1, keepdims=True)
    tri = jnp.where(lax.broadcasted_iota(I32, (s, s), 0) <= lax.broadcasted_iota(I32, (s, s), 1), 1.0, 0.0).astype(BF16)
    tie_rank = jnp.dot(jnp.where(tie, 1.0, 0.0).astype(BF16), tri, preferred_element_type=F32)
    chosen = above | (tie & (tie_rank <= (cap - n_above)))
    cum = jnp.dot(jnp.where(chosen, 1.0, 0.0).astype(BF16), tri, preferred_element_type=F32)
    cum_ref[...] = cum.astype(I32).reshape(nb, ne, s)
    sel_ref[...] = jnp.where(chosen, cum - 1.0, -1.0).astype(I32).reshape(nb, ne, s)


def _route(lg_t, cap):
    nb, ne, s = lg_t.shape
    ints = jax.ShapeDtypeStruct((nb, ne, s), I32)
    return pl.pallas_call(
        functools.partial(_route_body, cap=cap),
        out_shape=[jax.ShapeDtypeStruct((nb, ne, s), F32), ints, ints],
        compiler_params=pltpu.CompilerParams(vmem_limit_bytes=V7X_VMEM_LIMIT_BYTES),
        name="route",
    )(lg_t)


TOK_BLK = 256
WIN = 64


def _window_plan(cum, cap):
    c_end = cum[:, :, TOK_BLK - 1::TOK_BLK]
    c_start = jnp.concatenate([jnp.zeros_like(c_end[:, :, :1]), c_end[:, :, :-1]], axis=-1)
    base = (c_start // BF16_SUBLANES) * BF16_SUBLANES
    passes = jnp.where(c_end > c_start, (c_end - base + WIN - 1) // WIN, 0)
    return jnp.swapaxes(base, 1, 2), jnp.max(passes, axis=1)


GATHER_DT = 1024


def _gather_body(base_s, npass_s, sel_ref, aff_ref, basec_ref, v_ref, xs_ref, gs_ref, *, cap):
    b = pl.program_id(0)
    ne, s = sel_ref.shape
    nk = s // TOK_BLK
    xs_ref[...] = jnp.zeros(xs_ref.shape, xs_ref.dtype)
    gs_ref[...] = jnp.zeros(gs_ref.shape, gs_ref.dtype)
    wiota = lax.broadcasted_iota(I32, (WIN, TOK_BLK), 0)
    passes = []
    for k in range(nk):
        toks = slice(k * TOK_BLK, (k + 1) * TOK_BLK)
        selk = sel_ref[:, toks]
        basek = basec_ref[k]

        def one_pass(p, carry, k=k, toks=toks, selk=selk, basek=basek):
            first = basek + p * WIN
            wbase = jnp.minimum(first, cap - WIN)
            rel = jnp.where(selk >= first, selk - wbase, -1)
            hits = [rel[e:e + 1, :] == wiota for e in range(ne)]
            onehot = jnp.concatenate([jnp.where(h, 1.0, 0.0) for h in hits], axis=0).astype(BF16)
            rows = jnp.dot(onehot, v_ref[toks, :], preferred_element_type=F32)
            offs = [pl.multiple_of(jnp.minimum(base_s[(b * nk + k) * ne + e] + p * WIN, cap - WIN), BF16_SUBLANES)
                    for e in range(ne)]
            for e in range(ne):
                xs_ref[e, pl.ds(offs[e], WIN), :] += rows[e * WIN:(e + 1) * WIN].astype(BF16)
                gate = jnp.sum(jnp.where(hits[e], aff_ref[e:e + 1, toks], 0.0), axis=1, keepdims=True)
                gs_ref[e, pl.ds(offs[e], WIN), :] += gate
            return carry

        passes.append(one_pass)

    for k in range(nk):
        passes[k](0, 0)
    for k in range(nk):
        lax.fori_loop(1, npass_s[b * nk + k], passes[k], 0)


def _gather(base, npass, sel, aff, v3, cap):
    nb, ne, s = sel.shape
    d = v3.shape[-1]
    nk = s // TOK_BLK
    grid_spec = pltpu.PrefetchScalarGridSpec(
        num_scalar_prefetch=2,
        grid=(nb, d // GATHER_DT),
        in_specs=[
            pl.BlockSpec((None, ne, s), lambda b, j, *_: (b, 0, 0)),
            pl.BlockSpec((None, ne, s), lambda b, j, *_: (b, 0, 0)),
            pl.BlockSpec((None, nk, ne, 1), lambda b, j, *_: (b, 0, 0, 0)),
            pl.BlockSpec((None, s, GATHER_DT), lambda b, j, *_: (b, 0, j)),
        ],
        out_specs=[
            pl.BlockSpec((ne, cap, GATHER_DT), lambda b, j, *_: (0, b, j)),
            pl.BlockSpec((ne, cap, 1), lambda b, j, *_: (0, b, 0)),
        ],
    )
    return pl.pallas_call(
        functools.partial(_gather_body, cap=cap),
        grid_spec=grid_spec,
        out_shape=[jax.ShapeDtypeStruct((ne, nb * cap, d), BF16),
                   jax.ShapeDtypeStruct((ne, nb * cap, 1), F32)],
        compiler_params=_params(("arbitrary", "arbitrary")),
        name="gather",
    )(base.reshape(-1), npass.reshape(-1), sel, aff, base[..., None], v3)


COMB_HSLOTS = 3


def _combine_body(base_s, npass_s, selc_ref, baser_ref, y_hbm, h_hbm, g_ref, o_ref, acc_ref, hbuf, ybuf, hsem, ysem,
                  *, cap, nb, nk):
    b = pl.program_id(0)
    k = pl.program_id(1)
    ne = ybuf.shape[1]
    per = ne // nk
    step = b * nk + k

    def h_copy(st):
        st = jnp.asarray(st, I32)
        rows = pl.ds(pl.multiple_of((st % nk) * TOK_BLK, TOK_BLK), TOK_BLK)
        slot = st % COMB_HSLOTS
        return pltpu.make_async_copy(h_hbm.at[st // nk, rows, :], hbuf.at[slot], hsem.at[slot])

    def y_copy(batch, j):
        batch = jnp.asarray(batch, I32)
        experts = pl.ds(j * per, per)
        rows = pl.ds(pl.multiple_of(batch * cap, cap), cap)
        return pltpu.make_async_copy(y_hbm.at[experts, rows, :], ybuf.at[batch % 2, experts], ysem.at[batch % 2, j])

    @pl.when(step == 0)
    def _prime():
        h_copy(0).start()
        h_copy(1).start()
        for j in range(nk):
            y_copy(0, j).start()

    @pl.when(step + 2 < nb * nk)
    def _():
        h_copy(step + 2).start()

    @pl.when(b + 1 < nb)
    def _():
        y_copy(b + 1, k).start()

    h_copy(step).wait()

    @pl.when(k == 0)
    def _():
        for j in range(nk):
            y_copy(b, j).wait()

    y_ref = ybuf.at[b % 2]
    acc_ref[...] = hbuf[step % COMB_HSLOTS]
    selk = selc_ref[...]
    basek = baser_ref[...]
    lane = lax.broadcasted_iota(I32, (ne, ne * WIN), 1)
    spread = jnp.where(lane // WIN == lax.broadcasted_iota(I32, (ne, ne * WIN), 0), 1.0, 0.0).astype(BF16)
    wlane = (lax.broadcasted_iota(I32, (1, ne * WIN), 1) % WIN).astype(F32)

    def one_pass(p, carry):
        first = basek + p * WIN
        wbase = jnp.minimum(first, cap - WIN)
        rel = jnp.where(selk >= first, selk - wbase, -1)
        relx = jnp.dot(rel.astype(F32).astype(BF16), spread, preferred_element_type=F32)
        onehot = jnp.where(relx == wlane, 1.0, 0.0).astype(BF16)
        wins = []
        for e in range(ne):
            off = jnp.minimum(base_s[(b * nk + k) * ne + e] + p * WIN, cap - WIN)
            wins.append(y_ref[e, pl.ds(pl.multiple_of(off, BF16_SUBLANES), WIN), :])
        acc_ref[...] += jnp.dot(onehot, jnp.concatenate(wins, axis=0), preferred_element_type=F32)
        return carry

    lax.fori_loop(0, npass_s[b * nk + k], one_pass, 0)
    h = acc_ref[...]
    ms = jnp.mean(h * h, axis=-1, keepdims=True)
    o_ref[...] = h * lax.rsqrt(ms + EPS) * g_ref[...]


def _combine(base, npass, sel_col, ys, h3, g_final, cap):
    nb, s, ne = sel_col.shape
    d = h3.shape[-1]
    nk = s // TOK_BLK
    grid_spec = pltpu.PrefetchScalarGridSpec(
        num_scalar_prefetch=2,
        grid=(nb, nk),
        in_specs=[
            pl.BlockSpec((None, TOK_BLK, ne), lambda b, k, *_: (b, k, 0)),
            pl.BlockSpec((None, None, 1, ne), lambda b, k, *_: (b, k, 0, 0)),
            pl.BlockSpec(memory_space=pl.ANY),
            pl.BlockSpec(memory_space=pl.ANY),
            pl.BlockSpec((1, d), lambda b, k, *_: (0, 0)),
        ],
        out_specs=pl.BlockSpec((None, TOK_BLK, d), lambda b, k, *_: (b, k, 0)),
        scratch_shapes=[
            pltpu.VMEM((TOK_BLK, d), F32),
            pltpu.VMEM((COMB_HSLOTS, TOK_BLK, d), F32),
            pltpu.VMEM((2, ne, cap, d), BF16),
            pltpu.SemaphoreType.DMA((COMB_HSLOTS,)),
            pltpu.SemaphoreType.DMA((2, nk)),
        ],
    )
    assert ne % nk == 0 and nb * nk >= 2
    return pl.pallas_call(
        functools.partial(_combine_body, cap=cap, nb=nb, nk=nk),
        grid_spec=grid_spec,
        out_shape=jax.ShapeDtypeStruct((nb, s, d), F32),
        compiler_params=_params(("arbitrary", "arbitrary")),
        name="combine",
    )(base.reshape(-1), npass.reshape(-1), sel_col, base[:, :, None, :], ys, h3, g_final)


FFN_TM = 512
FFN_STEPS = 8
FFN_NC = 512


def _ffn_body(xs_ref, gs_ref, wg_ref, wu_ref, wd_ref, y_ref, wgu_scr, wd_scr, hid_scr):
    e1 = pl.program_id(0)
    j = pl.program_id(1)
    ff = wg_ref.shape[1]
    nt = ff // LANES

    nc = y_ref.shape[1] // FFN_NC
    stage_slot = e1 % 2
    slot = (e1 - 1) % 2
    kr, fr = wg_ref.shape[0], wd_ref.shape[0]
    r0 = pl.multiple_of(j * kr, kr)
    f0 = pl.multiple_of(j * fr, BF16_SUBLANES)

    def stage_gate_up(t):
        src = slice(t * LANES, (t + 1) * LANES)
        wgu_scr[stage_slot, pl.ds(r0, kr), 2 * t * LANES:(2 * t + 1) * LANES] = wg_ref[:, src].astype(BF16)
        wgu_scr[stage_slot, pl.ds(r0, kr), (2 * t + 1) * LANES:(2 * t + 2) * LANES] = wu_ref[:, src].astype(BF16)

    def stage_down(c):
        cols = slice(c * FFN_NC, (c + 1) * FFN_NC)
        wd_scr[stage_slot, pl.ds(f0, fr), cols] = wd_ref[:, cols].astype(BF16)

    pieces = [functools.partial(stage_gate_up, t) for t in range(nt)] + [
        functools.partial(stage_down, c) for c in range(nc)]

    def spread(n_chunks):
        return [pieces[i::n_chunks] for i in range(n_chunks)]

    @pl.when(e1 == 0)
    def _load_first_expert():
        for piece in pieces:
            piece()

    @pl.when((e1 > 0) & (j % 2 == 0))
    def _gate_up():
        xs = xs_ref[...]
        for t, todo in enumerate(spread(nt)):
            for piece in todo:
                piece()
            res = jnp.dot(xs, wgu_scr[slot, :, 2 * t * LANES:(2 * t + 2) * LANES], preferred_element_type=F32)
            hid_scr[:, t * LANES:(t + 1) * LANES] = (jax.nn.silu(res[:, :LANES]) * res[:, LANES:]).astype(BF16)

    @pl.when((e1 > 0) & (j % 2 == 1))
    def _down():
        hid = hid_scr[...]
        for c, todo in enumerate(spread(nc)):
            for piece in todo:
                piece()
            cols = slice(c * FFN_NC, (c + 1) * FFN_NC)
            y = jnp.dot(hid, wd_scr[slot, :, cols], preferred_element_type=F32)
            y_ref[:, cols] = (y * gs_ref[...]).astype(BF16)


def _ffn(xs, gs, w_gate_e, w_up_e, w_down_e):
    ne, m, d = xs.shape
    ff = w_gate_e.shape[-1]
    assert m == FFN_TM * FFN_STEPS // 2 and d % FFN_STEPS == 0 and ff % (FFN_STEPS * BF16_SUBLANES) == 0
    assert ff % LANES == 0
    prev = lambda e1, j: (jnp.maximum(e1 - 1, 0), jnp.where(e1 == 0, 0, j // 2), 0)
    nxt = lambda e1, j: (jnp.minimum(e1, ne - 1), j, 0)
    return pl.pallas_call(
        _ffn_body,
        grid=(ne + 1, FFN_STEPS),
        in_specs=[
            pl.BlockSpec((None, FFN_TM, d), prev),
            pl.BlockSpec((None, FFN_TM, 1), prev),
            pl.BlockSpec((None, d // FFN_STEPS, ff), nxt),
            pl.BlockSpec((None, d // FFN_STEPS, ff), nxt),
            pl.BlockSpec((None, ff // FFN_STEPS, d), nxt),
        ],
        out_specs=pl.BlockSpec((None, FFN_TM, d), prev),
        out_shape=jax.ShapeDtypeStruct((ne, m, d), BF16),
        scratch_shapes=[pltpu.VMEM((2, d, 2 * ff), BF16), pltpu.VMEM((2, ff, d), BF16),
                        pltpu.VMEM((FFN_TM, ff), BF16)],
        compiler_params=_params(("arbitrary", "arbitrary")),
        name="ffn",
    )(xs, gs, w_gate_e, w_up_e, w_down_e)


TRIG_SPLIT = 32


def _trig_rows(n, rows, cols):
    ang = ((rows[:, None] * cols[None, :]) % n).astype(F32) * (2.0 * math.pi / n)
    return jnp.cos(ang), jnp.sin(ang)


def _dft_tables(n, rows, cols):
    nr = rows.shape[0]
    r1 = lax.iota(I32, nr // TRIG_SPLIT) * TRIG_SPLIT
    r0 = lax.iota(I32, TRIG_SPLIT)
    c1, s1 = _trig_rows(n, r1, cols)
    c0, s0 = _trig_rows(n, r0, cols)
    scale = 1.0 / math.sqrt(n)
    c1, s1 = (c1 * scale)[:, None, :], (s1 * scale)[:, None, :]
    c0, s0 = c0[None], s0[None]
    cos = (c1 * c0 - s1 * s0).reshape(nr, -1)
    sin = (s1 * c0 + c1 * s0).reshape(nr, -1)
    return cos, sin


def _pool_tables(s):
    i = lax.iota(I32, s)[:, None]
    k = (i // POOL_TM) * POOL_TM - POOL_HALO + lax.iota(I32, POOL_BAND)[None, :]
    out = []
    for half in POOL_HALF:
        lo = jnp.clip(i - half, 0, s)
        hi = jnp.clip(i + half, 0, s)
        inside = (k >= lo) & (k < hi)
        cnt = (hi - lo).astype(F32)
        out.append((jnp.where(inside, 1.0 / cnt, 0.0) - jnp.where(i == k, 1.0, 0.0)).astype(BF16))
    return jnp.stack(out)


def kernel(x, norm_mix_g, w_in, w_fourier_mix, w_pool_mix, pool_scale, w_branch_f, w_branch_p, w_gate,
           b_gate, w_out, norm_moe_g, w_router, w_expert_gate, w_expert_up, w_expert_down, norm_final_g):
    nb, s, d = x.shape
    assert w_in.shape[0] == 1, "single-layer block only"
    assert s % (2 * TOK_BLK) == 0 and max(POOL_HALF) <= POOL_HALO
    cap = CAPACITY_FACTOR * s // N_EXPERTS
    t = nb * s

    freqs = lax.iota(I32, s // DFT_RADIX)
    tabs = []
    for r in range(DFT_RADIX):
        cos_r, sin_r = _dft_tables(s, freqs, DFT_RADIX * freqs + r)
        tabs += [cos_r.astype(BF16), sin_r.astype(BF16)]
    trig = jnp.stack(tabs)
    chan = lax.iota(I32, GROUP)
    cos_c, sin_c = _trig_rows(GROUP, chan, chan)
    cos_c = (cos_c / math.sqrt(GROUP)).astype(BF16)
    sin_c = (sin_c / math.sqrt(GROUP)).astype(BF16)

    x2 = x.reshape(t, d)
    g_mix = norm_mix_g[0][:, None]
    pf, pp, gates, wbf16, wbp16, wout16 = _proj(
        x2, (g_mix * w_in[0]).astype(BF16), (g_mix * w_gate[0]).astype(BF16), b_gate[0][None],
        (w_branch_f[0], w_branch_p[0], w_out[0]))
    pc, ps, pooled = _seqmix(trig, _pool_tables(s), pf.reshape(nb, s // DFT_RADIX, -1), pp.reshape(nb, s, -1), s)
    wr = norm_moe_g[0][:, None] * w_router[0]
    wr_hi = wr.astype(BF16)
    wr_lo = (wr - wr_hi.astype(F32)).astype(BF16)
    h2, v, logits = _tail(
        pc.reshape(t, -1), ps.reshape(t, -1), pooled.reshape(t, -1), gates, x2, cos_c, sin_c,
        w_fourier_mix[0].astype(BF16), w_pool_mix[0].astype(BF16), pool_scale[0][None],
        wbf16, wbp16, wout16,
        norm_moe_g[0][None], jnp.concatenate([wr_hi, wr_lo], axis=1))
    lg_t = jnp.swapaxes(logits.reshape(nb, s, N_EXPERTS), 1, 2)
    aff, sel, cum = _route(lg_t, cap)
    base, npass = _window_plan(cum, cap)
    xs, gs = _gather(base, npass, sel, aff, v.reshape(nb, s, d), cap)
    ys = _ffn(xs, gs, w_expert_gate[0], w_expert_up[0], w_expert_down[0])
    return _combine(base, npass, jnp.swapaxes(sel, 1, 2), ys, h2.reshape(nb, s, d), norm_final_g[None], cap)
```

```python
import functools
import math

import jax
import jax.numpy as jnp
from jax import lax
from jax.experimental import pallas as pl
from jax.experimental.pallas import tpu as pltpu

F32 = jnp.float32
BF16 = jnp.bfloat16
I32 = jnp.int32

FOURIER_WIDTH = 1024
N_GROUPS = 4
GROUP = 256
POOL_HALF = (1, 2, 4, 8)
N_EXPERTS = 16
EXPERT_FF = 1408
CAPACITY_FACTOR = 2
EPS = 1e-6

V7X_VMEM_LIMIT_BYTES = 58 * 1024 * 1024
BF16_SUBLANES = 16
LANES = 128


def _params(sem, vmem=V7X_VMEM_LIMIT_BYTES):
    return pltpu.CompilerParams(dimension_semantics=sem, vmem_limit_bytes=vmem)


def _const_spec(shape):
    nd = len(shape)
    return pl.BlockSpec(shape, lambda *_: (0,) * nd, pipeline_mode=pl.Buffered(1))


PROJ_TM = 512
PROJ_NC = 512
DFT_RADIX = 4


def _proj_body(x_ref, win_ref, wg_ref, bg_ref, *refs):
    n_later = (len(refs) - 4) // 2
    later_in, (pf_ref, pp_ref, gate_ref) = refs[:n_later], refs[n_later:n_later + 3]
    later_out, split_ref = refs[n_later + 3:n_later + 3 + n_later], refs[-1]
    for src, dst in zip(later_in, later_out):
        dst[...] = src[...].astype(BF16)
    x = x_ref[...]
    xb = x.astype(BF16)
    inv = lax.rsqrt(jnp.mean(x * x, axis=-1, keepdims=True) + EPS)
    part = PROJ_TM // DFT_RADIX
    for j in range(FOURIER_WIDTH // PROJ_NC):
        res = jnp.dot(xb, win_ref[:, j * PROJ_NC:(j + 1) * PROJ_NC], preferred_element_type=F32) * inv
        for c in range(PROJ_NC // LANES):
            col = j * PROJ_NC + c * LANES
            split_ref[c] = res[:, c * LANES:(c + 1) * LANES]
            for r in range(DFT_RADIX):
                pf_ref[:, r * FOURIER_WIDTH + col:r * FOURIER_WIDTH + col + LANES] = (
                    split_ref[c, pl.ds(r, part, stride=DFT_RADIX), :].astype(BF16))
    for j in range((win_ref.shape[1] - FOURIER_WIDTH) // PROJ_NC):
        sl = slice(FOURIER_WIDTH + j * PROJ_NC, FOURIER_WIDTH + (j + 1) * PROJ_NC)
        res = jnp.dot(xb, win_ref[:, sl], preferred_element_type=F32) * inv
        pp_ref[:, j * PROJ_NC:(j + 1) * PROJ_NC] = res.astype(BF16)
    for j in range(wg_ref.shape[1] // PROJ_NC):
        sl = slice(j * PROJ_NC, (j + 1) * PROJ_NC)
        a = jnp.dot(xb, wg_ref[:, sl], preferred_element_type=F32) * inv + bg_ref[:, sl]
        gate_ref[:, sl] = jax.nn.sigmoid(a).astype(BF16)


def _proj(x2, w_in, w_gate, b_gate, later_weights):
    t, d = x2.shape
    dm, dg = w_in.shape[1], w_gate.shape[1]
    steps = t // PROJ_TM
    slices = [pl.BlockSpec((w.shape[0] // steps, w.shape[1]), lambda i: (i, 0)) for w in later_weights]
    assert all(w.shape[0] % (steps * BF16_SUBLANES) == 0 for w in later_weights)
    return pl.pallas_call(
        _proj_body,
        grid=(steps,),
        in_specs=[
            pl.BlockSpec((PROJ_TM, d), lambda i: (i, 0)),
            _const_spec((d, dm)),
            _const_spec((d, dg)),
            _const_spec((1, dg)),
        ] + slices,
        out_specs=[
            pl.BlockSpec((PROJ_TM // DFT_RADIX, DFT_RADIX * FOURIER_WIDTH), lambda i: (i, 0)),
            pl.BlockSpec((PROJ_TM, dm - FOURIER_WIDTH), lambda i: (i, 0)),
            pl.BlockSpec((PROJ_TM, dg), lambda i: (i, 0)),
        ] + slices,
        out_shape=[jax.ShapeDtypeStruct((t // DFT_RADIX, DFT_RADIX * FOURIER_WIDTH), BF16),
                   jax.ShapeDtypeStruct((t, dm - FOURIER_WIDTH), BF16),
                   jax.ShapeDtypeStruct((t, dg), BF16)]
        + [jax.ShapeDtypeStruct(w.shape, BF16) for w in later_weights],
        scratch_shapes=[pltpu.VMEM((PROJ_NC // LANES, PROJ_TM, LANES), F32)],
        compiler_params=_params(("arbitrary",)),
        name="proj",
    )(x2, w_in, w_gate, b_gate, *later_weights)


DFT_MC = 256
DFT_NC = 512


def _fourier_body(trig_ref, *refs):
    p_refs, (pc_ref, ps_ref) = refs[:DFT_RADIX], refs[DFT_RADIX:]
    part = trig_ref.shape[1]
    p_in = [r[...] for r in p_refs]
    for i in range(part // DFT_MC):
        rows = slice(i * DFT_MC, (i + 1) * DFT_MC)
        c = [jnp.dot(trig_ref[2 * r, rows, :], p_in[r], preferred_element_type=F32) for r in range(DFT_RADIX)]
        s = [jnp.dot(trig_ref[2 * r + 1, rows, :], p_in[r], preferred_element_type=F32) for r in range(DFT_RADIX)]
        ec, oc, es, os_ = c[0] + c[2], c[0] - c[2], s[0] + s[2], s[0] - s[2]
        fc, gc, fs, gs = c[1] + c[3], c[1] - c[3], s[1] + s[3], s[1] - s[3]
        xc = (ec + fc, oc - gs, ec - fc, oc + gs)
        xs = (es + fs, os_ + gc, es - fs, os_ - gc)
        for q in range(DFT_RADIX):
            out_rows = slice(q * part + i * DFT_MC, q * part + (i + 1) * DFT_MC)
            pc_ref[out_rows, :] = xc[q].astype(BF16)
            ps_ref[out_rows, :] = xs[q].astype(BF16)


POOL_TM = 256
POOL_HALO = 128


POOL_BAND = POOL_TM + 2 * POOL_HALO


def _pool_body(a_ref, p_ref, o_ref, first_group):
    s = a_ref.shape[1]
    for g in range(p_ref.shape[1] // GROUP):
        cols = slice(g * GROUP, (g + 1) * GROUP)
        for i in range(s // POOL_TM):
            rows = slice(i * POOL_TM, (i + 1) * POOL_TM)
            lo, hi = i * POOL_TM - POOL_HALO, (i + 1) * POOL_TM + POOL_HALO
            band = slice(max(-lo, 0), POOL_BAND - max(hi - s, 0))
            src = slice(max(lo, 0), min(hi, s))
            o_ref[rows, cols] = jnp.dot(a_ref[first_group + g, rows, band], p_ref[src, cols],
                                        preferred_element_type=F32).astype(BF16)


def _seqmix_body(trig_ref, bands_ref, *refs):
    p_refs, pp_ref = refs[:DFT_RADIX], refs[DFT_RADIX]
    pc_ref, ps_ref, po_ref = refs[DFT_RADIX + 1:]
    _fourier_body(trig_ref, *p_refs, pc_ref, ps_ref)
    _pool_body(bands_ref, pp_ref, po_ref, pl.program_id(1) * (DFT_NC // GROUP))


def _seqmix(trig, pool_bands, p_split, pp3, s):
    assert DFT_RADIX == 4 and pp3.shape[-1] == FOURIER_WIDTH
    b, part, _ = p_split.shape
    per_res = FOURIER_WIDTH // DFT_NC
    out = jax.ShapeDtypeStruct((b, s, FOURIER_WIDTH), BF16)
    res_spec = lambda r: pl.BlockSpec((None, part, DFT_NC), lambda i, g: (i, 0, r * per_res + g))
    cols = pl.BlockSpec((None, s, DFT_NC), lambda i, g: (i, 0, g))
    return pl.pallas_call(
        _seqmix_body,
        grid=(b, per_res),
        in_specs=[_const_spec(trig.shape), _const_spec(pool_bands.shape)]
        + [res_spec(r) for r in range(DFT_RADIX)] + [cols],
        out_specs=[cols] * 3,
        out_shape=[out, out, out],
        compiler_params=_params(("arbitrary", "arbitrary")),
        name="seqmix",
    )(trig, pool_bands, *([p_split] * DFT_RADIX), pp3)


TAIL_TM = 256


def _tail_body(pc_ref, ps_ref, pl_ref, gate_ref, x_ref, cc_ref, sc_ref, wf_ref, wp_ref, psc_ref, wbf_ref, wbp_ref,
               wout_ref, gm_ref, wr_ref, h_ref, v_ref, lg_ref):
    yf, yp = [], []
    for g in range(N_GROUPS):
        cols = slice(g * GROUP, (g + 1) * GROUP)
        z = (jnp.dot(pc_ref[:, cols], cc_ref[...], preferred_element_type=F32)
             - jnp.dot(ps_ref[:, cols], sc_ref[...], preferred_element_type=F32))
        yf.append(jnp.dot(z.astype(BF16), wf_ref[g], preferred_element_type=F32).astype(BF16))
        ypg = jnp.dot(pl_ref[:, cols], wp_ref[g], preferred_element_type=F32) * psc_ref[:, cols]
        yp.append(ypg.astype(BF16))
    yf = jnp.concatenate(yf, axis=-1)
    yp = jnp.concatenate(yp, axis=-1)
    d = wout_ref.shape[0]
    bf = jnp.dot(yf, wbf_ref[...], preferred_element_type=F32)
    bp = jnp.dot(yp, wbp_ref[...], preferred_element_type=F32)
    merged = gate_ref[:, :d].astype(F32) * bf + gate_ref[:, d:].astype(F32) * bp
    h = x_ref[...] + jnp.dot(merged.astype(BF16), wout_ref[...], preferred_element_type=F32)
    h_ref[...] = h
    inv = lax.rsqrt(jnp.mean(h * h, axis=-1, keepdims=True) + EPS)
    v_ref[...] = (h * inv * gm_ref[...]).astype(BF16)
    acc = jnp.dot(h.astype(BF16), wr_ref[...], preferred_element_type=F32)
    lg_ref[...] = (acc[:, :N_EXPERTS] + acc[:, N_EXPERTS:]) * inv


def _tail(pc2, ps2, pl2, gates, x2, cc, sc, wf, wp, pscale, wbf, wbp, wout, gm, wr2):
    t, d = x2.shape
    row = lambda width: pl.BlockSpec((TAIL_TM, width), lambda i: (i, 0))
    return pl.pallas_call(
        _tail_body,
        grid=(t // TAIL_TM,),
        in_specs=[
            row(pc2.shape[1]), row(ps2.shape[1]), row(pl2.shape[1]), row(gates.shape[1]), row(d),
            _const_spec(cc.shape), _const_spec(sc.shape), _const_spec(wf.shape), _const_spec(wp.shape),
            _const_spec(pscale.shape), _const_spec(wbf.shape), _const_spec(wbp.shape),
            _const_spec(wout.shape), _const_spec(gm.shape), _const_spec(wr2.shape),
        ],
        out_specs=[row(d), row(d), row(N_EXPERTS)],
        out_shape=[jax.ShapeDtypeStruct((t, d), F32), jax.ShapeDtypeStruct((t, d), BF16),
                   jax.ShapeDtypeStruct((t, N_EXPERTS), F32)],
        compiler_params=_params(("parallel",)),
        name="tail",
    )(pc2, ps2, pl2, gates, x2, cc, sc, wf, wp, pscale, wbf, wbp, wout, gm, wr2)


F32_VALUE_BITS = 31


def _route_body(lg_ref, aff_ref, sel_ref, cum_ref, selt_ref, *, cap):
    nb, ne, s = lg_ref.shape
    lg = lg_ref[...]
    mx = jnp.max(lg, axis=1, keepdims=True)
    ex = jnp.exp(lg - mx)
    aff = ex / jnp.sum(ex, axis=1, keepdims=True)
    aff_ref[...] = aff
    aff2 = aff.reshape(nb * ne, s)

    def step(i, bits):
        cand = bits | jnp.left_shift(jnp.int32(1), F32_VALUE_BITS - 1 - i)
        n_ge = jnp.sum(jnp.where(aff2 >= pltpu.bitcast(cand, F32), 1.0, 0.0), axis=1, keepdims=True)
        return jnp.where(n_ge >= cap, cand, bits)

    thr = pltpu.bitcast(lax.fori_loop(0, F32_VALUE_BITS, step, jnp.zeros((nb * ne, 1), I32)), F32)
    above = aff2 > thr
    tie = aff2 == thr
    n_above = jnp.sum(jnp.where(above, 1.0, 0.0), axis=1, keepdims=True)
    tri = jnp.where(lax.broadcasted_iota(I32, (s, s), 0) <= lax.broadcasted_iota(I32, (s, s), 1), 1.0, 0.0).astype(BF16)
    tie_rank = jnp.dot(jnp.where(tie, 1.0, 0.0).astype(BF16), tri, preferred_element_type=F32)
    chosen = above | (tie & (tie_rank <= (cap - n_above)))
    cum = jnp.dot(jnp.where(chosen, 1.0, 0.0).astype(BF16), tri, preferred_element_type=F32)
    cum_ref[...] = cum.astype(I32).reshape(nb, ne, s)
    sel = jnp.where(chosen, cum - 1.0, -1.0).astype(I32)
    sel_ref[...] = sel.reshape(nb, ne, s)
    selt_ref[...] = sel.T


def _route(lg_t, cap):
    nb, ne, s = lg_t.shape
    ints = jax.ShapeDtypeStruct((nb, ne, s), I32)
    return pl.pallas_call(
        functools.partial(_route_body, cap=cap),
        out_shape=[jax.ShapeDtypeStruct((nb, ne, s), F32), ints, ints, jax.ShapeDtypeStruct((s, nb * ne), I32)],
        compiler_params=pltpu.CompilerParams(vmem_limit_bytes=V7X_VMEM_LIMIT_BYTES),
        name="route",
    )(lg_t)


TOK_BLK = 256
WIN = 64


def _window_plan(cum, cap):
    c_end = cum[:, :, TOK_BLK - 1::TOK_BLK]
    c_start = jnp.concatenate([jnp.zeros_like(c_end[:, :, :1]), c_end[:, :, :-1]], axis=-1)
    base = (c_start // BF16_SUBLANES) * BF16_SUBLANES
    passes = jnp.where(c_end > c_start, (c_end - base + WIN - 1) // WIN, 0)
    return jnp.swapaxes(base, 1, 2), jnp.max(passes, axis=1)


GATHER_DT = 1024


def _gather_body(base_s, npass_s, sel_ref, aff_ref, basec_ref, v_ref, xs_ref, gs_ref, *, cap):
    b = pl.program_id(0)
    ne, s = sel_ref.shape
    nk = s // TOK_BLK
    xs_ref[...] = jnp.zeros(xs_ref.shape, xs_ref.dtype)
    gs_ref[...] = jnp.zeros(gs_ref.shape, gs_ref.dtype)
    wiota = lax.broadcasted_iota(I32, (WIN, TOK_BLK), 0)
    passes = []
    for k in range(nk):
        toks = slice(k * TOK_BLK, (k + 1) * TOK_BLK)
        selk = sel_ref[:, toks]
        basek = basec_ref[k]

        def one_pass(p, carry, k=k, toks=toks, selk=selk, basek=basek):
            first = basek + p * WIN
            wbase = jnp.minimum(first, cap - WIN)
            rel = jnp.where(selk >= first, selk - wbase, -1)
            hits = [rel[e:e + 1, :] == wiota for e in range(ne)]
            onehot = jnp.concatenate([jnp.where(h, 1.0, 0.0) for h in hits], axis=0).astype(BF16)
            rows = jnp.dot(onehot, v_ref[toks, :], preferred_element_type=F32)
            offs = [pl.multiple_of(jnp.minimum(base_s[(b * nk + k) * ne + e] + p * WIN, cap - WIN), BF16_SUBLANES)
                    for e in range(ne)]
            for e in range(ne):
                xs_ref[e, pl.ds(offs[e], WIN), :] += rows[e * WIN:(e + 1) * WIN].astype(BF16)
                gate = jnp.sum(jnp.where(hits[e], aff_ref[e:e + 1, toks], 0.0), axis=1, keepdims=True)
                gs_ref[e, pl.ds(offs[e], WIN), :] += gate
            return carry

        passes.append(one_pass)

    for k in range(nk):
        passes[k](0, 0)
    for k in range(nk):
        lax.fori_loop(1, npass_s[b * nk + k], passes[k], 0)


def _gather(base, npass, sel, aff, v3, cap):
    nb, ne, s = sel.shape
    d = v3.shape[-1]
    nk = s // TOK_BLK
    grid_spec = pltpu.PrefetchScalarGridSpec(
        num_scalar_prefetch=2,
        grid=(nb, d // GATHER_DT),
        in_specs=[
            pl.BlockSpec((None, ne, s), lambda b, j, *_: (b, 0, 0)),
            pl.BlockSpec((None, ne, s), lambda b, j, *_: (b, 0, 0)),
            pl.BlockSpec((None, nk, ne, 1), lambda b, j, *_: (b, 0, 0, 0)),
            pl.BlockSpec((None, s, GATHER_DT), lambda b, j, *_: (b, 0, j)),
        ],
        out_specs=[
            pl.BlockSpec((ne, cap, GATHER_DT), lambda b, j, *_: (0, b, j)),
            pl.BlockSpec((ne, cap, 1), lambda b, j, *_: (0, b, 0)),
        ],
    )
    return pl.pallas_call(
        functools.partial(_gather_body, cap=cap),
        grid_spec=grid_spec,
        out_shape=[jax.ShapeDtypeStruct((ne, nb * cap, d), BF16),
                   jax.ShapeDtypeStruct((ne, nb * cap, 1), F32)],
        compiler_params=_params(("arbitrary", "arbitrary")),
        name="gather",
    )(base.reshape(-1), npass.reshape(-1), sel, aff, base[..., None], v3)


COMB_HSLOTS = 3


def _combine_body(base_s, npass_s, selc_ref, baser_ref, y_hbm, h_hbm, g_ref, o_ref, acc_ref, hbuf, ybuf, hsem, ysem,
                  *, cap, nb, nk):
    b = pl.program_id(0)
    k = pl.program_id(1)
    ne = ybuf.shape[1]
    per = ne // nk
    step = b * nk + k

    def h_copy(st):
        st = jnp.asarray(st, I32)
        rows = pl.ds(pl.multiple_of((st % nk) * TOK_BLK, TOK_BLK), TOK_BLK)
        slot = st % COMB_HSLOTS
        return pltpu.make_async_copy(h_hbm.at[st // nk, rows, :], hbuf.at[slot], hsem.at[slot])

    def y_copy(batch, j):
        batch = jnp.asarray(batch, I32)
        experts = pl.ds(j * per, per)
        rows = pl.ds(pl.multiple_of(batch * cap, cap), cap)
        return pltpu.make_async_copy(y_hbm.at[experts, rows, :], ybuf.at[batch % 2, experts], ysem.at[batch % 2, j])

    @pl.when(step == 0)
    def _prime():
        h_copy(0).start()
        h_copy(1).start()
        for j in range(nk):
            y_copy(0, j).start()

    @pl.when(step + 2 < nb * nk)
    def _():
        h_copy(step + 2).start()

    @pl.when(b + 1 < nb)
    def _():
        y_copy(b + 1, k).start()

    h_copy(step).wait()

    @pl.when(k == 0)
    def _():
        for j in range(nk):
            y_copy(b, j).wait()

    y_ref = ybuf.at[b % 2]
    acc_ref[...] = hbuf[step % COMB_HSLOTS]
    selk = selc_ref[...]
    basek = baser_ref[...]
    pairs = selk.shape[1]
    lane = lax.broadcasted_iota(I32, (pairs, ne * WIN), 1)
    spread = jnp.where(b * ne + lane // WIN == lax.broadcasted_iota(I32, (pairs, ne * WIN), 0), 1.0, 0.0).astype(BF16)
    wlane = (lax.broadcasted_iota(I32, (1, ne * WIN), 1) % WIN).astype(F32)

    def one_pass(p, carry):
        first = basek + p * WIN
        wbase = jnp.minimum(first, cap - WIN)
        rel = jnp.where(selk >= first, selk - wbase, -1)
        relx = jnp.dot(rel.astype(F32).astype(BF16), spread, preferred_element_type=F32)
        onehot = jnp.where(relx == wlane, 1.0, 0.0).astype(BF16)
        wins = []
        for e in range(ne):
            off = jnp.minimum(base_s[(b * nk + k) * ne + e] + p * WIN, cap - WIN)
            wins.append(y_ref[e, pl.ds(pl.multiple_of(off, BF16_SUBLANES), WIN), :])
        acc_ref[...] += jnp.dot(onehot, jnp.concatenate(wins, axis=0), preferred_element_type=F32)
        return carry

    one_pass(0, 0)
    lax.fori_loop(1, npass_s[b * nk + k], one_pass, 0)
    h = acc_ref[...]
    ms = jnp.mean(h * h, axis=-1, keepdims=True)
    o_ref[...] = h * lax.rsqrt(ms + EPS) * g_ref[...]


def _combine(base, npass, sel_rows, ys, h3, g_final, cap):
    nb, s, d = h3.shape
    ne = ys.shape[0]
    nk = s // TOK_BLK
    grid_spec = pltpu.PrefetchScalarGridSpec(
        num_scalar_prefetch=2,
        grid=(nb, nk),
        in_specs=[
            pl.BlockSpec((TOK_BLK, nb * ne), lambda b, k, *_: (k, 0)),
            pl.BlockSpec((None, 1, nb * ne), lambda b, k, *_: (k, 0, 0)),
            pl.BlockSpec(memory_space=pl.ANY),
            pl.BlockSpec(memory_space=pl.ANY),
            pl.BlockSpec((1, d), lambda b, k, *_: (0, 0)),
        ],
        out_specs=pl.BlockSpec((None, TOK_BLK, d), lambda b, k, *_: (b, k, 0)),
        scratch_shapes=[
            pltpu.VMEM((TOK_BLK, d), F32),
            pltpu.VMEM((COMB_HSLOTS, TOK_BLK, d), F32),
            pltpu.VMEM((2, ne, cap, d), BF16),
            pltpu.SemaphoreType.DMA((COMB_HSLOTS,)),
            pltpu.SemaphoreType.DMA((2, nk)),
        ],
    )
    assert ne % nk == 0 and nb * nk >= 2
    return pl.pallas_call(
        functools.partial(_combine_body, cap=cap, nb=nb, nk=nk),
        grid_spec=grid_spec,
        out_shape=jax.ShapeDtypeStruct((nb, s, d), F32),
        compiler_params=_params(("arbitrary", "arbitrary")),
        name="combine",
    )(base.reshape(-1), npass.reshape(-1), sel_rows, jnp.swapaxes(base, 0, 1).reshape(nk, 1, nb * ne), ys, h3, g_final)


FFN_TM = 512
FFN_STEPS = 8
FFN_NC = 512


def _ffn_body(xs_ref, gs_ref, wg_ref, wu_ref, wd_ref, y_ref, wgu_scr, wd_scr, hid_scr):
    e1 = pl.program_id(0)
    j = pl.program_id(1)
    ff = wg_ref.shape[1]
    nt = ff // LANES

    nc = y_ref.shape[1] // FFN_NC
    stage_slot = e1 % 2
    slot = (e1 - 1) % 2
    kr, fr = wg_ref.shape[0], wd_ref.shape[0]
    r0 = pl.multiple_of(j * kr, kr)
    f0 = pl.multiple_of(j * fr, BF16_SUBLANES)

    def stage_gate_up(t):
        src = slice(t * LANES, (t + 1) * LANES)
        wgu_scr[stage_slot, pl.ds(r0, kr), 2 * t * LANES:(2 * t + 1) * LANES] = wg_ref[:, src].astype(BF16)
        wgu_scr[stage_slot, pl.ds(r0, kr), (2 * t + 1) * LANES:(2 * t + 2) * LANES] = wu_ref[:, src].astype(BF16)

    def stage_down(c):
        cols = slice(c * FFN_NC, (c + 1) * FFN_NC)
        wd_scr[stage_slot, pl.ds(f0, fr), cols] = wd_ref[:, cols].astype(BF16)

    pieces = [functools.partial(stage_gate_up, t) for t in range(nt)] + [
        functools.partial(stage_down, c) for c in range(nc)]

    def spread(n_chunks):
        return [pieces[i::n_chunks] for i in range(n_chunks)]

    @pl.when(e1 == 0)
    def _load_first_expert():
        for piece in pieces:
            piece()

    @pl.when((e1 > 0) & (j % 2 == 0))
    def _gate_up():
        xs = xs_ref[...]
        for t, todo in enumerate(spread(nt)):
            for piece in todo:
                piece()
            res = jnp.dot(xs, wgu_scr[slot, :, 2 * t * LANES:(2 * t + 2) * LANES], preferred_element_type=F32)
            hid_scr[:, t * LANES:(t + 1) * LANES] = (jax.nn.silu(res[:, :LANES]) * res[:, LANES:]).astype(BF16)

    @pl.when((e1 > 0) & (j % 2 == 1))
    def _down():
        hid = hid_scr[...]
        for c, todo in enumerate(spread(nc)):
            for piece in todo:
                piece()
            cols = slice(c * FFN_NC, (c + 1) * FFN_NC)
            y = jnp.dot(hid, wd_scr[slot, :, cols], preferred_element_type=F32)
            y_ref[:, cols] = (y * gs_ref[...]).astype(BF16)


def _ffn(xs, gs, w_gate_e, w_up_e, w_down_e):
    ne, m, d = xs.shape
    ff = w_gate_e.shape[-1]
    assert m == FFN_TM * FFN_STEPS // 2 and d % FFN_STEPS == 0 and ff % (FFN_STEPS * BF16_SUBLANES) == 0
    assert ff % LANES == 0
    prev = lambda e1, j: (jnp.maximum(e1 - 1, 0), jnp.where(e1 == 0, 0, j // 2), 0)
    nxt = lambda e1, j: (jnp.minimum(e1, ne - 1), j, 0)
    return pl.pallas_call(
        _ffn_body,
        grid=(ne + 1, FFN_STEPS),
        in_specs=[
            pl.BlockSpec((None, FFN_TM, d), prev),
            pl.BlockSpec((None, FFN_TM, 1), prev),
            pl.BlockSpec((None, d // FFN_STEPS, ff), nxt),
            pl.BlockSpec((None, d // FFN_STEPS, ff), nxt),
            pl.BlockSpec((None, ff // FFN_STEPS, d), nxt),
        ],
        out_specs=pl.BlockSpec((None, FFN_TM, d), prev),
        out_shape=jax.ShapeDtypeStruct((ne, m, d), BF16),
        scratch_shapes=[pltpu.VMEM((2, d, 2 * ff), BF16), pltpu.VMEM((2, ff, d), BF16),
                        pltpu.VMEM((FFN_TM, ff), BF16)],
        compiler_params=_params(("arbitrary", "arbitrary")),
        name="ffn",
    )(xs, gs, w_gate_e, w_up_e, w_down_e)


TRIG_SPLIT = 32


def _trig_rows(n, rows, cols):
    ang = ((rows[:, None] * cols[None, :]) % n).astype(F32) * (2.0 * math.pi / n)
    return jnp.cos(ang), jnp.sin(ang)


def _dft_tables(n, rows, cols):
    nr = rows.shape[0]
    r1 = lax.iota(I32, nr // TRIG_SPLIT) * TRIG_SPLIT
    r0 = lax.iota(I32, TRIG_SPLIT)
    c1, s1 = _trig_rows(n, r1, cols)
    c0, s0 = _trig_rows(n, r0, cols)
    scale = 1.0 / math.sqrt(n)
    c1, s1 = (c1 * scale)[:, None, :], (s1 * scale)[:, None, :]
    c0, s0 = c0[None], s0[None]
    cos = (c1 * c0 - s1 * s0).reshape(nr, -1)
    sin = (s1 * c0 + c1 * s0).reshape(nr, -1)
    return cos, sin


def _pool_tables(s):
    i = lax.iota(I32, s)[:, None]
    k = (i // POOL_TM) * POOL_TM - POOL_HALO + lax.iota(I32, POOL_BAND)[None, :]
    out = []
    for half in POOL_HALF:
        lo = jnp.clip(i - half, 0, s)
        hi = jnp.clip(i + half, 0, s)
        inside = (k >= lo) & (k < hi)
        cnt = (hi - lo).astype(F32)
        out.append((jnp.where(inside, 1.0 / cnt, 0.0) - jnp.where(i == k, 1.0, 0.0)).astype(BF16))
    return jnp.stack(out)


def kernel(x, norm_mix_g, w_in, w_fourier_mix, w_pool_mix, pool_scale, w_branch_f, w_branch_p, w_gate,
           b_gate, w_out, norm_moe_g, w_router, w_expert_gate, w_expert_up, w_expert_down, norm_final_g):
    nb, s, d = x.shape
    assert w_in.shape[0] == 1, "single-layer block only"
    assert s % (2 * TOK_BLK) == 0 and max(POOL_HALF) <= POOL_HALO
    cap = CAPACITY_FACTOR * s // N_EXPERTS
    t = nb * s

    freqs = lax.iota(I32, s // DFT_RADIX)
    tabs = []
    for r in range(DFT_RADIX):
        cos_r, sin_r = _dft_tables(s, freqs, DFT_RADIX * freqs + r)
        tabs += [cos_r.astype(BF16), sin_r.astype(BF16)]
    trig = jnp.stack(tabs)
    chan = lax.iota(I32, GROUP)
    cos_c, sin_c = _trig_rows(GROUP, chan, chan)
    cos_c = (cos_c / math.sqrt(GROUP)).astype(BF16)
    sin_c = (sin_c / math.sqrt(GROUP)).astype(BF16)

    x2 = x.reshape(t, d)
    g_mix = norm_mix_g[0][:, None]
    pf, pp, gates, wbf16, wbp16, wout16 = _proj(
        x2, (g_mix * w_in[0]).astype(BF16), (g_mix * w_gate[0]).astype(BF16), b_gate[0][None],
        (w_branch_f[0], w_branch_p[0], w_out[0]))
    pc, ps, pooled = _seqmix(trig, _pool_tables(s), pf.reshape(nb, s // DFT_RADIX, -1), pp.reshape(nb, s, -1), s)
    wr = norm_moe_g[0][:, None] * w_router[0]
    wr_hi = wr.astype(BF16)
    wr_lo = (wr - wr_hi.astype(F32)).astype(BF16)
    h2, v, logits = _tail(
        pc.reshape(t, -1), ps.reshape(t, -1), pooled.reshape(t, -1), gates, x2, cos_c, sin_c,
        w_fourier_mix[0].astype(BF16), w_pool_mix[0].astype(BF16), pool_scale[0][None],
        wbf16, wbp16, wout16,
        norm_moe_g[0][None], jnp.concatenate([wr_hi, wr_lo], axis=1))
    lg_t = jnp.swapaxes(logits.reshape(nb, s, N_EXPERTS), 1, 2)
    aff, sel, cum, sel_rows = _route(lg_t, cap)
    base, npass = _window_plan(cum, cap)
    xs, gs = _gather(base, npass, sel, aff, v.reshape(nb, s, d), cap)
    ys = _ffn(xs, gs, w_expert_gate[0], w_expert_up[0], w_expert_down[0])
    return _combine(base, npass, sel_rows, ys, h2.reshape(nb, s, d), norm_final_g[None], cap)
```

```python
import functools
import math

import jax
import jax.numpy as jnp
from jax import lax
from jax.experimental import pallas as pl
from jax.experimental.pallas import tpu as pltpu

F32 = jnp.float32
BF16 = jnp.bfloat16
I32 = jnp.int32

FOURIER_WIDTH = 1024
N_GROUPS = 4
GROUP = 256
POOL_HALF = (1, 2, 4, 8)
N_EXPERTS = 16
EXPERT_FF = 1408
CAPACITY_FACTOR = 2
EPS = 1e-6

V7X_VMEM_LIMIT_BYTES = 58 * 1024 * 1024
BF16_SUBLANES = 16
LANES = 128


def _params(sem, vmem=V7X_VMEM_LIMIT_BYTES):
    return pltpu.CompilerParams(dimension_semantics=sem, vmem_limit_bytes=vmem)


def _const_spec(shape):
    nd = len(shape)
    return pl.BlockSpec(shape, lambda *_: (0,) * nd, pipeline_mode=pl.Buffered(1))


PROJ_TM = 512
PROJ_NC = 512
DFT_RADIX = 4


def _proj_body(x_ref, win_ref, wg_ref, bg_ref, *refs):
    n_later = (len(refs) - 4) // 2
    later_in, (pf_ref, pp_ref, gate_ref) = refs[:n_later], refs[n_later:n_later + 3]
    later_out, split_ref = refs[n_later + 3:n_later + 3 + n_later], refs[-1]
    for src, dst in zip(later_in, later_out):
        dst[...] = src[...].astype(BF16)
    x = x_ref[...]
    xb = x.astype(BF16)
    inv = lax.rsqrt(jnp.mean(x * x, axis=-1, keepdims=True) + EPS)
    part = PROJ_TM // DFT_RADIX
    for j in range(FOURIER_WIDTH // PROJ_NC):
        res = jnp.dot(xb, win_ref[:, j * PROJ_NC:(j + 1) * PROJ_NC], preferred_element_type=F32) * inv
        for c in range(PROJ_NC // LANES):
            col = j * PROJ_NC + c * LANES
            split_ref[c] = res[:, c * LANES:(c + 1) * LANES]
            for r in range(DFT_RADIX):
                pf_ref[:, r * FOURIER_WIDTH + col:r * FOURIER_WIDTH + col + LANES] = (
                    split_ref[c, pl.ds(r, part, stride=DFT_RADIX), :].astype(BF16))
    for j in range((win_ref.shape[1] - FOURIER_WIDTH) // PROJ_NC):
        sl = slice(FOURIER_WIDTH + j * PROJ_NC, FOURIER_WIDTH + (j + 1) * PROJ_NC)
        res = jnp.dot(xb, win_ref[:, sl], preferred_element_type=F32) * inv
        pp_ref[:, j * PROJ_NC:(j + 1) * PROJ_NC] = res.astype(BF16)
    for j in range(wg_ref.shape[1] // PROJ_NC):
        sl = slice(j * PROJ_NC, (j + 1) * PROJ_NC)
        a = jnp.dot(xb, wg_ref[:, sl], preferred_element_type=F32) * inv + bg_ref[:, sl]
        gate_ref[:, sl] = jax.nn.sigmoid(a).astype(BF16)


def _proj(x2, w_in, w_gate, b_gate, later_weights):
    t, d = x2.shape
    dm, dg = w_in.shape[1], w_gate.shape[1]
    steps = t // PROJ_TM
    slices = [pl.BlockSpec((w.shape[0] // steps, w.shape[1]), lambda i: (i, 0)) for w in later_weights]
    assert all(w.shape[0] % (steps * BF16_SUBLANES) == 0 for w in later_weights)
    return pl.pallas_call(
        _proj_body,
        grid=(steps,),
        in_specs=[
            pl.BlockSpec((PROJ_TM, d), lambda i: (i, 0)),
            _const_spec((d, dm)),
            _const_spec((d, dg)),
            _const_spec((1, dg)),
        ] + slices,
        out_specs=[
            pl.BlockSpec((PROJ_TM // DFT_RADIX, DFT_RADIX * FOURIER_WIDTH), lambda i: (i, 0)),
            pl.BlockSpec((PROJ_TM, dm - FOURIER_WIDTH), lambda i: (i, 0)),
            pl.BlockSpec((PROJ_TM, dg), lambda i: (i, 0)),
        ] + slices,
        out_shape=[jax.ShapeDtypeStruct((t // DFT_RADIX, DFT_RADIX * FOURIER_WIDTH), BF16),
                   jax.ShapeDtypeStruct((t, dm - FOURIER_WIDTH), BF16),
                   jax.ShapeDtypeStruct((t, dg), BF16)]
        + [jax.ShapeDtypeStruct(w.shape, BF16) for w in later_weights],
        scratch_shapes=[pltpu.VMEM((PROJ_NC // LANES, PROJ_TM, LANES), F32)],
        compiler_params=_params(("arbitrary",)),
        name="proj",
    )(x2, w_in, w_gate, b_gate, *later_weights)


DFT_MC = 256
DFT_NC = 512


def _fourier_body(trig_ref, *refs):
    p_refs, (pc_ref, ps_ref) = refs[:DFT_RADIX], refs[DFT_RADIX:]
    part = trig_ref.shape[1]
    p_in = [r[...] for r in p_refs]
    for i in range(part // DFT_MC):
        rows = slice(i * DFT_MC, (i + 1) * DFT_MC)
        c = [jnp.dot(trig_ref[2 * r, rows, :], p_in[r], preferred_element_type=F32) for r in range(DFT_RADIX)]
        s = [jnp.dot(trig_ref[2 * r + 1, rows, :], p_in[r], preferred_element_type=F32) for r in range(DFT_RADIX)]
        ec, oc, es, os_ = c[0] + c[2], c[0] - c[2], s[0] + s[2], s[0] - s[2]
        fc, gc, fs, gs = c[1] + c[3], c[1] - c[3], s[1] + s[3], s[1] - s[3]
        xc = (ec + fc, oc - gs, ec - fc, oc + gs)
        xs = (es + fs, os_ + gc, es - fs, os_ - gc)
        for q in range(DFT_RADIX):
            out_rows = slice(q * part + i * DFT_MC, q * part + (i + 1) * DFT_MC)
            pc_ref[out_rows, :] = xc[q].astype(BF16)
            ps_ref[out_rows, :] = xs[q].astype(BF16)


POOL_TM = 256
POOL_HALO = 128


POOL_BAND = POOL_TM + 2 * POOL_HALO


def _pool_body(a_ref, p_ref, o_ref, first_group):
    s = a_ref.shape[1]
    for g in range(p_ref.shape[1] // GROUP):
        cols = slice(g * GROUP, (g + 1) * GROUP)
        for i in range(s // POOL_TM):
            rows = slice(i * POOL_TM, (i + 1) * POOL_TM)
            lo, hi = i * POOL_TM - POOL_HALO, (i + 1) * POOL_TM + POOL_HALO
            band = slice(max(-lo, 0), POOL_BAND - max(hi - s, 0))
            src = slice(max(lo, 0), min(hi, s))
            o_ref[rows, cols] = jnp.dot(a_ref[first_group + g, rows, band], p_ref[src, cols],
                                        preferred_element_type=F32).astype(BF16)


def _seqmix_body(trig_ref, bands_ref, *refs):
    p_refs, pp_ref = refs[:DFT_RADIX], refs[DFT_RADIX]
    pc_ref, ps_ref, po_ref = refs[DFT_RADIX + 1:]
    _fourier_body(trig_ref, *p_refs, pc_ref, ps_ref)
    _pool_body(bands_ref, pp_ref, po_ref, pl.program_id(1) * (DFT_NC // GROUP))


def _seqmix(trig, pool_bands, p_split, pp3, s):
    assert DFT_RADIX == 4 and pp3.shape[-1] == FOURIER_WIDTH
    b, part, _ = p_split.shape
    per_res = FOURIER_WIDTH // DFT_NC
    out = jax.ShapeDtypeStruct((b, s, FOURIER_WIDTH), BF16)
    res_spec = lambda r: pl.BlockSpec((None, part, DFT_NC), lambda i, g: (i, 0, r * per_res + g))
    cols = pl.BlockSpec((None, s, DFT_NC), lambda i, g: (i, 0, g))
    return pl.pallas_call(
        _seqmix_body,
        grid=(b, per_res),
        in_specs=[_const_spec(trig.shape), _const_spec(pool_bands.shape)]
        + [res_spec(r) for r in range(DFT_RADIX)] + [cols],
        out_specs=[cols] * 3,
        out_shape=[out, out, out],
        compiler_params=_params(("arbitrary", "arbitrary")),
        name="seqmix",
    )(trig, pool_bands, *([p_split] * DFT_RADIX), pp3)


TAIL_TM = 256


def _fold_body(wf_ref, wbf_ref, wp_ref, psc_ref, wbp_ref, of_ref, op_ref):
    of_ref[...] = jnp.dot(wf_ref[...].astype(BF16), wbf_ref[...].astype(BF16),
                          preferred_element_type=F32).astype(BF16)
    op_ref[...] = jnp.dot((wp_ref[...] * psc_ref[...]).astype(BF16), wbp_ref[...].astype(BF16),
                          preferred_element_type=F32).astype(BF16)


def _fold(wf, wbf, wp, pscale, wbp):
    ng, gsz, _ = wf.shape
    d = wbf.shape[1]
    grp = pl.BlockSpec((None, gsz, gsz), lambda g: (g, 0, 0))
    rows = pl.BlockSpec((gsz, d), lambda g: (g, 0))
    out = jax.ShapeDtypeStruct((ng * gsz, d), BF16)
    return pl.pallas_call(
        _fold_body,
        grid=(ng,),
        in_specs=[grp, rows, grp, pl.BlockSpec((1, gsz), lambda g: (0, g)), rows],
        out_specs=[rows, rows],
        out_shape=[out, out],
        compiler_params=_params(("arbitrary",)),
        name="fold",
    )(wf, wbf, wp, pscale, wbp)


def _tail_body(pc_ref, ps_ref, pl_ref, gate_ref, x_ref, cc_ref, sc_ref, wbf_ref, wbp_ref, wout_ref, gm_ref, wr_ref,
               h_ref, v_ref, lg_ref):
    z = []
    for g in range(N_GROUPS):
        cols = slice(g * GROUP, (g + 1) * GROUP)
        zg = (jnp.dot(pc_ref[:, cols], cc_ref[...], preferred_element_type=F32)
              - jnp.dot(ps_ref[:, cols], sc_ref[...], preferred_element_type=F32))
        z.append(zg.astype(BF16))
    d = wout_ref.shape[0]
    bf = jnp.dot(jnp.concatenate(z, axis=-1), wbf_ref[...], preferred_element_type=F32)
    bp = jnp.dot(pl_ref[...], wbp_ref[...], preferred_element_type=F32)
    merged = gate_ref[:, :d].astype(F32) * bf + gate_ref[:, d:].astype(F32) * bp
    h = x_ref[...] + jnp.dot(merged.astype(BF16), wout_ref[...], preferred_element_type=F32)
    h_ref[...] = h
    inv = lax.rsqrt(jnp.mean(h * h, axis=-1, keepdims=True) + EPS)
    v_ref[...] = (h * inv * gm_ref[...]).astype(BF16)
    acc = jnp.dot(h.astype(BF16), wr_ref[...], preferred_element_type=F32)
    lg_ref[...] = (acc[:, :N_EXPERTS] + acc[:, N_EXPERTS:]) * inv


def _tail(pc2, ps2, pl2, gates, x2, cc, sc, wbf, wbp, wout, gm, wr2):
    t, d = x2.shape
    row = lambda width: pl.BlockSpec((TAIL_TM, width), lambda i: (i, 0))
    return pl.pallas_call(
        _tail_body,
        grid=(t // TAIL_TM,),
        in_specs=[
            row(pc2.shape[1]), row(ps2.shape[1]), row(pl2.shape[1]), row(gates.shape[1]), row(d),
            _const_spec(cc.shape), _const_spec(sc.shape), _const_spec(wbf.shape), _const_spec(wbp.shape),
            _const_spec(wout.shape), _const_spec(gm.shape), _const_spec(wr2.shape),
        ],
        out_specs=[row(d), row(d), row(N_EXPERTS)],
        out_shape=[jax.ShapeDtypeStruct((t, d), F32), jax.ShapeDtypeStruct((t, d), BF16),
                   jax.ShapeDtypeStruct((t, N_EXPERTS), F32)],
        compiler_params=_params(("parallel",)),
        name="tail",
    )(pc2, ps2, pl2, gates, x2, cc, sc, wbf, wbp, wout, gm, wr2)


F32_VALUE_BITS = 31


def _route_body(lg_ref, aff_ref, sel_ref, cum_ref, selt_ref, *, cap):
    nb, ne, s = lg_ref.shape
    lg = lg_ref[...]
    mx = jnp.max(lg, axis=1, keepdims=True)
    ex = jnp.exp(lg - mx)
    aff = ex / jnp.sum(ex, axis=1, keepdims=True)
    aff_ref[...] = aff
    aff2 = aff.reshape(nb * ne, s)

    def step(i, bits):
        cand = bits | jnp.left_shift(jnp.int32(1), F32_VALUE_BITS - 1 - i)
        n_ge = jnp.sum(jnp.where(aff2 >= pltpu.bitcast(cand, F32), 1.0, 0.0), axis=1, keepdims=True)
        return jnp.where(n_ge >= cap, cand, bits)

    thr = pltpu.bitcast(lax.fori_loop(0, F32_VALUE_BITS, step, jnp.zeros((nb * ne, 1), I32)), F32)
    above = aff2 > thr
    tie = aff2 == thr
    n_above = jnp.sum(jnp.where(above, 1.0, 0.0), axis=1, keepdims=True)
    tri = jnp.where(lax.broadcasted_iota(I32, (s, s), 0) <= lax.broadcasted_iota(I32, (s, s), 1), 1.0, 0.0).astype(BF16)
    tie_rank = jnp.dot(jnp.where(tie, 1.0, 0.0).astype(BF16), tri, preferred_element_type=F32)
    chosen = above | (tie & (tie_rank <= (cap - n_above)))
    cum = jnp.dot(jnp.where(chosen, 1.0, 0.0).astype(BF16), tri, preferred_element_type=F32)
    cum_ref[...] = cum.astype(I32).reshape(nb, ne, s)
    sel = jnp.where(chosen, cum - 1.0, -1.0).astype(I32)
    sel_ref[...] = sel.reshape(nb, ne, s)
    selt_ref[...] = sel.T


def _route(lg_t, cap):
    nb, ne, s = lg_t.shape
    ints = jax.ShapeDtypeStruct((nb, ne, s), I32)
    return pl.pallas_call(
        functools.partial(_route_body, cap=cap),
        out_shape=[jax.ShapeDtypeStruct((nb, ne, s), F32), ints, ints, jax.ShapeDtypeStruct((s, nb * ne), I32)],
        compiler_params=pltpu.CompilerParams(vmem_limit_bytes=V7X_VMEM_LIMIT_BYTES),
        name="route",
    )(lg_t)


TOK_BLK = 256
WIN = 64


def _window_plan(cum, cap):
    c_end = cum[:, :, TOK_BLK - 1::TOK_BLK]
    c_start = jnp.concatenate([jnp.zeros_like(c_end[:, :, :1]), c_end[:, :, :-1]], axis=-1)
    base = (c_start // BF16_SUBLANES) * BF16_SUBLANES
    passes = jnp.where(c_end > c_start, (c_end - base + WIN - 1) // WIN, 0)
    return jnp.swapaxes(base, 1, 2), jnp.max(passes, axis=1)


GATHER_DT = 1024


def _gather_body(base_s, npass_s, sel_ref, aff_ref, basec_ref, v_ref, xs_ref, gs_ref, *, cap):
    b = pl.program_id(0)
    ne, s = sel_ref.shape
    nk = s // TOK_BLK
    xs_ref[...] = jnp.zeros(xs_ref.shape, xs_ref.dtype)
    gs_ref[...] = jnp.zeros(gs_ref.shape, gs_ref.dtype)
    wiota = lax.broadcasted_iota(I32, (WIN, TOK_BLK), 0)
    passes = []
    for k in range(nk):
        toks = slice(k * TOK_BLK, (k + 1) * TOK_BLK)
        selk = sel_ref[:, toks]
        basek = basec_ref[k]

        def one_pass(p, carry, k=k, toks=toks, selk=selk, basek=basek):
            first = basek + p * WIN
            wbase = jnp.minimum(first, cap - WIN)
            rel = jnp.where(selk >= first, selk - wbase, -1)
            hits = [rel[e:e + 1, :] == wiota for e in range(ne)]
            onehot = jnp.concatenate([jnp.where(h, 1.0, 0.0) for h in hits], axis=0).astype(BF16)
            rows = jnp.dot(onehot, v_ref[toks, :], preferred_element_type=F32)
            offs = [pl.multiple_of(jnp.minimum(base_s[(b * nk + k) * ne + e] + p * WIN, cap - WIN), BF16_SUBLANES)
                    for e in range(ne)]
            for e in range(ne):
                xs_ref[e, pl.ds(offs[e], WIN), :] += rows[e * WIN:(e + 1) * WIN].astype(BF16)
                gate = jnp.sum(jnp.where(hits[e], aff_ref[e:e + 1, toks], 0.0), axis=1, keepdims=True)
                gs_ref[e, pl.ds(offs[e], WIN), :] += gate
            return carry

        passes.append(one_pass)

    for k in range(nk):
        passes[k](0, 0)
    for k in range(nk):
        lax.fori_loop(1, npass_s[b * nk + k], passes[k], 0)


def _gather(base, npass, sel, aff, v3, cap):
    nb, ne, s = sel.shape
    d = v3.shape[-1]
    nk = s // TOK_BLK
    grid_spec = pltpu.PrefetchScalarGridSpec(
        num_scalar_prefetch=2,
        grid=(nb, d // GATHER_DT),
        in_specs=[
            pl.BlockSpec((None, ne, s), lambda b, j, *_: (b, 0, 0)),
            pl.BlockSpec((None, ne, s), lambda b, j, *_: (b, 0, 0)),
            pl.BlockSpec((None, nk, ne, 1), lambda b, j, *_: (b, 0, 0, 0)),
            pl.BlockSpec((None, s, GATHER_DT), lambda b, j, *_: (b, 0, j)),
        ],
        out_specs=[
            pl.BlockSpec((ne, cap, GATHER_DT), lambda b, j, *_: (0, b, j)),
            pl.BlockSpec((ne, cap, 1), lambda b, j, *_: (0, b, 0)),
        ],
    )
    return pl.pallas_call(
        functools.partial(_gather_body, cap=cap),
        grid_spec=grid_spec,
        out_shape=[jax.ShapeDtypeStruct((ne, nb * cap, d), BF16),
                   jax.ShapeDtypeStruct((ne, nb * cap, 1), F32)],
        compiler_params=_params(("arbitrary", "arbitrary")),
        name="gather",
    )(base.reshape(-1), npass.reshape(-1), sel, aff, base[..., None], v3)


COMB_HSLOTS = 3


def _combine_body(base_s, npass_s, selc_ref, baser_ref, y_hbm, h_hbm, g_ref, o_ref, acc_ref, hbuf, ybuf, hsem, ysem,
                  *, cap, nb, nk):
    b = pl.program_id(0)
    k = pl.program_id(1)
    ne = ybuf.shape[1]
    per = ne // nk
    step = b * nk + k

    def h_copy(st):
        st = jnp.asarray(st, I32)
        rows = pl.ds(pl.multiple_of((st % nk) * TOK_BLK, TOK_BLK), TOK_BLK)
        slot = st % COMB_HSLOTS
        return pltpu.make_async_copy(h_hbm.at[st // nk, rows, :], hbuf.at[slot], hsem.at[slot])

    def y_copy(batch, j):
        batch = jnp.asarray(batch, I32)
        experts = pl.ds(j * per, per)
        rows = pl.ds(pl.multiple_of(batch * cap, cap), cap)
        return pltpu.make_async_copy(y_hbm.at[experts, rows, :], ybuf.at[batch % 2, experts], ysem.at[batch % 2, j])

    @pl.when(step == 0)
    def _prime():
        h_copy(0).start()
        h_copy(1).start()
        for j in range(nk):
            y_copy(0, j).start()

    @pl.when(step + 2 < nb * nk)
    def _():
        h_copy(step + 2).start()

    @pl.when(b + 1 < nb)
    def _():
        y_copy(b + 1, k).start()

    h_copy(step).wait()

    @pl.when(k == 0)
    def _():
        for j in range(nk):
            y_copy(b, j).wait()

    y_ref = ybuf.at[b % 2]
    acc_ref[...] = hbuf[step % COMB_HSLOTS]
    selk = selc_ref[...]
    basek = baser_ref[...]
    pairs = selk.shape[1]
    lane = lax.broadcasted_iota(I32, (pairs, ne * WIN), 1)
    spread = jnp.where(b * ne + lane // WIN == lax.broadcasted_iota(I32, (pairs, ne * WIN), 0), 1.0, 0.0).astype(BF16)
    wlane = (lax.broadcasted_iota(I32, (1, ne * WIN), 1) % WIN).astype(F32)

    def one_pass(p, carry):
        first = basek + p * WIN
        wbase = jnp.minimum(first, cap - WIN)
        rel = jnp.where(selk >= first, selk - wbase, -1)
        relx = jnp.dot(rel.astype(F32).astype(BF16), spread, preferred_element_type=F32)
        onehot = jnp.where(relx == wlane, 1.0, 0.0).astype(BF16)
        wins = []
        for e in range(ne):
            off = jnp.minimum(base_s[(b * nk + k) * ne + e] + p * WIN, cap - WIN)
            wins.append(y_ref[e, pl.ds(pl.multiple_of(off, BF16_SUBLANES), WIN), :])
        acc_ref[...] += jnp.dot(onehot, jnp.concatenate(wins, axis=0), preferred_element_type=F32)
        return carry

    one_pass(0, 0)
    lax.fori_loop(1, npass_s[b * nk + k], one_pass, 0)
    h = acc_ref[...]
    ms = jnp.mean(h * h, axis=-1, keepdims=True)
    o_ref[...] = h * lax.rsqrt(ms + EPS) * g_ref[...]


def _combine(base, npass, sel_rows, ys, h3, g_final, cap):
    nb, s, d = h3.shape
    ne = ys.shape[0]
    nk = s // TOK_BLK
    grid_spec = pltpu.PrefetchScalarGridSpec(
        num_scalar_prefetch=2,
        grid=(nb, nk),
        in_specs=[
            pl.BlockSpec((TOK_BLK, nb * ne), lambda b, k, *_: (k, 0)),
            pl.BlockSpec((None, 1, nb * ne), lambda b, k, *_: (k, 0, 0)),
            pl.BlockSpec(memory_space=pl.ANY),
            pl.BlockSpec(memory_space=pl.ANY),
            pl.BlockSpec((1, d), lambda b, k, *_: (0, 0)),
        ],
        out_specs=pl.BlockSpec((None, TOK_BLK, d), lambda b, k, *_: (b, k, 0)),
        scratch_shapes=[
            pltpu.VMEM((TOK_BLK, d), F32),
            pltpu.VMEM((COMB_HSLOTS, TOK_BLK, d), F32),
            pltpu.VMEM((2, ne, cap, d), BF16),
            pltpu.SemaphoreType.DMA((COMB_HSLOTS,)),
            pltpu.SemaphoreType.DMA((2, nk)),
        ],
    )
    assert ne % nk == 0 and nb * nk >= 2
    return pl.pallas_call(
        functools.partial(_combine_body, cap=cap, nb=nb, nk=nk),
        grid_spec=grid_spec,
        out_shape=jax.ShapeDtypeStruct((nb, s, d), F32),
        compiler_params=_params(("arbitrary", "arbitrary")),
        name="combine",
    )(base.reshape(-1), npass.reshape(-1), sel_rows, jnp.swapaxes(base, 0, 1).reshape(nk, 1, nb * ne), ys, h3, g_final)


FFN_TM = 512
FFN_STEPS = 8
FFN_NC = 512


def _ffn_body(xs_ref, gs_ref, wg_ref, wu_ref, wd_ref, y_ref, wgu_scr, wd_scr, hid_scr):
    e1 = pl.program_id(0)
    j = pl.program_id(1)
    ff = wg_ref.shape[1]
    nt = ff // LANES

    nc = y_ref.shape[1] // FFN_NC
    stage_slot = e1 % 2
    slot = (e1 - 1) % 2
    kr, fr = wg_ref.shape[0], wd_ref.shape[0]
    r0 = pl.multiple_of(j * kr, kr)
    f0 = pl.multiple_of(j * fr, BF16_SUBLANES)

    def stage_gate_up(t):
        src = slice(t * LANES, (t + 1) * LANES)
        wgu_scr[stage_slot, pl.ds(r0, kr), 2 * t * LANES:(2 * t + 1) * LANES] = wg_ref[:, src].astype(BF16)
        wgu_scr[stage_slot, pl.ds(r0, kr), (2 * t + 1) * LANES:(2 * t + 2) * LANES] = wu_ref[:, src].astype(BF16)

    def stage_down(c):
        cols = slice(c * FFN_NC, (c + 1) * FFN_NC)
        wd_scr[stage_slot, pl.ds(f0, fr), cols] = wd_ref[:, cols].astype(BF16)

    pieces = [functools.partial(stage_gate_up, t) for t in range(nt)] + [
        functools.partial(stage_down, c) for c in range(nc)]

    def spread(n_chunks):
        return [pieces[i::n_chunks] for i in range(n_chunks)]

    @pl.when(e1 == 0)
    def _load_first_expert():
        for piece in pieces:
            piece()

    @pl.when((e1 > 0) & (j % 2 == 0))
    def _gate_up():
        xs = xs_ref[...]
        for t, todo in enumerate(spread(nt)):
            for piece in todo:
                piece()
            res = jnp.dot(xs, wgu_scr[slot, :, 2 * t * LANES:(2 * t + 2) * LANES], preferred_element_type=F32)
            hid_scr[:, t * LANES:(t + 1) * LANES] = (jax.nn.silu(res[:, :LANES]) * res[:, LANES:]).astype(BF16)

    @pl.when((e1 > 0) & (j % 2 == 1))
    def _down():
        hid = hid_scr[...]
        for c, todo in enumerate(spread(nc)):
            for piece in todo:
                piece()
            cols = slice(c * FFN_NC, (c + 1) * FFN_NC)
            y = jnp.dot(hid, wd_scr[slot, :, cols], preferred_element_type=F32)
            y_ref[:, cols] = (y * gs_ref[...]).astype(BF16)


def _ffn(xs, gs, w_gate_e, w_up_e, w_down_e):
    ne, m, d = xs.shape
    ff = w_gate_e.shape[-1]
    assert m == FFN_TM * FFN_STEPS // 2 and d % FFN_STEPS == 0 and ff % (FFN_STEPS * BF16_SUBLANES) == 0
    assert ff % LANES == 0
    prev = lambda e1, j: (jnp.maximum(e1 - 1, 0), jnp.where(e1 == 0, 0, j // 2), 0)
    nxt = lambda e1, j: (jnp.minimum(e1, ne - 1), j, 0)
    return pl.pallas_call(
        _ffn_body,
        grid=(ne + 1, FFN_STEPS),
        in_specs=[
            pl.BlockSpec((None, FFN_TM, d), prev),
            pl.BlockSpec((None, FFN_TM, 1), prev),
            pl.BlockSpec((None, d // FFN_STEPS, ff), nxt),
            pl.BlockSpec((None, d // FFN_STEPS, ff), nxt),
            pl.BlockSpec((None, ff // FFN_STEPS, d), nxt),
        ],
        out_specs=pl.BlockSpec((None, FFN_TM, d), prev),
        out_shape=jax.ShapeDtypeStruct((ne, m, d), BF16),
        scratch_shapes=[pltpu.VMEM((2, d, 2 * ff), BF16), pltpu.VMEM((2, ff, d), BF16),
                        pltpu.VMEM((FFN_TM, ff), BF16)],
        compiler_params=_params(("arbitrary", "arbitrary")),
        name="ffn",
    )(xs, gs, w_gate_e, w_up_e, w_down_e)


TRIG_SPLIT = 32


def _trig_rows(n, rows, cols):
    ang = ((rows[:, None] * cols[None, :]) % n).astype(F32) * (2.0 * math.pi / n)
    return jnp.cos(ang), jnp.sin(ang)


def _dft_tables(n, rows, cols):
    nr = rows.shape[0]
    r1 = lax.iota(I32, nr // TRIG_SPLIT) * TRIG_SPLIT
    r0 = lax.iota(I32, TRIG_SPLIT)
    c1, s1 = _trig_rows(n, r1, cols)
    c0, s0 = _trig_rows(n, r0, cols)
    scale = 1.0 / math.sqrt(n)
    c1, s1 = (c1 * scale)[:, None, :], (s1 * scale)[:, None, :]
    c0, s0 = c0[None], s0[None]
    cos = (c1 * c0 - s1 * s0).reshape(nr, -1)
    sin = (s1 * c0 + c1 * s0).reshape(nr, -1)
    return cos, sin


def _pool_tables(s):
    i = lax.iota(I32, s)[:, None]
    k = (i // POOL_TM) * POOL_TM - POOL_HALO + lax.iota(I32, POOL_BAND)[None, :]
    out = []
    for half in POOL_HALF:
        lo = jnp.clip(i - half, 0, s)
        hi = jnp.clip(i + half, 0, s)
        inside = (k >= lo) & (k < hi)
        cnt = (hi - lo).astype(F32)
        out.append((jnp.where(inside, 1.0 / cnt, 0.0) - jnp.where(i == k, 1.0, 0.0)).astype(BF16))
    return jnp.stack(out)


def kernel(x, norm_mix_g, w_in, w_fourier_mix, w_pool_mix, pool_scale, w_branch_f, w_branch_p, w_gate,
           b_gate, w_out, norm_moe_g, w_router, w_expert_gate, w_expert_up, w_expert_down, norm_final_g):
    nb, s, d = x.shape
    assert w_in.shape[0] == 1, "single-layer block only"
    assert s % (2 * TOK_BLK) == 0 and max(POOL_HALF) <= POOL_HALO
    cap = CAPACITY_FACTOR * s // N_EXPERTS
    t = nb * s

    freqs = lax.iota(I32, s // DFT_RADIX)
    tabs = []
    for r in range(DFT_RADIX):
        cos_r, sin_r = _dft_tables(s, freqs, DFT_RADIX * freqs + r)
        tabs += [cos_r.astype(BF16), sin_r.astype(BF16)]
    trig = jnp.stack(tabs)
    chan = lax.iota(I32, GROUP)
    cos_c, sin_c = _trig_rows(GROUP, chan, chan)
    cos_c = (cos_c / math.sqrt(GROUP)).astype(BF16)
    sin_c = (sin_c / math.sqrt(GROUP)).astype(BF16)

    x2 = x.reshape(t, d)
    g_mix = norm_mix_g[0][:, None]
    pf, pp, gates, wout16 = _proj(
        x2, (g_mix * w_in[0]).astype(BF16), (g_mix * w_gate[0]).astype(BF16), b_gate[0][None], (w_out[0],))
    wbf16, wbp16 = _fold(w_fourier_mix[0], w_branch_f[0], w_pool_mix[0], pool_scale[0][None], w_branch_p[0])
    pc, ps, pooled = _seqmix(trig, _pool_tables(s), pf.reshape(nb, s // DFT_RADIX, -1), pp.reshape(nb, s, -1), s)
    wr = norm_moe_g[0][:, None] * w_router[0]
    wr_hi = wr.astype(BF16)
    wr_lo = (wr - wr_hi.astype(F32)).astype(BF16)
    h2, v, logits = _tail(
        pc.reshape(t, -1), ps.reshape(t, -1), pooled.reshape(t, -1), gates, x2, cos_c, sin_c, wbf16, wbp16, wout16,
        norm_moe_g[0][None], jnp.concatenate([wr_hi, wr_lo], axis=1))
    lg_t = jnp.swapaxes(logits.reshape(nb, s, N_EXPERTS), 1, 2)
    aff, sel, cum, sel_rows = _route(lg_t, cap)
    base, npass = _window_plan(cum, cap)
    xs, gs = _gather(base, npass, sel, aff, v.reshape(nb, s, d), cap)
    ys = _ffn(xs, gs, w_expert_gate[0], w_expert_up[0], w_expert_down[0])
    return _combine(base, npass, sel_rows, ys, h2.reshape(nb, s, d), norm_final_g[None], cap)
```
